```python
import jax, jax.numpy as jnp
from jax import lax
import numpy as np

D_MODEL = 2048
BATCH = 1
SEQ = 16384
DEPTH = 2
DEC_BATCH = 8
DEC_SEQ = 2048
PAST_LEN = 128

GRID_W = 64
HEAD_DIM = 128
ATTN_WIDTH = D_MODEL // 2
N_ATTN_HEADS = ATTN_WIDTH // HEAD_DIM
N_KV_HEADS = 2
KV_WIDTH = N_KV_HEADS * HEAD_DIM
CONV_WIDTH = D_MODEL // 4
CONV_KSIZE = 31
HGRN_WIDTH = D_MODEL // 4
HGRN_DK = 128
HGRN_DV = 128
HGRN_HEADS = HGRN_WIDTH // HGRN_DV
HGRN_FDIM = HGRN_HEADS * HGRN_DK
HGRN_CHUNK = 64
MIX_WIDTH = ATTN_WIDTH + CONV_WIDTH + HGRN_WIDTH
IN_SPLIT_SIZES = (ATTN_WIDTH, KV_WIDTH, KV_WIDTH, 2 * CONV_WIDTH, HGRN_FDIM, HGRN_FDIM, HGRN_FDIM, HGRN_WIDTH, HGRN_WIDTH)
IN_COLS = ATTN_WIDTH + 2 * KV_WIDTH + 2 * CONV_WIDTH + 3 * HGRN_FDIM + 2 * HGRN_WIDTH
FFN_HIDDEN = ((8 * D_MODEL // 3 + 255) // 256) * 256
PLE_DIM = 256
Q_BLOCK = 128
ROPE_THETA = 10000.0
NORM_EPS = 1e-6

kernel_name = 'hybrid_parallel_heads_bidir_encoder'


def rms_norm(x, g):
    xf = x.astype(jnp.float32)
    y = xf * lax.rsqrt(jnp.mean(xf * xf, axis=-1, keepdims=True) + NORM_EPS)
    return (y * g.astype(jnp.float32)).astype(x.dtype)


def axial_rope_tables(n):
    rows = n // GRID_W
    row = jnp.repeat(jnp.arange(rows, dtype=jnp.float32), GRID_W)
    col = jnp.tile(jnp.arange(GRID_W, dtype=jnp.float32), rows)
    axis_dims = HEAD_DIM // 2
    inv = ROPE_THETA ** (-jnp.arange(0, axis_dims, 2, dtype=jnp.float32) / axis_dims)
    ang = jnp.concatenate([row[:, None] * inv, col[:, None] * inv], axis=-1)
    return jnp.cos(ang), jnp.sin(ang)


def apply_rope(x, cos, sin):
    xf = x.astype(jnp.float32).reshape(x.shape[:-1] + (HEAD_DIM // 2, 2))
    x0, x1 = xf[..., 0], xf[..., 1]
    c = cos[None, :, None, :]
    s = sin[None, :, None, :]
    out = jnp.stack([x0 * c - x1 * s, x0 * s + x1 * c], axis=-1).reshape(x.shape)
    return out.astype(x.dtype)


def attention_mixer(q, k, v, q_gain, k_gain):
    b, n, _ = q.shape
    q = rms_norm(q.reshape(b, n, N_ATTN_HEADS, HEAD_DIM), q_gain)
    k = rms_norm(k.reshape(b, n, N_KV_HEADS, HEAD_DIM), k_gain)
    v = v.reshape(b, n, N_KV_HEADS, HEAD_DIM)
    cos, sin = axial_rope_tables(n)
    q = apply_rope(q, cos, sin)
    k = apply_rope(k, cos, sin)
    group = N_ATTN_HEADS // N_KV_HEADS
    nb = n // Q_BLOCK
    qb = q.reshape(b, nb, Q_BLOCK, N_KV_HEADS, group, HEAD_DIM).transpose(1, 0, 2, 3, 4, 5)
    scale = HEAD_DIM ** -0.5

    def block(qi):
        s = jnp.einsum('bqkgd,bskd->bkgqs', qi, k, preferred_element_type=jnp.float32) * scale
        p = jax.nn.softmax(s, axis=-1)
        return jnp.einsum('bkgqs,bskd->bqkgd', p.astype(v.dtype), v)

    o = lax.map(block, qb)
    return o.transpose(1, 0, 2, 3, 4, 5).reshape(b, n, ATTN_WIDTH)


def conv_mixer(u, dw, dw_b, ln_g, ln_b, pw):
    a, gate = jnp.split(u, 2, axis=-1)
    h = a * jax.nn.sigmoid(gate)
    h = lax.conv_general_dilated(h, dw[:, None, :], window_strides=(1,),
                                 padding=[(CONV_KSIZE // 2, CONV_KSIZE // 2)],
                                 dimension_numbers=('NWC', 'WIO', 'NWC'),
                                 feature_group_count=CONV_WIDTH) + dw_b
    hf = h.astype(jnp.float32)
    mu = jnp.mean(hf, axis=-1, keepdims=True)
    var = jnp.mean(jnp.square(hf - mu), axis=-1, keepdims=True)
    hf = (hf - mu) * lax.rsqrt(var + NORM_EPS) * ln_g.astype(jnp.float32) + ln_b.astype(jnp.float32)
    h = jax.nn.silu(hf).astype(u.dtype)
    return h @ pw


def hgrn2_scan(q, k, v, logf):
    b, h, n, dk = q.shape
    dv = v.shape[-1]
    nc = n // HGRN_CHUNK

    def to_chunks(t):
        return t.reshape(b, h, nc, HGRN_CHUNK, t.shape[-1]).transpose(2, 0, 1, 3, 4)

    qc, kc, vc, gc = to_chunks(q), to_chunks(k), to_chunks(v), to_chunks(logf)
    lower = jnp.tril(jnp.ones((HGRN_CHUNK, HGRN_CHUNK), dtype=bool))[:, :, None]

    def step(S, inp):
        qi, ki, vi, gi = inp
        bc = jnp.cumsum(gi, axis=2)
        diff = bc[:, :, :, None, :] - bc[:, :, None, :, :]
        decay = jnp.exp(jnp.where(lower, diff, -jnp.inf))
        A = jnp.einsum('bhtd,bhsd,bhtsd->bhts', qi, ki, decay)
        o = jnp.einsum('bhts,bhsv->bhtv', A, vi) + jnp.einsum('bhtd,bhdv->bhtv', qi * jnp.exp(bc), S)
        last = bc[:, :, -1:, :]
        S = jnp.exp(last[:, :, 0, :])[..., None] * S + jnp.einsum('bhsd,bhsv->bhdv', ki * jnp.exp(last - bc), vi)
        return S, o

    S0 = jnp.zeros((b, h, dk, dv), jnp.float32)
    _, o = lax.scan(step, S0, (qc, kc, vc, gc))
    return o.transpose(1, 2, 0, 3, 4).reshape(b, h, n, dv)


def hgrn2_mixer(hq, f_fwd, f_bwd, hi, hg, lb_fwd, lb_bwd, gn):
    b, n, _ = hq.shape

    def heads(t, d):
        return t.astype(jnp.float32).reshape(b, n, HGRN_HEADS, d).transpose(0, 2, 1, 3)

    qh = heads(jax.nn.silu(hq.astype(jnp.float32)), HGRN_DK)
    vh = heads(hi, HGRN_DV)

    def direction(f_logit, lb, reverse):
        f = lb.astype(jnp.float32) + (1.0 - lb.astype(jnp.float32)) * jax.nn.sigmoid(f_logit.astype(jnp.float32))
        fh = heads(f, HGRN_DK)
        kh, logf = 1.0 - fh, jnp.log(fh)
        if reverse:
            o = hgrn2_scan(jnp.flip(qh, 2), jnp.flip(kh, 2), jnp.flip(vh, 2), jnp.flip(logf, 2))
            return jnp.flip(o, 2)
        return hgrn2_scan(qh, kh, vh, logf)

    o = direction(f_fwd, lb_fwd, False) + direction(f_bwd, lb_bwd, True)
    o = o.transpose(0, 2, 1, 3)
    o = o * lax.rsqrt(jnp.mean(o * o, axis=-1, keepdims=True) + NORM_EPS)
    o = o.reshape(b, n, HGRN_WIDTH) * gn.astype(jnp.float32) * jax.nn.silu(hg.astype(jnp.float32))
    return o.astype(hq.dtype)


def encoder_layer(x, pe, lb_fwd, lb_bwd, norm_mix_pre, norm_mix_post, w_in, q_norm, k_norm,
                  conv_dw, conv_dw_b, conv_ln_g, conv_ln_b, conv_pw, hgrn_gn, w_out,
                  norm_ffn_pre, norm_ffn_post, w_ffn_in, w_ffn_out, w_ple_gate, w_ple_proj, ple_norm):
    u = rms_norm(x, norm_mix_pre) @ w_in
    points, acc = [], 0
    for sz in IN_SPLIT_SIZES[:-1]:
        acc += sz
        points.append(acc)
    q, k, v, c, hq, hff, hfb, hi, hg = jnp.split(u, points, axis=-1)
    a_out = attention_mixer(q, k, v, q_norm, k_norm)
    c_out = conv_mixer(c, conv_dw, conv_dw_b, conv_ln_g, conv_ln_b, conv_pw)
    h_out = hgrn2_mixer(hq, hff, hfb, hi, hg, lb_fwd, lb_bwd, hgrn_gn)
    mix = jnp.concatenate([a_out, c_out, h_out], axis=-1) @ w_out
    x = x + rms_norm(mix, norm_mix_post)
    gt, up = jnp.split(rms_norm(x, norm_ffn_pre) @ w_ffn_in, 2, axis=-1)
    x = x + rms_norm((jax.nn.silu(gt) * up) @ w_ffn_out, norm_ffn_post)
    gate = jax.nn.sigmoid(x @ w_ple_gate)
    x = x + rms_norm((pe @ w_ple_proj) * gate, ple_norm)
    return x


def run_trunk(x, p, lb, norm_mix_pre, norm_mix_post, w_in, q_norm, k_norm, conv_dw, conv_dw_b,
              conv_ln_g, conv_ln_b, conv_pw, hgrn_gn, w_out, norm_ffn_pre, norm_ffn_post,
              w_ffn_in, w_ffn_out, w_ple_gate, w_ple_proj, ple_norm):
    for l in range(DEPTH):
        x = encoder_layer(x, p[l], lb[l, 0], lb[l, 1], norm_mix_pre[l], norm_mix_post[l], w_in[l],
                          q_norm[l], k_norm[l], conv_dw[l], conv_dw_b[l], conv_ln_g[l], conv_ln_b[l],
                          conv_pw[l], hgrn_gn[l], w_out[l], norm_ffn_pre[l], norm_ffn_post[l],
                          w_ffn_in[l], w_ffn_out[l], w_ple_gate[l], w_ple_proj[l], ple_norm[l])
    return x


def setup_inputs(seed: int = 0) -> dict:
    key = jax.random.key(seed)
    ks = jax.random.split(key, 32)
    f32 = jnp.float32

    def nrm(k, shape, scale):
        return jax.random.normal(k, shape, f32) * scale

    def gain(k, shape):
        return 1.0 + 0.01 * jax.random.normal(k, shape, f32)

    L = DEPTH
    return {
        'x_prompt': nrm(ks[0], (BATCH, SEQ, D_MODEL), 1.0),
        'x_sample': nrm(ks[1], (DEC_BATCH, DEC_SEQ, D_MODEL), 1.0),
        'p_prompt': nrm(ks[2], (DEPTH, BATCH, SEQ, PLE_DIM), 1.0),
        'p_sample': nrm(ks[3], (DEPTH, DEC_BATCH, DEC_SEQ, PLE_DIM), 1.0),
        'norm_mix_pre': gain(ks[4], (L, D_MODEL)),
        'norm_mix_post': gain(ks[5], (L, D_MODEL)),
        'w_in': nrm(ks[6], (L, D_MODEL, IN_COLS), D_MODEL ** -0.5),
        'q_norm': gain(ks[7], (L, HEAD_DIM)),
        'k_norm': gain(ks[8], (L, HEAD_DIM)),
        'conv_dw': nrm(ks[9], (L, CONV_KSIZE, CONV_WIDTH), CONV_KSIZE ** -0.5),
        'conv_dw_b': nrm(ks[10], (L, CONV_WIDTH), 0.01),
        'conv_ln_g': gain(ks[11], (L, CONV_WIDTH)),
        'conv_ln_b': nrm(ks[12], (L, CONV_WIDTH), 0.01),
        'conv_pw': nrm(ks[13], (L, CONV_WIDTH, CONV_WIDTH), CONV_WIDTH ** -0.5),
        'hgrn_lb': nrm(ks[14], (L, 2, HGRN_FDIM), 0.1),
        'hgrn_gn': gain(ks[15], (L, HGRN_WIDTH)),
        'w_out': nrm(ks[16], (L, MIX_WIDTH, D_MODEL), MIX_WIDTH ** -0.5),
        'norm_ffn_pre': gain(ks[17], (L, D_MODEL)),
        'norm_ffn_post': gain(ks[18], (L, D_MODEL)),
        'w_ffn_in': nrm(ks[19], (L, D_MODEL, 2 * FFN_HIDDEN), D_MODEL ** -0.5),
        'w_ffn_out': nrm(ks[20], (L, FFN_HIDDEN, D_MODEL), FFN_HIDDEN ** -0.5),
        'w_ple_gate': nrm(ks[21], (L, D_MODEL, D_MODEL), D_MODEL ** -0.5),
        'w_ple_proj': nrm(ks[22], (L, PLE_DIM, D_MODEL), PLE_DIM ** -0.5),
        'ple_norm': gain(ks[23], (L, D_MODEL)),
    }


def reference(x_prompt, x_sample, p_prompt, p_sample, norm_mix_pre, norm_mix_post, w_in, q_norm, k_norm,
              conv_dw, conv_dw_b, conv_ln_g, conv_ln_b, conv_pw, hgrn_lb, hgrn_gn, w_out,
              norm_ffn_pre, norm_ffn_post, w_ffn_in, w_ffn_out, w_ple_gate, w_ple_proj, ple_norm):
    sm = jax.nn.softmax(hgrn_lb.astype(jnp.float32), axis=0)
    lb = jnp.cumsum(sm, axis=0) - sm[0]
    y_prompt = run_trunk(x_prompt, p_prompt, lb, norm_mix_pre, norm_mix_post, w_in, q_norm, k_norm,
                         conv_dw, conv_dw_b, conv_ln_g, conv_ln_b, conv_pw, hgrn_gn, w_out,
                         norm_ffn_pre, norm_ffn_post, w_ffn_in, w_ffn_out, w_ple_gate, w_ple_proj, ple_norm)
    y_sample = run_trunk(x_sample, p_sample, lb, norm_mix_pre, norm_mix_post, w_in, q_norm, k_norm,
                         conv_dw, conv_dw_b, conv_ln_g, conv_ln_b, conv_pw, hgrn_gn, w_out,
                         norm_ffn_pre, norm_ffn_post, w_ffn_in, w_ffn_out, w_ple_gate, w_ple_proj, ple_norm)
    return (y_prompt, y_sample)
```

```python
import functools

import numpy as np
import jax
import jax.numpy as jnp
from jax import lax
from jax.experimental import pallas as pl
from jax.experimental.pallas import tpu as pltpu

F32 = jnp.float32
BF16 = jnp.bfloat16

NORM_EPS = 1e-6
ROPE_THETA = 10000.0
GRID_W = 64
HEAD_DIM = 128
N_Q_HEADS = 8
N_KV_HEADS = 2
Q_GROUP = N_Q_HEADS // N_KV_HEADS
ATTN_W = N_Q_HEADS * HEAD_DIM
KV_W = N_KV_HEADS * HEAD_DIM
CONV_W = 512
CONV_K = 31
CONV_HALO = 16
HGRN_W = 512
HGRN_D = 128
HGRN_HEADS = HGRN_W // HGRN_D
HGRN_CHUNK = 128

OFF_Q = 0
OFF_K = OFF_Q + ATTN_W
OFF_V = OFF_K + KV_W
OFF_CA = OFF_V + KV_W
OFF_CG = OFF_CA + CONV_W
OFF_HQ = OFF_CG + CONV_W
OFF_FF = OFF_HQ + HGRN_W
OFF_FB = OFF_FF + HGRN_W
OFF_HI = OFF_FB + HGRN_W
OFF_HG = OFF_HI + HGRN_W
IN_COLS = OFF_HG + HGRN_W

V7X_VMEM_LIMIT_BYTES = 56 * 1024 * 1024


def _params(*semantics):
    return pltpu.CompilerParams(dimension_semantics=semantics, vmem_limit_bytes=V7X_VMEM_LIMIT_BYTES)


def _tile(dim, pref):
    t = min(dim, pref)
    assert dim % t == 0, (dim, pref)
    return t


def _rms(x, g):
    return x * lax.rsqrt(jnp.mean(x * x, axis=-1, keepdims=True) + NORM_EPS) * g


def _sigmoid(x):
    return 1.0 / (1.0 + jnp.exp(-x))


def _silu(x):
    return x * _sigmoid(x)


def _dot(a, b):
    return jnp.dot(a, b, preferred_element_type=F32)


def _dot_nt(a, b):
    return lax.dot_general(a, b, (((1,), (1,)), ((), ())), preferred_element_type=F32)


def _in_proj_kernel(x_ref, g_ref, w_ref, o_ref, xn_ref):
    @pl.when(pl.program_id(1) == 0)
    def _():
        xn_ref[...] = _rms(x_ref[...], g_ref[...]).astype(BF16)

    o_ref[...] = _dot(xn_ref[...], w_ref[...]).astype(o_ref.dtype)


def _in_proj(x, g, w):
    t, d = x.shape
    n = w.shape[1]
    tm, tn = _tile(t, 1024), _tile(n, 1024)
    return pl.pallas_call(
        _in_proj_kernel,
        grid=(t // tm, n // tn),
        in_specs=[
            pl.BlockSpec((tm, d), lambda i, j: (i, 0)),
            pl.BlockSpec((1, d), lambda i, j: (0, 0)),
            pl.BlockSpec((d, tn), lambda i, j: (0, j)),
        ],
        out_specs=pl.BlockSpec((tm, tn), lambda i, j: (i, j)),
        out_shape=jax.ShapeDtypeStruct((t, n), BF16),
        scratch_shapes=[pltpu.VMEM((tm, d), BF16)],
        compiler_params=_params("parallel", "arbitrary"),
        name="in_proj",
    )(x, g.reshape(1, d), w)


def _rope_tables(n):
    rows = n // GRID_W
    row = jnp.repeat(jnp.arange(rows, dtype=F32), GRID_W)
    col = jnp.tile(jnp.arange(GRID_W, dtype=F32), rows)
    axis_dims = HEAD_DIM // 2
    inv = ROPE_THETA ** (-jnp.arange(0, axis_dims, 2, dtype=F32) / axis_dims)
    ang = jnp.concatenate([row[:, None] * inv, col[:, None] * inv], axis=-1)
    cos, sin = jnp.cos(ang), jnp.sin(ang)
    cos_l = jnp.repeat(cos, 2, axis=-1)
    sin_l = jnp.stack([-sin, sin], axis=-1).reshape(n, HEAD_DIM)
    return cos_l, sin_l


def _attn_prep_kernel(q_ref, k_ref, cos_ref, sin_ref, qg_ref, kg_ref, qo_ref, ko_ref):
    cos = cos_ref[...]
    sin = sin_ref[...]
    lane = lax.broadcasted_iota(jnp.int32, cos.shape, 1)
    even = (lane & 1) == 0

    def norm_rope(x, g):
        y = _rms(x.astype(F32), g)
        partner = jnp.where(even, pltpu.roll(y, HEAD_DIM - 1, 1), pltpu.roll(y, 1, 1))
        return y * cos + partner * sin

    scale = HEAD_DIM ** -0.5
    for h in range(N_Q_HEADS):
        sl = slice(h * HEAD_DIM, (h + 1) * HEAD_DIM)
        qo_ref[:, sl] = (norm_rope(q_ref[:, sl], qg_ref[...]) * scale).astype(qo_ref.dtype)
    for h in range(N_KV_HEADS):
        sl = slice(h * HEAD_DIM, (h + 1) * HEAD_DIM)
        ko_ref[:, sl] = norm_rope(k_ref[:, sl], kg_ref[...]).astype(ko_ref.dtype)


def _attn_prep(u, cos_l, sin_l, q_gain, k_gain, n):
    t = u.shape[0]
    tm = _tile(n, 512)
    per_seq = n // tm
    return pl.pallas_call(
        _attn_prep_kernel,
        grid=(t // tm,),
        in_specs=[
            pl.BlockSpec((tm, ATTN_W), lambda i: (i, OFF_Q // ATTN_W)),
            pl.BlockSpec((tm, KV_W), lambda i: (i, OFF_K // KV_W)),
            pl.BlockSpec((tm, HEAD_DIM), lambda i: (i % per_seq, 0)),
            pl.BlockSpec((tm, HEAD_DIM), lambda i: (i % per_seq, 0)),
            pl.BlockSpec((1, HEAD_DIM), lambda i: (0, 0)),
            pl.BlockSpec((1, HEAD_DIM), lambda i: (0, 0)),
        ],
        out_specs=[
            pl.BlockSpec((tm, ATTN_W), lambda i: (i, 0)),
            pl.BlockSpec((tm, KV_W), lambda i: (i, 0)),
        ],
        out_shape=[
            jax.ShapeDtypeStruct((t, ATTN_W), BF16),
            jax.ShapeDtypeStruct((t, KV_W), BF16),
        ],
        compiler_params=_params("parallel"),
        name="attn_prep",
    )(u, u, cos_l, sin_l, q_gain.reshape(1, HEAD_DIM), k_gain.reshape(1, HEAD_DIM))


def _flash_kernel(q_ref, k_ref, v_ref, o_ref, m_ref, l_ref, acc_ref, *, tq):
    ki = pl.program_id(3)

    @pl.when(ki == 0)
    def _():
        m_ref[...] = jnp.full_like(m_ref, -jnp.inf)
        l_ref[...] = jnp.zeros_like(l_ref)
        acc_ref[...] = jnp.zeros_like(acc_ref)

    q = jnp.concatenate([q_ref[:, g * HEAD_DIM:(g + 1) * HEAD_DIM] for g in range(Q_GROUP)], axis=0)
    s = _dot_nt(q, k_ref[...])
    m_prev = m_ref[...]
    m_new = jnp.maximum(m_prev, jnp.max(s, axis=-1, keepdims=True))
    alpha = jnp.exp(m_prev - m_new)
    p = jnp.exp(s - m_new)
    l_ref[...] = alpha * l_ref[...] + jnp.sum(p, axis=-1, keepdims=True)
    acc_ref[...] = alpha * acc_ref[...] + _dot(p.astype(BF16), v_ref[...])
    m_ref[...] = m_new

    @pl.when(ki == pl.num_programs(3) - 1)
    def _():
        out = acc_ref[...] / l_ref[...]
        for g in range(Q_GROUP):
            o_ref[:, g * HEAD_DIM:(g + 1) * HEAD_DIM] = out[g * tq:(g + 1) * tq].astype(o_ref.dtype)


def _flash(q, k, u, batch, n):
    t = q.shape[0]
    tq, tk = _tile(n, 256), _tile(n, 512)
    nq, nk = n // tq, n // tk
    gw = Q_GROUP * HEAD_DIM
    rows = Q_GROUP * tq
    return pl.pallas_call(
        functools.partial(_flash_kernel, tq=tq),
        grid=(batch, N_KV_HEADS, nq, nk),
        in_specs=[
            pl.BlockSpec((tq, gw), lambda b, h, qi, ki: (b * nq + qi, h)),
            pl.BlockSpec((tk, HEAD_DIM), lambda b, h, qi, ki: (b * nk + ki, h)),
            pl.BlockSpec((tk, HEAD_DIM), lambda b, h, qi, ki: (b * nk + ki, OFF_V // HEAD_DIM + h)),
        ],
        out_specs=pl.BlockSpec((tq, gw), lambda b, h, qi, ki: (b * nq + qi, h)),
        out_shape=jax.ShapeDtypeStruct((t, ATTN_W), BF16),
        scratch_shapes=[
            pltpu.VMEM((rows, 1), F32),
            pltpu.VMEM((rows, 1), F32),
            pltpu.VMEM((rows, HEAD_DIM), F32),
        ],
        compiler_params=_params("parallel", "parallel", "parallel", "arbitrary"),
        name="flash_attn",
    )(q, k, u)


def _conv_kernel(a_ref, g_ref, ap_ref, gp_ref, an_ref, gn_ref, dw_ref, dwb_ref, lng_ref, lnb_ref, pw_ref,
                 o_ref, h_ref, *, tm, per_seq):
    i = pl.program_id(0)
    pos = i % per_seq

    def glu(a, g):
        return a.astype(F32) * _sigmoid(g.astype(F32))

    prev = jnp.where(pos == 0, 0.0, glu(ap_ref[...], gp_ref[...]))
    nxt = jnp.where(pos == per_seq - 1, 0.0, glu(an_ref[...], gn_ref[...]))
    h_ref[0:CONV_HALO, :] = prev
    h_ref[CONV_HALO:CONV_HALO + tm, :] = glu(a_ref[...], g_ref[...])
    h_ref[CONV_HALO + tm:, :] = nxt

    base = CONV_HALO - CONV_K // 2
    acc = jnp.zeros((tm, CONV_W), F32) + dwb_ref[...]
    for j in range(CONV_K):
        acc = acc + dw_ref[j:j + 1, :] * h_ref[base + j:base + j + tm, :]
    mu = jnp.mean(acc, axis=-1, keepdims=True)
    cen = acc - mu
    var = jnp.mean(cen * cen, axis=-1, keepdims=True)
    y = cen * lax.rsqrt(var + NORM_EPS) * lng_ref[...] + lnb_ref[...]
    o_ref[...] = _dot(_silu(y).astype(BF16), pw_ref[...]).astype(o_ref.dtype)


def _conv(u, dw, dwb, lng, lnb, pw, n):
    t = u.shape[0]
    tm = _tile(n, 512)
    per_seq = n // tm
    hb = tm // CONV_HALO
    last = t // CONV_HALO - 1
    ca, cg = OFF_CA // CONV_W, OFF_CG // CONV_W
    vec = lambda v: v.reshape(1, CONV_W)
    row = pl.BlockSpec((1, CONV_W), lambda i: (0, 0))
    return pl.pallas_call(
        functools.partial(_conv_kernel, tm=tm, per_seq=per_seq),
        grid=(t // tm,),
        in_specs=[
            pl.BlockSpec((tm, CONV_W), lambda i: (i, ca)),
            pl.BlockSpec((tm, CONV_W), lambda i: (i, cg)),
            pl.BlockSpec((CONV_HALO, CONV_W), lambda i: (jnp.maximum(i * hb - 1, 0), ca)),
            pl.BlockSpec((CONV_HALO, CONV_W), lambda i: (jnp.maximum(i * hb - 1, 0), cg)),
            pl.BlockSpec((CONV_HALO, CONV_W), lambda i: (jnp.minimum((i + 1) * hb, last), ca)),
            pl.BlockSpec((CONV_HALO, CONV_W), lambda i: (jnp.minimum((i + 1) * hb, last), cg)),
            pl.BlockSpec((CONV_K, CONV_W), lambda i: (0, 0)),
            row, row, row,
            pl.BlockSpec((CONV_W, CONV_W), lambda i: (0, 0)),
        ],
        out_specs=pl.BlockSpec((tm, CONV_W), lambda i: (i, 0)),
        out_shape=jax.ShapeDtypeStruct((t, CONV_W), BF16),
        scratch_shapes=[pltpu.VMEM((tm + 2 * CONV_HALO, CONV_W), F32)],
        compiler_params=_params("parallel"),
        name="conv_mixer",
    )(u, u, u, u, u, u, dw, vec(dwb), vec(lng), vec(lnb), pw)


def _hgrn_tables(reverse):
    c = HGRN_CHUNK
    levels = []
    h = c // 2
    while h >= 1:
        levels.append(h)
        h //= 2
    r = np.arange(c)
    col = r[None, :]
    seg = np.zeros((len(levels) + 2, c, c), np.float32)
    mask = np.zeros((len(levels) + 1, c, c), np.float32)
    for li, h in enumerate(levels):
        blk = r // (2 * h)
        upper = (r % (2 * h)) >= h
        b = (blk * 2 * h + h - 1)[:, None]
        rr = r[:, None]
        seg[li] = np.where(upper[:, None], (col > b) & (col <= rr), (col > rr) & (col <= b))
        mask[li] = upper[:, None] & (~upper)[None, :] & (blk[:, None] == blk[None, :])
    seg[-2] = col <= r[:, None]
    seg[-1] = col > r[:, None]
    mask[-1] = np.eye(c)
    if reverse:
        seg = seg[:, ::-1, ::-1]
        mask = mask[:, ::-1, ::-1]
    return (jnp.asarray(seg.reshape(-1, c), dtype=BF16), jnp.asarray(mask, dtype=F32), len(levels))


def _hgrn_kernel(*refs, reverse, final, n_levels):
    if final:
        hq_ref, hf_ref, hi_ref, lb_ref, seg_ref, mask_ref, oprev_ref, hg_ref, gn_ref, o_ref, st_ref = refs
    else:
        hq_ref, hf_ref, hi_ref, lb_ref, seg_ref, mask_ref, o_ref, st_ref = refs
    c = HGRN_CHUNK

    @pl.when(pl.program_id(1) == 0)
    def _():
        st_ref[...] = jnp.zeros_like(st_ref)

    lb = lb_ref[...]
    f = lb + (1.0 - lb) * _sigmoid(hf_ref[...].astype(F32))
    logf = jnp.log(f)
    kk = 1.0 - f
    qq = _silu(hq_ref[...].astype(F32))
    vv = hi_ref[...]

    g_hi = logf.astype(BF16)
    g_lo = (logf - g_hi.astype(F32)).astype(BF16)
    seg = seg_ref[...]
    e_all = jnp.exp(_dot(seg, g_hi) + _dot(seg, g_lo))

    total_row = 0 if reverse else c - 1
    outs = []
    for hd in range(HGRN_HEADS):
        sl = slice(hd * HGRN_D, (hd + 1) * HGRN_D)
        qh, kh, vh = qq[:, sl], kk[:, sl], vv[:, sl]
        a = mask_ref[n_levels] * _dot_nt(qh.astype(BF16), kh.astype(BF16))
        for l in range(n_levels):
            el = e_all[l * c:(l + 1) * c, sl]
            a = a + mask_ref[l] * _dot_nt((qh * el).astype(BF16), (kh * el).astype(BF16))
        e_pre = e_all[n_levels * c:(n_levels + 1) * c, sl]
        e_suf = e_all[(n_levels + 1) * c:(n_levels + 2) * c, sl]
        st = st_ref[hd]
        o = _dot(a.astype(BF16), vh) + _dot_nt((qh * e_pre).astype(BF16), st.astype(BF16))
        v_t = vh.astype(F32).T.astype(BF16)
        st_ref[hd] = e_pre[total_row:total_row + 1, :] * st + _dot(v_t, (kh * e_suf).astype(BF16))
        outs.append(o)

    if not final:
        for hd in range(HGRN_HEADS):
            o_ref[:, hd * HGRN_D:(hd + 1) * HGRN_D] = outs[hd]
    else:
        gate = _silu(hg_ref[...].astype(F32)) * gn_ref[...]
        for hd in range(HGRN_HEADS):
            sl = slice(hd * HGRN_D, (hd + 1) * HGRN_D)
            o = outs[hd] + oprev_ref[:, sl]
            o = o * lax.rsqrt(jnp.mean(o * o, axis=-1, keepdims=True) + NORM_EPS)
            o_ref[:, sl] = (o * gate[:, sl]).astype(o_ref.dtype)


def _hgrn_pass(u, lb, batch, n, *, reverse, o_prev=None, gn=None):
    t = u.shape[0]
    c = HGRN_CHUNK
    nc = n // c
    final = o_prev is not None
    seg, mask, n_levels = _hgrn_tables(reverse)

    def rows(b, ci):
        return b * nc + (nc - 1 - ci if reverse else ci)

    def col_spec(off):
        return pl.BlockSpec((c, HGRN_W), lambda b, ci: (rows(b, ci), off // HGRN_W))

    const2 = lambda b, ci: (0, 0)
    in_specs = [
        col_spec(OFF_HQ),
        col_spec(OFF_FB if reverse else OFF_FF),
        col_spec(OFF_HI),
        pl.BlockSpec((1, HGRN_W), const2),
        pl.BlockSpec(seg.shape, const2),
        pl.BlockSpec(mask.shape, lambda b, ci: (0, 0, 0)),
    ]
    args = [u, u, u, lb.reshape(1, HGRN_W), seg, mask]
    if final:
        in_specs += [
            pl.BlockSpec((c, HGRN_W), lambda b, ci: (rows(b, ci), 0)),
            col_spec(OFF_HG),
            pl.BlockSpec((1, HGRN_W), const2),
        ]
        args += [o_prev, u, gn.reshape(1, HGRN_W)]
    return pl.pallas_call(
        functools.partial(_hgrn_kernel, reverse=reverse, final=final, n_levels=n_levels),
        grid=(batch, nc),
        in_specs=in_specs,
        out_specs=pl.BlockSpec((c, HGRN_W), lambda b, ci: (rows(b, ci), 0)),
        out_shape=jax.ShapeDtypeStruct((t, HGRN_W), BF16 if final else F32),
        scratch_shapes=[pltpu.VMEM((HGRN_HEADS, HGRN_D, HGRN_D), F32)],
        compiler_params=_params("parallel", "arbitrary"),
        name="hgrn_bwd" if reverse else "hgrn_fwd",
    )(*args)


def _out_proj_kernel(a_ref, c_ref, h_ref, x_ref, w_ref, g_ref, o_ref):
    mix = _dot(a_ref[...], w_ref[0:ATTN_W, :])
    mix = mix + _dot(c_ref[...], w_ref[ATTN_W:ATTN_W + CONV_W, :])
    mix = mix + _dot(h_ref[...], w_ref[ATTN_W + CONV_W:, :])
    o_ref[...] = x_ref[...] + _rms(mix, g_ref[...])


def _out_proj(a, cv, hg, x, w, g):
    t, d = x.shape
    tm = _tile(t, 512)
    mixw = w.shape[0]
    return pl.pallas_call(
        _out_proj_kernel,
        grid=(t // tm,),
        in_specs=[
            pl.BlockSpec((tm, ATTN_W), lambda i: (i, 0)),
            pl.BlockSpec((tm, CONV_W), lambda i: (i, 0)),
            pl.BlockSpec((tm, HGRN_W), lambda i: (i, 0)),
            pl.BlockSpec((tm, d), lambda i: (i, 0)),
            pl.BlockSpec((mixw, d), lambda i: (0, 0)),
            pl.BlockSpec((1, d), lambda i: (0, 0)),
        ],
        out_specs=pl.BlockSpec((tm, d), lambda i: (i, 0)),
        out_shape=jax.ShapeDtypeStruct((t, d), F32),
        compiler_params=_params("parallel"),
        name="out_proj",
    )(a, cv, hg, x, w, g.reshape(1, d))


def _ffn_kernel(x_ref, gpre_ref, wg_ref, wu_ref, wo_ref, gpost_ref, o_ref, xn_ref, acc_ref):
    j = pl.program_id(1)

    @pl.when(j == 0)
    def _():
        xn_ref[...] = _rms(x_ref[...], gpre_ref[...]).astype(BF16)
        acc_ref[...] = jnp.zeros_like(acc_ref)

    xn = xn_ref[...]
    hidden = _silu(_dot(xn, wg_ref[...])) * _dot(xn, wu_ref[...])
    acc_ref[...] += _dot(hidden.astype(BF16), wo_ref[...])

    @pl.when(j == pl.num_programs(1) - 1)
    def _():
        o_ref[...] = x_ref[...] + _rms(acc_ref[...], gpost_ref[...])


def _ffn(x, gpre, w_in, w_out, gpost):
    t, d = x.shape
    f = w_out.shape[0]
    tm, th = _tile(t, 512), _tile(f, 512)
    nh = f // th
    return pl.pallas_call(
        _ffn_kernel,
        grid=(t // tm, nh),
        in_specs=[
            pl.BlockSpec((tm, d), lambda i, j: (i, 0)),
            pl.BlockSpec((1, d), lambda i, j: (0, 0)),
            pl.BlockSpec((d, th), lambda i, j: (0, j)),
            pl.BlockSpec((d, th), lambda i, j: (0, nh + j)),
            pl.BlockSpec((th, d), lambda i, j: (j, 0)),
            pl.BlockSpec((1, d), lambda i, j: (0, 0)),
        ],
        out_specs=pl.BlockSpec((tm, d), lambda i, j: (i, 0)),
        out_shape=jax.ShapeDtypeStruct((t, d), F32),
        scratch_shapes=[pltpu.VMEM((tm, d), BF16), pltpu.VMEM((tm, d), F32)],
        compiler_params=_params("parallel", "arbitrary"),
        name="ffn",
    )(x, gpre.reshape(1, d), w_in, w_in, w_out, gpost.reshape(1, d))


def _ple_kernel(x_ref, p_ref, wg_ref, wp_ref, g_ref, o_ref):
    x = x_ref[...]
    gate = _sigmoid(_dot(x.astype(BF16), wg_ref[...]))
    proj = _dot(p_ref[...].astype(BF16), wp_ref[...])
    o_ref[...] = x + _rms(proj * gate, g_ref[...])


def _ple(x, p, layer, w_gate, w_proj, g):
    t, d = x.shape
    pd = p.shape[-1]
    tm = _tile(t, 512)
    return pl.pallas_call(
        _ple_kernel,
        grid=(t // tm,),
        in_specs=[
            pl.BlockSpec((tm, d), lambda i: (i, 0)),
            pl.BlockSpec((None, tm, pd), lambda i: (layer, i, 0)),
            pl.BlockSpec((d, d), lambda i: (0, 0)),
            pl.BlockSpec((pd, d), lambda i: (0, 0)),
            pl.BlockSpec((1, d), lambda i: (0, 0)),
        ],
        out_specs=pl.BlockSpec((tm, d), lambda i: (i, 0)),
        out_shape=jax.ShapeDtypeStruct((t, d), F32),
        compiler_params=_params("parallel"),
        name="ple",
    )(x, p, w_gate, w_proj, g.reshape(1, d))


def _trunk(x, p, lb, wts):
    depth = p.shape[0]
    batch, n, d = x.shape
    t = batch * n
    x = x.reshape(t, d)
    p = p.reshape(depth, t, p.shape[-1])
    cos_l, sin_l = _rope_tables(n)
    for l in range(depth):
        w = {k: v[l] for k, v in wts.items()}
        u = _in_proj(x, w["norm_mix_pre"], w["w_in"])
        q_r, k_r = _attn_prep(u, cos_l, sin_l, w["q_norm"], w["k_norm"], n)
        a_out = _flash(q_r, k_r, u, batch, n)
        c_out = _conv(u, w["conv_dw"], w["conv_dw_b"], w["conv_ln_g"], w["conv_ln_b"], w["conv_pw"], n)
        o_fwd = _hgrn_pass(u, lb[l, 0], batch, n, reverse=False)
        h_out = _hgrn_pass(u, lb[l, 1], batch, n, reverse=True, o_prev=o_fwd, gn=w["hgrn_gn"])
        x = _out_proj(a_out, c_out, h_out, x, w["w_out"], w["norm_mix_post"])
        x = _ffn(x, w["norm_ffn_pre"], w["w_ffn_in"], w["w_ffn_out"], w["norm_ffn_post"])
        x = _ple(x, p, l, w["w_ple_gate"], w["w_ple_proj"], w["ple_norm"])
    return x.reshape(batch, n, d)


def kernel(x_prompt, x_sample, p_prompt, p_sample, norm_mix_pre, norm_mix_post, w_in, q_norm, k_norm, conv_dw, conv_dw_b, conv_ln_g, conv_ln_b, conv_pw, hgrn_lb, hgrn_gn, w_out, norm_ffn_pre, norm_ffn_post, w_ffn_in, w_ffn_out, w_ple_gate, w_ple_proj, ple_norm):
    sm = jax.nn.softmax(hgrn_lb.astype(F32), axis=0)
    lb = jnp.cumsum(sm, axis=0) - sm[0]
    wts = dict(
        norm_mix_pre=norm_mix_pre, norm_mix_post=norm_mix_post, w_in=w_in.astype(BF16),
        q_norm=q_norm, k_norm=k_norm, conv_dw=conv_dw, conv_dw_b=conv_dw_b,
        conv_ln_g=conv_ln_g, conv_ln_b=conv_ln_b, conv_pw=conv_pw.astype(BF16),
        hgrn_gn=hgrn_gn, w_out=w_out.astype(BF16), norm_ffn_pre=norm_ffn_pre,
        norm_ffn_post=norm_ffn_post, w_ffn_in=w_ffn_in.astype(BF16), w_ffn_out=w_ffn_out.astype(BF16),
        w_ple_gate=w_ple_gate.astype(BF16), w_ple_proj=w_ple_proj.astype(BF16), ple_norm=ple_norm,
    )
    y_prompt = _trunk(x_prompt, p_prompt, lb, wts)
    y_sample = _trunk(x_sample, p_sample, lb, wts)
    return (y_prompt, y_sample)
```

```python
import functools

import numpy as np
import jax
import jax.numpy as jnp
from jax import lax
from jax.experimental import pallas as pl
from jax.experimental.pallas import tpu as pltpu

F32 = jnp.float32
BF16 = jnp.bfloat16

NORM_EPS = 1e-6
ROPE_THETA = 10000.0
LOG2_E = 1.4426950408889634
GRID_W = 64
HEAD_DIM = 128
N_Q_HEADS = 8
N_KV_HEADS = 2
Q_GROUP = N_Q_HEADS // N_KV_HEADS
ATTN_W = N_Q_HEADS * HEAD_DIM
KV_W = N_KV_HEADS * HEAD_DIM
CONV_W = 512
CONV_K = 31
CONV_HALO = 16
HGRN_W = 512
HGRN_D = 128
HGRN_HEADS = HGRN_W // HGRN_D
HGRN_CHUNK = 128
FLASH_ONES_ROWS = 16
FLASH_LOOKAHEAD = 3

OFF_Q = 0
OFF_K = OFF_Q + ATTN_W
OFF_V = OFF_K + KV_W
OFF_CA = OFF_V + KV_W
OFF_CG = OFF_CA + CONV_W
OFF_HQ = OFF_CG + CONV_W
OFF_FF = OFF_HQ + HGRN_W
OFF_FB = OFF_FF + HGRN_W
OFF_HI = OFF_FB + HGRN_W
OFF_HG = OFF_HI + HGRN_W
IN_COLS = OFF_HG + HGRN_W

V7X_VMEM_LIMIT_BYTES = 56 * 1024 * 1024


def _params(*semantics):
    return pltpu.CompilerParams(dimension_semantics=semantics, vmem_limit_bytes=V7X_VMEM_LIMIT_BYTES)


def _tile(dim, pref):
    t = min(dim, pref)
    assert dim % t == 0, (dim, pref)
    return t


def _rms(x, g):
    return x * lax.rsqrt(jnp.mean(x * x, axis=-1, keepdims=True) + NORM_EPS) * g


def _sigmoid(x):
    return 1.0 / (1.0 + jnp.exp(-x))


def _silu(x):
    return x * _sigmoid(x)


def _dot(a, b):
    return jnp.dot(a, b, preferred_element_type=F32)


def _dot_nt(a, b):
    return lax.dot_general(a, b, (((1,), (1,)), ((), ())), preferred_element_type=F32)


def _in_proj_kernel(x_ref, g_ref, w_ref, o_ref, xn_ref):
    @pl.when(pl.program_id(1) == 0)
    def _():
        xn_ref[...] = _rms(x_ref[...], g_ref[...]).astype(BF16)

    o_ref[...] = _dot(xn_ref[...], w_ref[...]).astype(o_ref.dtype)


def _in_proj(x, g, w):
    t, d = x.shape
    n = w.shape[1]
    tm, tn = _tile(t, 1024), _tile(n, 1024)
    return pl.pallas_call(
        _in_proj_kernel,
        grid=(t // tm, n // tn),
        in_specs=[
            pl.BlockSpec((tm, d), lambda i, j: (i, 0)),
            pl.BlockSpec((1, d), lambda i, j: (0, 0)),
            pl.BlockSpec((d, tn), lambda i, j: (0, j)),
        ],
        out_specs=pl.BlockSpec((tm, tn), lambda i, j: (i, j)),
        out_shape=jax.ShapeDtypeStruct((t, n), BF16),
        scratch_shapes=[pltpu.VMEM((tm, d), BF16)],
        compiler_params=_params("parallel", "arbitrary"),
        name="in_proj",
    )(x, g.reshape(1, d), w)


def _rope_tables(n):
    rows = n // GRID_W
    row = jnp.repeat(jnp.arange(rows, dtype=F32), GRID_W)
    col = jnp.tile(jnp.arange(GRID_W, dtype=F32), rows)
    axis_dims = HEAD_DIM // 2
    inv = ROPE_THETA ** (-jnp.arange(0, axis_dims, 2, dtype=F32) / axis_dims)
    ang = jnp.concatenate([row[:, None] * inv, col[:, None] * inv], axis=-1)
    cos, sin = jnp.cos(ang), jnp.sin(ang)
    cos_l = jnp.repeat(cos, 2, axis=-1)
    sin_l = jnp.stack([-sin, sin], axis=-1).reshape(n, HEAD_DIM)
    return cos_l, sin_l


def _attn_prep_kernel(q_ref, k_ref, v_ref, cos_ref, sin_ref, qg_ref, kg_ref, qo_ref, ko_ref, vt_ref):
    cos = cos_ref[...]
    sin = sin_ref[...]
    lane = lax.broadcasted_iota(jnp.int32, cos.shape, 1)
    even = (lane & 1) == 0

    def norm_rope(x, g):
        y = _rms(x.astype(F32), g)
        partner = jnp.where(even, pltpu.roll(y, HEAD_DIM - 1, 1), pltpu.roll(y, 1, 1))
        return y * cos + partner * sin

    scale = HEAD_DIM ** -0.5 * LOG2_E
    for h in range(N_Q_HEADS):
        sl = slice(h * HEAD_DIM, (h + 1) * HEAD_DIM)
        qo_ref[:, sl] = (norm_rope(q_ref[:, sl], qg_ref[...]) * scale).astype(qo_ref.dtype)
    for h in range(N_KV_HEADS):
        sl = slice(h * HEAD_DIM, (h + 1) * HEAD_DIM)
        ko_ref[:, sl] = norm_rope(k_ref[:, sl], kg_ref[...]).astype(ko_ref.dtype)
    vt_ref[...] = v_ref[...].astype(F32).T.astype(vt_ref.dtype)


def _attn_prep(u, cos_l, sin_l, q_gain, k_gain, n):
    t = u.shape[0]
    tm = _tile(n, 512)
    per_seq = n // tm
    return pl.pallas_call(
        _attn_prep_kernel,
        grid=(t // tm,),
        in_specs=[
            pl.BlockSpec((tm, ATTN_W), lambda i: (i, OFF_Q // ATTN_W)),
            pl.BlockSpec((tm, KV_W), lambda i: (i, OFF_K // KV_W)),
            pl.BlockSpec((tm, KV_W), lambda i: (i, OFF_V // KV_W)),
            pl.BlockSpec((tm, HEAD_DIM), lambda i: (i % per_seq, 0)),
            pl.BlockSpec((tm, HEAD_DIM), lambda i: (i % per_seq, 0)),
            pl.BlockSpec((1, HEAD_DIM), lambda i: (0, 0)),
            pl.BlockSpec((1, HEAD_DIM), lambda i: (0, 0)),
        ],
        out_specs=[
            pl.BlockSpec((tm, ATTN_W), lambda i: (i, 0)),
            pl.BlockSpec((tm, KV_W), lambda i: (i, 0)),
            pl.BlockSpec((KV_W, tm), lambda i: (0, i)),
        ],
        out_shape=[
            jax.ShapeDtypeStruct((t, ATTN_W), BF16),
            jax.ShapeDtypeStruct((t, KV_W), BF16),
            jax.ShapeDtypeStruct((KV_W, t), BF16),
        ],
        compiler_params=_params("parallel"),
        name="attn_prep",
    )(u, u, u, cos_l, sin_l, q_gain.reshape(1, HEAD_DIM), k_gain.reshape(1, HEAD_DIM))


def _flash_kernel(q_ref, k_ref, vt_ref, o_ref, m_ref, acc_ref, *, tk, sub):
    ki = pl.program_id(3)

    @pl.when(ki == 0)
    def _():
        m_ref[...] = jnp.full_like(m_ref, -jnp.inf)
        acc_ref[...] = jnp.zeros_like(acc_ref)

    stages = [(s0, g) for s0 in range(0, tk, sub) for g in range(Q_GROUP)]

    def scores(stage):
        s0, g = stage
        return _dot_nt(k_ref[s0:s0 + sub, :], q_ref[:, g * HEAD_DIM:(g + 1) * HEAD_DIM])

    ones = jnp.ones((FLASH_ONES_ROWS, sub), BF16)

    pending = [scores(st) for st in stages[:FLASH_LOOKAHEAD]]
    for i, (s0, g) in enumerate(stages):
        s = pending.pop(0)
        if i + FLASH_LOOKAHEAD < len(stages):
            pending.append(scores(stages[i + FLASH_LOOKAHEAD]))
        m_prev = m_ref[g]
        m_new = jnp.maximum(m_prev, jnp.max(s, axis=0, keepdims=True))
        alpha = jnp.exp2(m_prev - m_new)
        p = jnp.exp2((s - m_new).astype(BF16))
        vt1 = jnp.concatenate([vt_ref[:, s0:s0 + sub], ones], axis=0)
        acc_ref[g] = alpha * acc_ref[g] + _dot(vt1, p)
        m_ref[g] = m_new

    @pl.when(ki == pl.num_programs(3) - 1)
    def _():
        for g in range(Q_GROUP):
            out = acc_ref[g, 0:HEAD_DIM, :] / acc_ref[g, HEAD_DIM:HEAD_DIM + 1, :]
            o_ref[:, g * HEAD_DIM:(g + 1) * HEAD_DIM] = out.T.astype(o_ref.dtype)


def _flash(q, k, vt, batch, n):
    t = q.shape[0]
    tq, tk = _tile(n, 256), _tile(n, 4096)
    sub = _tile(tk, 512)
    nq, nk = n // tq, n // tk
    gw = Q_GROUP * HEAD_DIM
    return pl.pallas_call(
        functools.partial(_flash_kernel, tk=tk, sub=sub),
        grid=(batch, N_KV_HEADS, nq, nk),
        in_specs=[
            pl.BlockSpec((tq, gw), lambda b, h, qi, ki: (b * nq + qi, h)),
            pl.BlockSpec((tk, HEAD_DIM), lambda b, h, qi, ki: (b * nk + ki, h)),
            pl.BlockSpec((HEAD_DIM, tk), lambda b, h, qi, ki: (h, b * nk + ki)),
        ],
        out_specs=pl.BlockSpec((tq, gw), lambda b, h, qi, ki: (b * nq + qi, h)),
        out_shape=jax.ShapeDtypeStruct((t, ATTN_W), BF16),
        scratch_shapes=[
            pltpu.VMEM((Q_GROUP, 1, tq), F32),
            pltpu.VMEM((Q_GROUP, HEAD_DIM + FLASH_ONES_ROWS, tq), F32),
        ],
        compiler_params=_params("parallel", "parallel", "parallel", "arbitrary"),
        name="flash_attn",
    )(q, k, vt)


def _conv_kernel(a_ref, g_ref, ap_ref, gp_ref, an_ref, gn_ref, dw_ref, dwb_ref, lng_ref, lnb_ref, pw_ref,
                 o_ref, h_ref, *, tm, per_seq):
    i = pl.program_id(0)
    pos = i % per_seq

    def glu(a, g):
        return a.astype(F32) * _sigmoid(g.astype(F32))

    prev = jnp.where(pos == 0, 0.0, glu(ap_ref[...], gp_ref[...]))
    nxt = jnp.where(pos == per_seq - 1, 0.0, glu(an_ref[...], gn_ref[...]))
    h_ref[0:CONV_HALO, :] = prev
    h_ref[CONV_HALO:CONV_HALO + tm, :] = glu(a_ref[...], g_ref[...])
    h_ref[CONV_HALO + tm:, :] = nxt

    base = CONV_HALO - CONV_K // 2
    acc = jnp.zeros((tm, CONV_W), F32) + dwb_ref[...]
    for j in range(CONV_K):
        acc = acc + dw_ref[j:j + 1, :] * h_ref[base + j:base + j + tm, :]
    mu = jnp.mean(acc, axis=-1, keepdims=True)
    cen = acc - mu
    var = jnp.mean(cen * cen, axis=-1, keepdims=True)
    y = cen * lax.rsqrt(var + NORM_EPS) * lng_ref[...] + lnb_ref[...]
    o_ref[...] = _dot(_silu(y).astype(BF16), pw_ref[...]).astype(o_ref.dtype)


def _conv(u, dw, dwb, lng, lnb, pw, n):
    t = u.shape[0]
    tm = _tile(n, 512)
    per_seq = n // tm
    hb = tm // CONV_HALO
    last = t // CONV_HALO - 1
    ca, cg = OFF_CA // CONV_W, OFF_CG // CONV_W
    vec = lambda v: v.reshape(1, CONV_W)
    row = pl.BlockSpec((1, CONV_W), lambda i: (0, 0))
    return pl.pallas_call(
        functools.partial(_conv_kernel, tm=tm, per_seq=per_seq),
        grid=(t // tm,),
        in_specs=[
            pl.BlockSpec((tm, CONV_W), lambda i: (i, ca)),
            pl.BlockSpec((tm, CONV_W), lambda i: (i, cg)),
            pl.BlockSpec((CONV_HALO, CONV_W), lambda i: (jnp.maximum(i * hb - 1, 0), ca)),
            pl.BlockSpec((CONV_HALO, CONV_W), lambda i: (jnp.maximum(i * hb - 1, 0), cg)),
            pl.BlockSpec((CONV_HALO, CONV_W), lambda i: (jnp.minimum((i + 1) * hb, last), ca)),
            pl.BlockSpec((CONV_HALO, CONV_W), lambda i: (jnp.minimum((i + 1) * hb, last), cg)),
            pl.BlockSpec((CONV_K, CONV_W), lambda i: (0, 0)),
            row, row, row,
            pl.BlockSpec((CONV_W, CONV_W), lambda i: (0, 0)),
        ],
        out_specs=pl.BlockSpec((tm, CONV_W), lambda i: (i, 0)),
        out_shape=jax.ShapeDtypeStruct((t, CONV_W), BF16),
        scratch_shapes=[pltpu.VMEM((tm + 2 * CONV_HALO, CONV_W), F32)],
        compiler_params=_params("parallel"),
        name="conv_mixer",
    )(u, u, u, u, u, u, dw, vec(dwb), vec(lng), vec(lnb), pw)


def _hgrn_tables(reverse):
    c = HGRN_CHUNK
    levels = []
    h = c // 2
    while h >= 1:
        levels.append(h)
        h //= 2
    r = np.arange(c)
    col = r[None, :]
    seg = np.zeros((len(levels) + 2, c, c), np.float32)
    mask = np.zeros((len(levels) + 1, c, c), np.float32)
    for li, h in enumerate(levels):
        blk = r // (2 * h)
        upper = (r % (2 * h)) >= h
        b = (blk * 2 * h + h - 1)[:, None]
        rr = r[:, None]
        seg[li] = np.where(upper[:, None], (col > b) & (col <= rr), (col > rr) & (col <= b))
        mask[li] = upper[:, None] & (~upper)[None, :] & (blk[:, None] == blk[None, :])
    seg[-2] = col <= r[:, None]
    seg[-1] = col > r[:, None]
    mask[-1] = np.eye(c)
    if reverse:
        seg = seg[:, ::-1, ::-1]
        mask = mask[:, ::-1, ::-1]
    return (jnp.asarray(seg.reshape(-1, c), dtype=BF16), jnp.asarray(mask, dtype=F32), len(levels))


def _hgrn_kernel(*refs, reverse, final, n_levels):
    if final:
        hq_ref, hf_ref, hi_ref, lb_ref, seg_ref, mask_ref, oprev_ref, hg_ref, gn_ref, o_ref, st_ref = refs
    else:
        hq_ref, hf_ref, hi_ref, lb_ref, seg_ref, mask_ref, o_ref, st_ref = refs
    c = HGRN_CHUNK

    @pl.when(pl.program_id(1) == 0)
    def _():
        st_ref[...] = jnp.zeros_like(st_ref)

    lb = lb_ref[...]
    f = lb + (1.0 - lb) * _sigmoid(hf_ref[...].astype(F32))
    logf = jnp.log(f)
    kk = 1.0 - f
    qq = _silu(hq_ref[...].astype(F32))
    vv = hi_ref[...]

    g_hi = logf.astype(BF16)
    g_lo = (logf - g_hi.astype(F32)).astype(BF16)
    seg = seg_ref[...]
    e_all = jnp.exp(_dot(seg, g_hi) + _dot(seg, g_lo))

    total_row = 0 if reverse else c - 1
    outs = []
    for hd in range(HGRN_HEADS):
        sl = slice(hd * HGRN_D, (hd + 1) * HGRN_D)
        qh, kh, vh = qq[:, sl], kk[:, sl], vv[:, sl]
        a = mask_ref[n_levels] * _dot_nt(qh.astype(BF16), kh.astype(BF16))
        for l in range(n_levels):
            el = e_all[l * c:(l + 1) * c, sl]
            a = a + mask_ref[l] * _dot_nt((qh * el).astype(BF16), (kh * el).astype(BF16))
        e_pre = e_all[n_levels * c:(n_levels + 1) * c, sl]
        e_suf = e_all[(n_levels + 1) * c:(n_levels + 2) * c, sl]
        st = st_ref[hd]
        o = _dot(a.astype(BF16), vh) + _dot_nt((qh * e_pre).astype(BF16), st.astype(BF16))
        v_t = vh.astype(F32).T.astype(BF16)
        st_ref[hd] = e_pre[total_row:total_row + 1, :] * st + _dot(v_t, (kh * e_suf).astype(BF16))
        outs.append(o)

    if not final:
        for hd in range(HGRN_HEADS):
            o_ref[:, hd * HGRN_D:(hd + 1) * HGRN_D] = outs[hd]
    else:
        gate = _silu(hg_ref[...].astype(F32)) * gn_ref[...]
        for hd in range(HGRN_HEADS):
            sl = slice(hd * HGRN_D, (hd + 1) * HGRN_D)
            o = outs[hd] + oprev_ref[:, sl]
            o = o * lax.rsqrt(jnp.mean(o * o, axis=-1, keepdims=True) + NORM_EPS)
            o_ref[:, sl] = (o * gate[:, sl]).astype(o_ref.dtype)


def _hgrn_pass(u, lb, batch, n, *, reverse, o_prev=None, gn=None):
    t = u.shape[0]
    c = HGRN_CHUNK
    nc = n // c
    final = o_prev is not None
    seg, mask, n_levels = _hgrn_tables(reverse)

    def rows(b, ci):
        return b * nc + (nc - 1 - ci if reverse else ci)

    def col_spec(off):
        return pl.BlockSpec((c, HGRN_W), lambda b, ci: (rows(b, ci), off // HGRN_W))

    const2 = lambda b, ci: (0, 0)
    in_specs = [
        col_spec(OFF_HQ),
        col_spec(OFF_FB if reverse else OFF_FF),
        col_spec(OFF_HI),
        pl.BlockSpec((1, HGRN_W), const2),
        pl.BlockSpec(seg.shape, const2),
        pl.BlockSpec(mask.shape, lambda b, ci: (0, 0, 0)),
    ]
    args = [u, u, u, lb.reshape(1, HGRN_W), seg, mask]
    if final:
        in_specs += [
            pl.BlockSpec((c, HGRN_W), lambda b, ci: (rows(b, ci), 0)),
            col_spec(OFF_HG),
            pl.BlockSpec((1, HGRN_W), const2),
        ]
        args += [o_prev, u, gn.reshape(1, HGRN_W)]
    return pl.pallas_call(
        functools.partial(_hgrn_kernel, reverse=reverse, final=final, n_levels=n_levels),
        grid=(batch, nc),
        in_specs=in_specs,
        out_specs=pl.BlockSpec((c, HGRN_W), lambda b, ci: (rows(b, ci), 0)),
        out_shape=jax.ShapeDtypeStruct((t, HGRN_W), BF16 if final else F32),
        scratch_shapes=[pltpu.VMEM((HGRN_HEADS, HGRN_D, HGRN_D), F32)],
        compiler_params=_params("parallel", "arbitrary"),
        name="hgrn_bwd" if reverse else "hgrn_fwd",
    )(*args)


def _out_proj_kernel(a_ref, c_ref, h_ref, x_ref, w_ref, g_ref, o_ref):
    mix = _dot(a_ref[...], w_ref[0:ATTN_W, :])
    mix = mix + _dot(c_ref[...], w_ref[ATTN_W:ATTN_W + CONV_W, :])
    mix = mix + _dot(h_ref[...], w_ref[ATTN_W + CONV_W:, :])
    o_ref[...] = x_ref[...] + _rms(mix, g_ref[...])


def _out_proj(a, cv, hg, x, w, g):
    t, d = x.shape
    tm = _tile(t, 512)
    mixw = w.shape[0]
    return pl.pallas_call(
        _out_proj_kernel,
        grid=(t // tm,),
        in_specs=[
            pl.BlockSpec((tm, ATTN_W), lambda i: (i, 0)),
            pl.BlockSpec((tm, CONV_W), lambda i: (i, 0)),
            pl.BlockSpec((tm, HGRN_W), lambda i: (i, 0)),
            pl.BlockSpec((tm, d), lambda i: (i, 0)),
            pl.BlockSpec((mixw, d), lambda i: (0, 0)),
            pl.BlockSpec((1, d), lambda i: (0, 0)),
        ],
        out_specs=pl.BlockSpec((tm, d), lambda i: (i, 0)),
        out_shape=jax.ShapeDtypeStruct((t, d), F32),
        compiler_params=_params("parallel"),
        name="out_proj",
    )(a, cv, hg, x, w, g.reshape(1, d))


def _ffn_kernel(x_ref, gpre_ref, wg_ref, wu_ref, wo_ref, gpost_ref, o_ref, xn_ref, acc_ref):
    j = pl.program_id(1)

    @pl.when(j == 0)
    def _():
        xn_ref[...] = _rms(x_ref[...], gpre_ref[...]).astype(BF16)
        acc_ref[...] = jnp.zeros_like(acc_ref)

    xn = xn_ref[...]
    hidden = _silu(_dot(xn, wg_ref[...])) * _dot(xn, wu_ref[...])
    acc_ref[...] += _dot(hidden.astype(BF16), wo_ref[...])

    @pl.when(j == pl.num_programs(1) - 1)
    def _():
        o_ref[...] = x_ref[...] + _rms(acc_ref[...], gpost_ref[...])


def _ffn(x, gpre, w_in, w_out, gpost):
    t, d = x.shape
    f = w_out.shape[0]
    tm, th = _tile(t, 512), _tile(f, 512)
    nh = f // th
    return pl.pallas_call(
        _ffn_kernel,
        grid=(t // tm, nh),
        in_specs=[
            pl.BlockSpec((tm, d), lambda i, j: (i, 0)),
            pl.BlockSpec((1, d), lambda i, j: (0, 0)),
            pl.BlockSpec((d, th), lambda i, j: (0, j)),
            pl.BlockSpec((d, th), lambda i, j: (0, nh + j)),
            pl.BlockSpec((th, d), lambda i, j: (j, 0)),
            pl.BlockSpec((1, d), lambda i, j: (0, 0)),
        ],
        out_specs=pl.BlockSpec((tm, d), lambda i, j: (i, 0)),
        out_shape=jax.ShapeDtypeStruct((t, d), F32),
        scratch_shapes=[pltpu.VMEM((tm, d), BF16), pltpu.VMEM((tm, d), F32)],
        compiler_params=_params("parallel", "arbitrary"),
        name="ffn",
    )(x, gpre.reshape(1, d), w_in, w_in, w_out, gpost.reshape(1, d))


def _ple_kernel(x_ref, p_ref, wg_ref, wp_ref, g_ref, o_ref):
    x = x_ref[...]
    gate = _sigmoid(_dot(x.astype(BF16), wg_ref[...]))
    proj = _dot(p_ref[...].astype(BF16), wp_ref[...])
    o_ref[...] = x + _rms(proj * gate, g_ref[...])


def _ple(x, p, layer, w_gate, w_proj, g):
    t, d = x.shape
    pd = p.shape[-1]
    tm = _tile(t, 512)
    return pl.pallas_call(
        _ple_kernel,
        grid=(t // tm,),
        in_specs=[
            pl.BlockSpec((tm, d), lambda i: (i, 0)),
            pl.BlockSpec((None, tm, pd), lambda i: (layer, i, 0)),
            pl.BlockSpec((d, d), lambda i: (0, 0)),
            pl.BlockSpec((pd, d), lambda i: (0, 0)),
            pl.BlockSpec((1, d), lambda i: (0, 0)),
        ],
        out_specs=pl.BlockSpec((tm, d), lambda i: (i, 0)),
        out_shape=jax.ShapeDtypeStruct((t, d), F32),
        compiler_params=_params("parallel"),
        name="ple",
    )(x, p, w_gate, w_proj, g.reshape(1, d))


def _trunk(x, p, lb, wts):
    depth = p.shape[0]
    batch, n, d = x.shape
    t = batch * n
    x = x.reshape(t, d)
    p = p.reshape(depth, t, p.shape[-1])
    cos_l, sin_l = _rope_tables(n)
    for l in range(depth):
        w = {k: v[l] for k, v in wts.items()}
        u = _in_proj(x, w["norm_mix_pre"], w["w_in"])
        q_r, k_r, v_t = _attn_prep(u, cos_l, sin_l, w["q_norm"], w["k_norm"], n)
        a_out = _flash(q_r, k_r, v_t, batch, n)
        c_out = _conv(u, w["conv_dw"], w["conv_dw_b"], w["conv_ln_g"], w["conv_ln_b"], w["conv_pw"], n)
        o_fwd = _hgrn_pass(u, lb[l, 0], batch, n, reverse=False)
        h_out = _hgrn_pass(u, lb[l, 1], batch, n, reverse=True, o_prev=o_fwd, gn=w["hgrn_gn"])
        x = _out_proj(a_out, c_out, h_out, x, w["w_out"], w["norm_mix_post"])
        x = _ffn(x, w["norm_ffn_pre"], w["w_ffn_in"], w["w_ffn_out"], w["norm_ffn_post"])
        x = _ple(x, p, l, w["w_ple_gate"], w["w_ple_proj"], w["ple_norm"])
    return x.reshape(batch, n, d)


def kernel(x_prompt, x_sample, p_prompt, p_sample, norm_mix_pre, norm_mix_post, w_in, q_norm, k_norm, conv_dw, conv_dw_b, conv_ln_g, conv_ln_b, conv_pw, hgrn_lb, hgrn_gn, w_out, norm_ffn_pre, norm_ffn_post, w_ffn_in, w_ffn_out, w_ple_gate, w_ple_proj, ple_norm):
    sm = jax.nn.softmax(hgrn_lb.astype(F32), axis=0)
    lb = jnp.cumsum(sm, axis=0) - sm[0]
    wts = dict(
        norm_mix_pre=norm_mix_pre, norm_mix_post=norm_mix_post, w_in=w_in.astype(BF16),
        q_norm=q_norm, k_norm=k_norm, conv_dw=conv_dw, conv_dw_b=conv_dw_b,
        conv_ln_g=conv_ln_g, conv_ln_b=conv_ln_b, conv_pw=conv_pw.astype(BF16),
        hgrn_gn=hgrn_gn, w_out=w_out.astype(BF16), norm_ffn_pre=norm_ffn_pre,
        norm_ffn_post=norm_ffn_post, w_ffn_in=w_ffn_in.astype(BF16), w_ffn_out=w_ffn_out.astype(BF16),
        w_ple_gate=w_ple_gate.astype(BF16), w_ple_proj=w_ple_proj.astype(BF16), ple_norm=ple_norm,
    )
    y_prompt = _trunk(x_prompt, p_prompt, lb, wts)
    y_sample = _trunk(x_sample, p_sample, lb, wts)
    return (y_prompt, y_sample)
```

```python
import functools

import numpy as np
import jax
import jax.numpy as jnp
from jax import lax
from jax.experimental import pallas as pl
from jax.experimental.pallas import tpu as pltpu

F32 = jnp.float32
BF16 = jnp.bfloat16

NORM_EPS = 1e-6
ROPE_THETA = 10000.0
LOG2_E = 1.4426950408889634
GRID_W = 64
HEAD_DIM = 128
N_Q_HEADS = 8
N_KV_HEADS = 2
Q_GROUP = N_Q_HEADS // N_KV_HEADS
ATTN_W = N_Q_HEADS * HEAD_DIM
KV_W = N_KV_HEADS * HEAD_DIM
CONV_W = 512
CONV_K = 31
CONV_HALO = 16
HGRN_W = 512
HGRN_D = 128
HGRN_HEADS = HGRN_W // HGRN_D
HGRN_CHUNK = 128
HGRN_CHUNKS_PER_STEP = 4
FLASH_ONES_ROWS = 16
FLASH_LOOKAHEAD = 3
FLASH_TQ = 256
FLASH_TK = 4096
FLASH_SUB = 512

OFF_Q = 0
OFF_K = OFF_Q + ATTN_W
OFF_V = OFF_K + KV_W
OFF_CA = OFF_V + KV_W
OFF_CG = OFF_CA + CONV_W
OFF_HQ = OFF_CG + CONV_W
OFF_FF = OFF_HQ + HGRN_W
OFF_FB = OFF_FF + HGRN_W
OFF_HI = OFF_FB + HGRN_W
OFF_HG = OFF_HI + HGRN_W
IN_COLS = OFF_HG + HGRN_W

V7X_VMEM_LIMIT_BYTES = 56 * 1024 * 1024


def _params(*semantics):
    return pltpu.CompilerParams(dimension_semantics=semantics, vmem_limit_bytes=V7X_VMEM_LIMIT_BYTES)


def _tile(dim, pref):
    t = min(dim, pref)
    assert dim % t == 0, (dim, pref)
    return t


def _rms(x, g):
    return x * lax.rsqrt(jnp.mean(x * x, axis=-1, keepdims=True) + NORM_EPS) * g


def _sigmoid(x):
    return 1.0 / (1.0 + jnp.exp(-x))


def _silu(x):
    return x * _sigmoid(x)


def _dot(a, b):
    return jnp.dot(a, b, preferred_element_type=F32)


def _dot_nt(a, b):
    return lax.dot_general(a, b, (((1,), (1,)), ((), ())), preferred_element_type=F32)


def _in_proj_kernel(x_ref, g_ref, w_ref, o_ref, xn_ref):
    @pl.when(pl.program_id(1) == 0)
    def _():
        xn_ref[...] = _rms(x_ref[...], g_ref[...]).astype(BF16)

    o_ref[...] = _dot(xn_ref[...], w_ref[...]).astype(o_ref.dtype)


def _in_proj(x, g, w):
    t, d = x.shape
    n = w.shape[1]
    tm, tn = _tile(t, 1024), _tile(n, 1024)
    return pl.pallas_call(
        _in_proj_kernel,
        grid=(t // tm, n // tn),
        in_specs=[
            pl.BlockSpec((tm, d), lambda i, j: (i, 0)),
            pl.BlockSpec((1, d), lambda i, j: (0, 0)),
            pl.BlockSpec((d, tn), lambda i, j: (0, j)),
        ],
        out_specs=pl.BlockSpec((tm, tn), lambda i, j: (i, j)),
        out_shape=jax.ShapeDtypeStruct((t, n), BF16),
        scratch_shapes=[pltpu.VMEM((tm, d), BF16)],
        compiler_params=_params("parallel", "arbitrary"),
        name="in_proj",
    )(x, g.reshape(1, d), w)


def _rope_tables(n):
    rows = n // GRID_W
    row = jnp.repeat(jnp.arange(rows, dtype=F32), GRID_W)
    col = jnp.tile(jnp.arange(GRID_W, dtype=F32), rows)
    axis_dims = HEAD_DIM // 2
    inv = ROPE_THETA ** (-jnp.arange(0, axis_dims, 2, dtype=F32) / axis_dims)
    ang = jnp.concatenate([row[:, None] * inv, col[:, None] * inv], axis=-1)
    cos, sin = jnp.cos(ang), jnp.sin(ang)
    cos_l = jnp.repeat(cos, 2, axis=-1)
    sin_l = jnp.stack([-sin, sin], axis=-1).reshape(n, HEAD_DIM)
    return cos_l, sin_l


def _attn_prep_kernel(q_ref, k_ref, v_ref, cos_ref, sin_ref, qg_ref, kg_ref, qo_ref, ko_ref, vt_ref):
    cos = cos_ref[...]
    sin = sin_ref[...]
    lane = lax.broadcasted_iota(jnp.int32, cos.shape, 1)
    even = (lane & 1) == 0

    def norm_rope(x, g):
        y = _rms(x.astype(F32), g)
        partner = jnp.where(even, pltpu.roll(y, HEAD_DIM - 1, 1), pltpu.roll(y, 1, 1))
        return y * cos + partner * sin

    scale = HEAD_DIM ** -0.5 * LOG2_E
    for h in range(N_Q_HEADS):
        sl = slice(h * HEAD_DIM, (h + 1) * HEAD_DIM)
        qo_ref[:, sl] = (norm_rope(q_ref[:, sl], qg_ref[...]) * scale).astype(qo_ref.dtype)
    for h in range(N_KV_HEADS):
        sl = slice(h * HEAD_DIM, (h + 1) * HEAD_DIM)
        ko_ref[:, sl] = norm_rope(k_ref[:, sl], kg_ref[...]).astype(ko_ref.dtype)
    vt_ref[...] = v_ref[...].astype(F32).T.astype(vt_ref.dtype)


def _attn_prep(u, cos_l, sin_l, q_gain, k_gain, n):
    t = u.shape[0]
    tm = _tile(n, 512)
    per_seq = n // tm
    return pl.pallas_call(
        _attn_prep_kernel,
        grid=(t // tm,),
        in_specs=[
            pl.BlockSpec((tm, ATTN_W), lambda i: (i, OFF_Q // ATTN_W)),
            pl.BlockSpec((tm, KV_W), lambda i: (i, OFF_K // KV_W)),
            pl.BlockSpec((tm, KV_W), lambda i: (i, OFF_V // KV_W)),
            pl.BlockSpec((tm, HEAD_DIM), lambda i: (i % per_seq, 0)),
            pl.BlockSpec((tm, HEAD_DIM), lambda i: (i % per_seq, 0)),
            pl.BlockSpec((1, HEAD_DIM), lambda i: (0, 0)),
            pl.BlockSpec((1, HEAD_DIM), lambda i: (0, 0)),
        ],
        out_specs=[
            pl.BlockSpec((tm, ATTN_W), lambda i: (i, 0)),
            pl.BlockSpec((tm, KV_W), lambda i: (i, 0)),
            pl.BlockSpec((KV_W, tm), lambda i: (0, i)),
        ],
        out_shape=[
            jax.ShapeDtypeStruct((t, ATTN_W), BF16),
            jax.ShapeDtypeStruct((t, KV_W), BF16),
            jax.ShapeDtypeStruct((KV_W, t), BF16),
        ],
        compiler_params=_params("parallel"),
        name="attn_prep",
    )(u, u, u, cos_l, sin_l, q_gain.reshape(1, HEAD_DIM), k_gain.reshape(1, HEAD_DIM))


def _flash_kernel(q_ref, k_ref, vt_ref, o_ref, m_ref, acc_ref, *, tk, sub):
    ki = pl.program_id(3)

    @pl.when(ki == 0)
    def _():
        m_ref[...] = jnp.full_like(m_ref, -jnp.inf)
        acc_ref[...] = jnp.zeros_like(acc_ref)

    stages = [(s0, g) for s0 in range(0, tk, sub) for g in range(Q_GROUP)]

    def scores(stage):
        s0, g = stage
        return _dot_nt(k_ref[s0:s0 + sub, :], q_ref[:, g * HEAD_DIM:(g + 1) * HEAD_DIM])

    ones = jnp.ones((FLASH_ONES_ROWS, sub), BF16)

    pending = [scores(st) for st in stages[:FLASH_LOOKAHEAD]]
    for i, (s0, g) in enumerate(stages):
        s = pending.pop(0)
        if i + FLASH_LOOKAHEAD < len(stages):
            pending.append(scores(stages[i + FLASH_LOOKAHEAD]))
        m_prev = m_ref[g]
        m_new = jnp.maximum(m_prev, jnp.max(s, axis=0, keepdims=True))
        alpha = jnp.exp2(m_prev - m_new)
        p = jnp.exp2((s - m_new).astype(BF16))
        vt1 = jnp.concatenate([vt_ref[:, s0:s0 + sub], ones], axis=0)
        acc_ref[g] = alpha * acc_ref[g] + _dot(vt1, p)
        m_ref[g] = m_new

    @pl.when(ki == pl.num_programs(3) - 1)
    def _():
        for g in range(Q_GROUP):
            out = acc_ref[g, 0:HEAD_DIM, :] / acc_ref[g, HEAD_DIM:HEAD_DIM + 1, :]
            o_ref[:, g * HEAD_DIM:(g + 1) * HEAD_DIM] = out.T.astype(o_ref.dtype)


def _flash(q, k, vt, batch, n):
    t = q.shape[0]
    tq, tk = _tile(n, FLASH_TQ), _tile(n, FLASH_TK)
    sub = _tile(tk, FLASH_SUB)
    nq, nk = n // tq, n // tk
    gw = Q_GROUP * HEAD_DIM
    return pl.pallas_call(
        functools.partial(_flash_kernel, tk=tk, sub=sub),
        grid=(batch, N_KV_HEADS, nq, nk),
        in_specs=[
            pl.BlockSpec((tq, gw), lambda b, h, qi, ki: (b * nq + qi, h)),
            pl.BlockSpec((tk, HEAD_DIM), lambda b, h, qi, ki: (b * nk + ki, h)),
            pl.BlockSpec((HEAD_DIM, tk), lambda b, h, qi, ki: (h, b * nk + ki)),
        ],
        out_specs=pl.BlockSpec((tq, gw), lambda b, h, qi, ki: (b * nq + qi, h)),
        out_shape=jax.ShapeDtypeStruct((t, ATTN_W), BF16),
        scratch_shapes=[
            pltpu.VMEM((Q_GROUP, 1, tq), F32),
            pltpu.VMEM((Q_GROUP, HEAD_DIM + FLASH_ONES_ROWS, tq), F32),
        ],
        compiler_params=_params("parallel", "parallel", "parallel", "arbitrary"),
        name="flash_attn",
    )(q, k, vt)


def _conv_kernel(a_ref, g_ref, ap_ref, gp_ref, an_ref, gn_ref, dw_ref, dwb_ref, lng_ref, lnb_ref, pw_ref,
                 o_ref, h_ref, *, tm, per_seq):
    i = pl.program_id(0)
    pos = i % per_seq

    def glu(a, g):
        return a.astype(F32) * _sigmoid(g.astype(F32))

    prev = jnp.where(pos == 0, 0.0, glu(ap_ref[...], gp_ref[...]))
    nxt = jnp.where(pos == per_seq - 1, 0.0, glu(an_ref[...], gn_ref[...]))
    h_ref[0:CONV_HALO, :] = prev
    h_ref[CONV_HALO:CONV_HALO + tm, :] = glu(a_ref[...], g_ref[...])
    h_ref[CONV_HALO + tm:, :] = nxt

    base = CONV_HALO - CONV_K // 2
    acc = jnp.zeros((tm, CONV_W), F32) + dwb_ref[...]
    for j in range(CONV_K):
        acc = acc + dw_ref[j:j + 1, :] * h_ref[base + j:base + j + tm, :]
    mu = jnp.mean(acc, axis=-1, keepdims=True)
    cen = acc - mu
    var = jnp.mean(cen * cen, axis=-1, keepdims=True)
    y = cen * lax.rsqrt(var + NORM_EPS) * lng_ref[...] + lnb_ref[...]
    o_ref[...] = _dot(_silu(y).astype(BF16), pw_ref[...]).astype(o_ref.dtype)


def _conv(u, dw, dwb, lng, lnb, pw, n):
    t = u.shape[0]
    tm = _tile(n, 512)
    per_seq = n // tm
    hb = tm // CONV_HALO
    last = t // CONV_HALO - 1
    ca, cg = OFF_CA // CONV_W, OFF_CG // CONV_W
    vec = lambda v: v.reshape(1, CONV_W)
    row = pl.BlockSpec((1, CONV_W), lambda i: (0, 0))
    return pl.pallas_call(
        functools.partial(_conv_kernel, tm=tm, per_seq=per_seq),
        grid=(t // tm,),
        in_specs=[
            pl.BlockSpec((tm, CONV_W), lambda i: (i, ca)),
            pl.BlockSpec((tm, CONV_W), lambda i: (i, cg)),
            pl.BlockSpec((CONV_HALO, CONV_W), lambda i: (jnp.maximum(i * hb - 1, 0), ca)),
            pl.BlockSpec((CONV_HALO, CONV_W), lambda i: (jnp.maximum(i * hb - 1, 0), cg)),
            pl.BlockSpec((CONV_HALO, CONV_W), lambda i: (jnp.minimum((i + 1) * hb, last), ca)),
            pl.BlockSpec((CONV_HALO, CONV_W), lambda i: (jnp.minimum((i + 1) * hb, last), cg)),
            pl.BlockSpec((CONV_K, CONV_W), lambda i: (0, 0)),
            row, row, row,
            pl.BlockSpec((CONV_W, CONV_W), lambda i: (0, 0)),
        ],
        out_specs=pl.BlockSpec((tm, CONV_W), lambda i: (i, 0)),
        out_shape=jax.ShapeDtypeStruct((t, CONV_W), BF16),
        scratch_shapes=[pltpu.VMEM((tm + 2 * CONV_HALO, CONV_W), F32)],
        compiler_params=_params("parallel"),
        name="conv_mixer",
    )(u, u, u, u, u, u, dw, vec(dwb), vec(lng), vec(lnb), pw)


def _hgrn_tables(reverse):
    c = HGRN_CHUNK
    levels = []
    h = c // 2
    while h >= 1:
        levels.append(h)
        h //= 2
    r = np.arange(c)
    col = r[None, :]
    seg = np.zeros((len(levels) + 2, c, c), np.float32)
    mask = np.zeros((len(levels) + 1, c, c), np.float32)
    for li, h in enumerate(levels):
        blk = r // (2 * h)
        upper = (r % (2 * h)) >= h
        b = (blk * 2 * h + h - 1)[:, None]
        rr = r[:, None]
        seg[li] = np.where(upper[:, None], (col > b) & (col <= rr), (col > rr) & (col <= b))
        mask[li] = upper[:, None] & (~upper)[None, :] & (blk[:, None] == blk[None, :])
    seg[-2] = col <= r[:, None]
    seg[-1] = col > r[:, None]
    mask[-1] = np.eye(c)
    if reverse:
        seg = seg[:, ::-1, ::-1]
        mask = mask[:, ::-1, ::-1]
    seg = seg.reshape(-1, c)
    seg2 = np.concatenate([seg, seg], axis=1)
    return (jnp.asarray(seg2, dtype=BF16), jnp.asarray(mask, dtype=F32), len(levels))


def _hgrn_kernel(*refs, reverse, final, n_levels, n_chunks):
    if final:
        hq_ref, hf_ref, hi_ref, lb_ref, seg_ref, mask_ref, oprev_ref, hg_ref, gn_ref, o_ref, st_ref = refs
    else:
        hq_ref, hf_ref, hi_ref, lb_ref, seg_ref, mask_ref, o_ref, st_ref = refs
    c = HGRN_CHUNK

    @pl.when(pl.program_id(1) == 0)
    def _():
        st_ref[...] = jnp.zeros_like(st_ref)

    lb = lb_ref[...]
    seg = seg_ref[...]
    total_row = 0 if reverse else c - 1

    def prep(j):
        rows = slice(j * c, (j + 1) * c)
        f = lb + (1.0 - lb) * _sigmoid(hf_ref[rows, :].astype(F32))
        log2f = jnp.log(f) * LOG2_E
        g_hi = log2f.astype(BF16)
        g_lo = (log2f - g_hi.astype(F32)).astype(BF16)
        e_all = jnp.exp2(_dot(seg, jnp.concatenate([g_hi, g_lo], axis=0)))
        decay = e_all[n_levels * c + total_row:n_levels * c + total_row + 1, :]
        return _silu(hq_ref[rows, :].astype(F32)).astype(BF16), (1.0 - f).astype(BF16), e_all.astype(BF16), decay

    order = list(range(n_chunks))[::-1] if reverse else list(range(n_chunks))
    pending = prep(order[0])
    for idx, j in enumerate(order):
        qq, kk, e_all, decay = pending
        if idx + 1 < n_chunks:
            pending = prep(order[idx + 1])
        rows = slice(j * c, (j + 1) * c)
        if final:
            gate = _silu(hg_ref[rows, :].astype(F32)) * gn_ref[...]
        for hd in range(HGRN_HEADS):
            sl = slice(hd * HGRN_D, (hd + 1) * HGRN_D)
            qh, kh, vh = qq[:, sl], kk[:, sl], hi_ref[rows, sl]
            a = mask_ref[n_levels] * _dot_nt(qh, kh)
            for l in range(n_levels):
                el = e_all[l * c:(l + 1) * c, sl]
                a = a + mask_ref[l] * _dot_nt(qh * el, kh * el)
            e_pre = e_all[n_levels * c:(n_levels + 1) * c, sl]
            e_suf = e_all[(n_levels + 1) * c:(n_levels + 2) * c, sl]
            st = st_ref[hd]
            o = _dot(a.astype(BF16), vh) + _dot_nt(qh * e_pre, st.astype(BF16))
            v_t = vh.astype(F32).T.astype(BF16)
            st_ref[hd] = decay[:, sl] * st + _dot(v_t, kh * e_suf)
            if final:
                o = o + oprev_ref[rows, sl]
                o = o * lax.rsqrt(jnp.mean(o * o, axis=-1, keepdims=True) + NORM_EPS)
                o_ref[rows, sl] = (o * gate[:, sl]).astype(o_ref.dtype)
            else:
                o_ref[rows, sl] = o


def _hgrn_pass(u, lb, batch, n, *, reverse, o_prev=None, gn=None):
    t = u.shape[0]
    n_chunks = _tile(n // HGRN_CHUNK, HGRN_CHUNKS_PER_STEP)
    c = n_chunks * HGRN_CHUNK
    nc = n // c
    final = o_prev is not None
    seg, mask, n_levels = _hgrn_tables(reverse)

    def rows(b, ci):
        return b * nc + (nc - 1 - ci if reverse else ci)

    def col_spec(off):
        return pl.BlockSpec((c, HGRN_W), lambda b, ci: (rows(b, ci), off // HGRN_W))

    const2 = lambda b, ci: (0, 0)
    in_specs = [
        col_spec(OFF_HQ),
        col_spec(OFF_FB if reverse else OFF_FF),
        col_spec(OFF_HI),
        pl.BlockSpec((1, HGRN_W), const2),
        pl.BlockSpec(seg.shape, const2),
        pl.BlockSpec(mask.shape, lambda b, ci: (0, 0, 0)),
    ]
    args = [u, u, u, lb.reshape(1, HGRN_W), seg, mask]
    if final:
        in_specs += [
            pl.BlockSpec((c, HGRN_W), lambda b, ci: (rows(b, ci), 0)),
            col_spec(OFF_HG),
            pl.BlockSpec((1, HGRN_W), const2),
        ]
        args += [o_prev, u, gn.reshape(1, HGRN_W)]
    return pl.pallas_call(
        functools.partial(_hgrn_kernel, reverse=reverse, final=final, n_levels=n_levels, n_chunks=n_chunks),
        grid=(batch, nc),
        in_specs=in_specs,
        out_specs=pl.BlockSpec((c, HGRN_W), lambda b, ci: (rows(b, ci), 0)),
        out_shape=jax.ShapeDtypeStruct((t, HGRN_W), BF16 if final else F32),
        scratch_shapes=[pltpu.VMEM((HGRN_HEADS, HGRN_D, HGRN_D), F32)],
        compiler_params=_params("parallel", "arbitrary"),
        name="hgrn_bwd" if reverse else "hgrn_fwd",
    )(*args)


def _out_proj_kernel(a_ref, c_ref, h_ref, x_ref, w_ref, g_ref, o_ref):
    mix = _dot(a_ref[...], w_ref[0:ATTN_W, :])
    mix = mix + _dot(c_ref[...], w_ref[ATTN_W:ATTN_W + CONV_W, :])
    mix = mix + _dot(h_ref[...], w_ref[ATTN_W + CONV_W:, :])
    o_ref[...] = x_ref[...] + _rms(mix, g_ref[...])


def _out_proj(a, cv, hg, x, w, g):
    t, d = x.shape
    tm = _tile(t, 512)
    mixw = w.shape[0]
    return pl.pallas_call(
        _out_proj_kernel,
        grid=(t // tm,),
        in_specs=[
            pl.BlockSpec((tm, ATTN_W), lambda i: (i, 0)),
            pl.BlockSpec((tm, CONV_W), lambda i: (i, 0)),
            pl.BlockSpec((tm, HGRN_W), lambda i: (i, 0)),
            pl.BlockSpec((tm, d), lambda i: (i, 0)),
            pl.BlockSpec((mixw, d), lambda i: (0, 0)),
            pl.BlockSpec((1, d), lambda i: (0, 0)),
        ],
        out_specs=pl.BlockSpec((tm, d), lambda i: (i, 0)),
        out_shape=jax.ShapeDtypeStruct((t, d), F32),
        compiler_params=_params("parallel"),
        name="out_proj",
    )(a, cv, hg, x, w, g.reshape(1, d))


def _ffn_kernel(x_ref, gpre_ref, wg_ref, wu_ref, wo_ref, gpost_ref, o_ref, xn_ref, acc_ref, h_ref):
    j = pl.program_id(1)
    last = pl.num_programs(1) - 1
    slot = j % 2

    def hidden():
        xn = xn_ref[...]
        return (_silu(_dot(xn, wg_ref[...])) * _dot(xn, wu_ref[...])).astype(BF16)

    @pl.when(j == 0)
    def _():
        xn_ref[...] = _rms(x_ref[...], gpre_ref[...]).astype(BF16)
        acc_ref[...] = jnp.zeros_like(acc_ref)
        h_ref[0] = hidden()

    @pl.when(jnp.logical_and(j > 0, j < last))
    def _():
        h_new = hidden()
        acc_ref[...] += _dot(h_ref[1 - slot], wo_ref[...])
        h_ref[slot] = h_new

    @pl.when(j == last)
    def _():
        acc = acc_ref[...] + _dot(h_ref[1 - slot], wo_ref[...])
        o_ref[...] = x_ref[...] + _rms(acc, gpost_ref[...])


def _ffn(x, gpre, w_in, w_out, gpost):
    t, d = x.shape
    f = w_out.shape[0]
    tm, th = _tile(t, 512), _tile(f, 512)
    nh = f // th
    return pl.pallas_call(
        _ffn_kernel,
        grid=(t // tm, nh + 1),
        in_specs=[
            pl.BlockSpec((tm, d), lambda i, j: (i, 0)),
            pl.BlockSpec((1, d), lambda i, j: (0, 0)),
            pl.BlockSpec((d, th), lambda i, j: (0, jnp.minimum(j, nh - 1))),
            pl.BlockSpec((d, th), lambda i, j: (0, nh + jnp.minimum(j, nh - 1))),
            pl.BlockSpec((th, d), lambda i, j: (jnp.maximum(j - 1, 0), 0)),
            pl.BlockSpec((1, d), lambda i, j: (0, 0)),
        ],
        out_specs=pl.BlockSpec((tm, d), lambda i, j: (i, 0)),
        out_shape=jax.ShapeDtypeStruct((t, d), F32),
        scratch_shapes=[pltpu.VMEM((tm, d), BF16), pltpu.VMEM((tm, d), F32), pltpu.VMEM((2, tm, th), BF16)],
        compiler_params=_params("parallel", "arbitrary"),
        name="ffn",
    )(x, gpre.reshape(1, d), w_in, w_in, w_out, gpost.reshape(1, d))


def _ple_kernel(x_ref, p_ref, wg_ref, wp_ref, g_ref, o_ref):
    x = x_ref[...]
    gate = _sigmoid(_dot(x.astype(BF16), wg_ref[...]))
    proj = _dot(p_ref[...].astype(BF16), wp_ref[...])
    o_ref[...] = x + _rms(proj * gate, g_ref[...])


def _ple(x, p, layer, w_gate, w_proj, g):
    t, d = x.shape
    pd = p.shape[-1]
    tm = _tile(t, 512)
    return pl.pallas_call(
        _ple_kernel,
        grid=(t // tm,),
        in_specs=[
            pl.BlockSpec((tm, d), lambda i: (i, 0)),
            pl.BlockSpec((None, tm, pd), lambda i: (layer, i, 0)),
            pl.BlockSpec((d, d), lambda i: (0, 0)),
            pl.BlockSpec((pd, d), lambda i: (0, 0)),
            pl.BlockSpec((1, d), lambda i: (0, 0)),
        ],
        out_specs=pl.BlockSpec((tm, d), lambda i: (i, 0)),
        out_shape=jax.ShapeDtypeStruct((t, d), F32),
        compiler_params=_params("parallel"),
        name="ple",
    )(x, p, w_gate, w_proj, g.reshape(1, d))


def _trunk(x, p, lb, layers):
    depth = p.shape[0]
    batch, n, d = x.shape
    t = batch * n
    x = x.reshape(t, d)
    p = p.reshape(depth, t, p.shape[-1])
    cos_l, sin_l = _rope_tables(n)
    for l, w in enumerate(layers):
        u = _in_proj(x, w["norm_mix_pre"], w["w_in"])
        q_r, k_r, v_t = _attn_prep(u, cos_l, sin_l, w["q_norm"], w["k_norm"], n)
        a_out = _flash(q_r, k_r, v_t, batch, n)
        c_out = _conv(u, w["conv_dw"], w["conv_dw_b"], w["conv_ln_g"], w["conv_ln_b"], w["conv_pw"], n)
        o_fwd = _hgrn_pass(u, lb[l, 0], batch, n, reverse=False)
        h_out = _hgrn_pass(u, lb[l, 1], batch, n, reverse=True, o_prev=o_fwd, gn=w["hgrn_gn"])
        x = _out_proj(a_out, c_out, h_out, x, w["w_out"], w["norm_mix_post"])
        x = _ffn(x, w["norm_ffn_pre"], w["w_ffn_in"], w["w_ffn_out"], w["norm_ffn_post"])
        x = _ple(x, p, l, w["w_ple_gate"], w["w_ple_proj"], w["ple_norm"])
    return x.reshape(batch, n, d)


def kernel(x_prompt, x_sample, p_prompt, p_sample, norm_mix_pre, norm_mix_post, w_in, q_norm, k_norm, conv_dw, conv_dw_b, conv_ln_g, conv_ln_b, conv_pw, hgrn_lb, hgrn_gn, w_out, norm_ffn_pre, norm_ffn_post, w_ffn_in, w_ffn_out, w_ple_gate, w_ple_proj, ple_norm):
    sm = jax.nn.softmax(hgrn_lb.astype(F32), axis=0)
    lb = jnp.cumsum(sm, axis=0) - sm[0]
    vectors = dict(
        norm_mix_pre=norm_mix_pre, norm_mix_post=norm_mix_post, q_norm=q_norm, k_norm=k_norm,
        conv_dw=conv_dw, conv_dw_b=conv_dw_b, conv_ln_g=conv_ln_g, conv_ln_b=conv_ln_b,
        hgrn_gn=hgrn_gn, norm_ffn_pre=norm_ffn_pre, norm_ffn_post=norm_ffn_post, ple_norm=ple_norm,
    )
    matrices = dict(
        w_in=w_in, conv_pw=conv_pw, w_out=w_out, w_ffn_in=w_ffn_in, w_ffn_out=w_ffn_out,
        w_ple_gate=w_ple_gate, w_ple_proj=w_ple_proj,
    )
    layers = [
        {**{k: v[l] for k, v in vectors.items()}, **{k: v[l].astype(BF16) for k, v in matrices.items()}}
        for l in range(p_prompt.shape[0])
    ]
    y_prompt = _trunk(x_prompt, p_prompt, lb, layers)
    y_sample = _trunk(x_sample, p_sample, lb, layers)
    return (y_prompt, y_sample)
```

```python
import functools

import numpy as np
import jax
import jax.numpy as jnp
from jax import lax
from jax.experimental import pallas as pl
from jax.experimental.pallas import tpu as pltpu

F32 = jnp.float32
BF16 = jnp.bfloat16

NORM_EPS = 1e-6
ROPE_THETA = 10000.0
LOG2_E = 1.4426950408889634
GRID_W = 64
HEAD_DIM = 128
N_Q_HEADS = 8
N_KV_HEADS = 2
Q_GROUP = N_Q_HEADS // N_KV_HEADS
ATTN_W = N_Q_HEADS * HEAD_DIM
KV_W = N_KV_HEADS * HEAD_DIM
CONV_W = 512
CONV_K = 31
CONV_HALO = 16
SUBLANES = 8
HGRN_W = 512
HGRN_D = 128
HGRN_HEADS = HGRN_W // HGRN_D
HGRN_CHUNK = 128
HGRN_CHUNKS_PER_STEP = 8
FLASH_ONES_ROWS = 16
FLASH_LOOKAHEAD = 3
FLASH_TQ = 256
FLASH_TK = 4096
FLASH_SUB = 512

OFF_Q = 0
OFF_K = OFF_Q + ATTN_W
OFF_V = OFF_K + KV_W
OFF_CA = OFF_V + KV_W
OFF_CG = OFF_CA + CONV_W
OFF_HQ = OFF_CG + CONV_W
OFF_FF = OFF_HQ + HGRN_W
OFF_FB = OFF_FF + HGRN_W
OFF_HI = OFF_FB + HGRN_W
OFF_HG = OFF_HI + HGRN_W
IN_COLS = OFF_HG + HGRN_W

V7X_VMEM_LIMIT_BYTES = 56 * 1024 * 1024


def _params(*semantics):
    return pltpu.CompilerParams(dimension_semantics=semantics, vmem_limit_bytes=V7X_VMEM_LIMIT_BYTES)


def _tile(dim, pref):
    t = min(dim, pref)
    assert dim % t == 0, (dim, pref)
    return t


def _rms(x, g):
    return x * lax.rsqrt(jnp.mean(x * x, axis=-1, keepdims=True) + NORM_EPS) * g


def _sigmoid(x):
    return 1.0 / (1.0 + jnp.exp(-x))


def _silu(x):
    return x * _sigmoid(x)


def _dot(a, b):
    return jnp.dot(a, b, preferred_element_type=F32)


def _dot_nt(a, b):
    return lax.dot_general(a, b, (((1,), (1,)), ((), ())), preferred_element_type=F32)


def _in_proj_kernel(x_ref, g_ref, w_ref, o_ref, xn_ref):
    @pl.when(pl.program_id(1) == 0)
    def _():
        xn_ref[...] = _rms(x_ref[...], g_ref[...]).astype(BF16)

    o_ref[...] = _dot(xn_ref[...], w_ref[...]).astype(o_ref.dtype)


def _in_proj(x, g, w):
    t, d = x.shape
    n = w.shape[1]
    tm, tn = _tile(t, 1024), _tile(n, 1024)
    return pl.pallas_call(
        _in_proj_kernel,
        grid=(t // tm, n // tn),
        in_specs=[
            pl.BlockSpec((tm, d), lambda i, j: (i, 0)),
            pl.BlockSpec((1, d), lambda i, j: (0, 0)),
            pl.BlockSpec((d, tn), lambda i, j: (0, j)),
        ],
        out_specs=pl.BlockSpec((tm, tn), lambda i, j: (i, j)),
        out_shape=jax.ShapeDtypeStruct((t, n), BF16),
        scratch_shapes=[pltpu.VMEM((tm, d), BF16)],
        compiler_params=_params("parallel", "arbitrary"),
        name="in_proj",
    )(x, g.reshape(1, d), w)


def _rope_tables(n):
    rows = n // GRID_W
    row = jnp.repeat(jnp.arange(rows, dtype=F32), GRID_W)
    col = jnp.tile(jnp.arange(GRID_W, dtype=F32), rows)
    axis_dims = HEAD_DIM // 2
    inv = ROPE_THETA ** (-jnp.arange(0, axis_dims, 2, dtype=F32) / axis_dims)
    ang = jnp.concatenate([row[:, None] * inv, col[:, None] * inv], axis=-1)
    cos, sin = jnp.cos(ang), jnp.sin(ang)
    cos_l = jnp.repeat(cos, 2, axis=-1)
    sin_l = jnp.stack([-sin, sin], axis=-1).reshape(n, HEAD_DIM)
    return cos_l, sin_l


def _attn_prep_kernel(q_ref, k_ref, v_ref, cos_ref, sin_ref, qg_ref, kg_ref, qo_ref, ko_ref, vt_ref):
    cos = cos_ref[...]
    sin = sin_ref[...]
    lane = lax.broadcasted_iota(jnp.int32, cos.shape, 1)
    even = (lane & 1) == 0

    def norm_rope(x, g):
        y = _rms(x.astype(F32), g)
        partner = jnp.where(even, pltpu.roll(y, HEAD_DIM - 1, 1), pltpu.roll(y, 1, 1))
        return y * cos + partner * sin

    scale = HEAD_DIM ** -0.5 * LOG2_E
    for h in range(N_Q_HEADS):
        sl = slice(h * HEAD_DIM, (h + 1) * HEAD_DIM)
        qo_ref[:, sl] = (norm_rope(q_ref[:, sl], qg_ref[...]) * scale).astype(qo_ref.dtype)
    for h in range(N_KV_HEADS):
        sl = slice(h * HEAD_DIM, (h + 1) * HEAD_DIM)
        ko_ref[:, sl] = norm_rope(k_ref[:, sl], kg_ref[...]).astype(ko_ref.dtype)
    vt_ref[...] = v_ref[...].astype(F32).T.astype(vt_ref.dtype)


def _attn_prep(u, cos_l, sin_l, q_gain, k_gain, n):
    t = u.shape[0]
    tm = _tile(n, 512)
    per_seq = n // tm
    return pl.pallas_call(
        _attn_prep_kernel,
        grid=(t // tm,),
        in_specs=[
            pl.BlockSpec((tm, ATTN_W), lambda i: (i, OFF_Q // ATTN_W)),
            pl.BlockSpec((tm, KV_W), lambda i: (i, OFF_K // KV_W)),
            pl.BlockSpec((tm, KV_W), lambda i: (i, OFF_V // KV_W)),
            pl.BlockSpec((tm, HEAD_DIM), lambda i: (i % per_seq, 0)),
            pl.BlockSpec((tm, HEAD_DIM), lambda i: (i % per_seq, 0)),
            pl.BlockSpec((1, HEAD_DIM), lambda i: (0, 0)),
            pl.BlockSpec((1, HEAD_DIM), lambda i: (0, 0)),
        ],
        out_specs=[
            pl.BlockSpec((tm, ATTN_W), lambda i: (i, 0)),
            pl.BlockSpec((tm, KV_W), lambda i: (i, 0)),
            pl.BlockSpec((KV_W, tm), lambda i: (0, i)),
        ],
        out_shape=[
            jax.ShapeDtypeStruct((t, ATTN_W), BF16),
            jax.ShapeDtypeStruct((t, KV_W), BF16),
            jax.ShapeDtypeStruct((KV_W, t), BF16),
        ],
        compiler_params=_params("parallel"),
        name="attn_prep",
    )(u, u, u, cos_l, sin_l, q_gain.reshape(1, HEAD_DIM), k_gain.reshape(1, HEAD_DIM))


def _flash_kernel(q_ref, k_ref, vt_ref, o_ref, m_ref, acc_ref, *, tk, sub):
    ki = pl.program_id(3)

    @pl.when(ki == 0)
    def _():
        m_ref[...] = jnp.full_like(m_ref, -jnp.inf)
        acc_ref[...] = jnp.zeros_like(acc_ref)

    stages = [(s0, g) for s0 in range(0, tk, sub) for g in range(Q_GROUP)]

    def scores(stage):
        s0, g = stage
        return _dot_nt(k_ref[s0:s0 + sub, :], q_ref[:, g * HEAD_DIM:(g + 1) * HEAD_DIM])

    ones = jnp.ones((FLASH_ONES_ROWS, sub), BF16)

    pending = [scores(st) for st in stages[:FLASH_LOOKAHEAD]]
    for i, (s0, g) in enumerate(stages):
        s = pending.pop(0)
        if i + FLASH_LOOKAHEAD < len(stages):
            pending.append(scores(stages[i + FLASH_LOOKAHEAD]))
        m_prev = m_ref[g]
        m_new = jnp.maximum(m_prev, jnp.max(s, axis=0, keepdims=True))
        alpha = jnp.exp2(m_prev - m_new)
        p = jnp.exp2((s - m_new).astype(BF16))
        vt1 = jnp.concatenate([vt_ref[:, s0:s0 + sub], ones], axis=0)
        acc_ref[g] = alpha * acc_ref[g] + _dot(vt1, p)
        m_ref[g] = m_new

    @pl.when(ki == pl.num_programs(3) - 1)
    def _():
        for g in range(Q_GROUP):
            out = acc_ref[g, 0:HEAD_DIM, :] / acc_ref[g, HEAD_DIM:HEAD_DIM + 1, :]
            o_ref[:, g * HEAD_DIM:(g + 1) * HEAD_DIM] = out.T.astype(o_ref.dtype)


def _flash(q, k, vt, batch, n):
    t = q.shape[0]
    tq, tk = _tile(n, FLASH_TQ), _tile(n, FLASH_TK)
    sub = _tile(tk, FLASH_SUB)
    nq, nk = n // tq, n // tk
    gw = Q_GROUP * HEAD_DIM
    return pl.pallas_call(
        functools.partial(_flash_kernel, tk=tk, sub=sub),
        grid=(batch, N_KV_HEADS, nq, nk),
        in_specs=[
            pl.BlockSpec((tq, gw), lambda b, h, qi, ki: (b * nq + qi, h)),
            pl.BlockSpec((tk, HEAD_DIM), lambda b, h, qi, ki: (b * nk + ki, h)),
            pl.BlockSpec((HEAD_DIM, tk), lambda b, h, qi, ki: (h, b * nk + ki)),
        ],
        out_specs=pl.BlockSpec((tq, gw), lambda b, h, qi, ki: (b * nq + qi, h)),
        out_shape=jax.ShapeDtypeStruct((t, ATTN_W), BF16),
        scratch_shapes=[
            pltpu.VMEM((Q_GROUP, 1, tq), F32),
            pltpu.VMEM((Q_GROUP, HEAD_DIM + FLASH_ONES_ROWS, tq), F32),
        ],
        compiler_params=_params("parallel", "parallel", "parallel", "arbitrary"),
        name="flash_attn",
    )(q, k, vt)


def _conv_kernel(a_ref, g_ref, ap_ref, gp_ref, an_ref, gn_ref, dw_ref, dwb_ref, lng_ref, lnb_ref, pw_ref,
                 o_ref, h_ref, sh_ref, *, tm, per_seq):
    i = pl.program_id(0)
    pos = i % per_seq

    def glu(a, g):
        return a.astype(F32) * _sigmoid(g.astype(F32))

    prev = jnp.where(pos == 0, 0.0, glu(ap_ref[...], gp_ref[...]))
    nxt = jnp.where(pos == per_seq - 1, 0.0, glu(an_ref[...], gn_ref[...]))
    h_ref[0:CONV_HALO, :] = prev
    h_ref[CONV_HALO:CONV_HALO + tm, :] = glu(a_ref[...], g_ref[...])
    h_ref[CONV_HALO + tm:, :] = nxt

    base = CONV_HALO - CONV_K // 2
    span = tm + 2 * CONV_HALO - SUBLANES
    for b in range(SUBLANES):
        sh_ref[b] = h_ref[b:b + span, :]
    acc = jnp.zeros((tm, CONV_W), F32) + dwb_ref[...]
    for j in range(CONV_K):
        a, b = divmod(base + j, SUBLANES)
        acc = acc + dw_ref[j:j + 1, :] * sh_ref[b, a * SUBLANES:a * SUBLANES + tm, :]
    mu = jnp.mean(acc, axis=-1, keepdims=True)
    cen = acc - mu
    var = jnp.mean(cen * cen, axis=-1, keepdims=True)
    y = cen * lax.rsqrt(var + NORM_EPS) * lng_ref[...] + lnb_ref[...]
    o_ref[...] = _dot(_silu(y).astype(BF16), pw_ref[...]).astype(o_ref.dtype)


def _conv(u, dw, dwb, lng, lnb, pw, n):
    t = u.shape[0]
    tm = _tile(n, 512)
    per_seq = n // tm
    hb = tm // CONV_HALO
    last = t // CONV_HALO - 1
    ca, cg = OFF_CA // CONV_W, OFF_CG // CONV_W
    vec = lambda v: v.reshape(1, CONV_W)
    row = pl.BlockSpec((1, CONV_W), lambda i: (0, 0))
    return pl.pallas_call(
        functools.partial(_conv_kernel, tm=tm, per_seq=per_seq),
        grid=(t // tm,),
        in_specs=[
            pl.BlockSpec((tm, CONV_W), lambda i: (i, ca)),
            pl.BlockSpec((tm, CONV_W), lambda i: (i, cg)),
            pl.BlockSpec((CONV_HALO, CONV_W), lambda i: (jnp.maximum(i * hb - 1, 0), ca)),
            pl.BlockSpec((CONV_HALO, CONV_W), lambda i: (jnp.maximum(i * hb - 1, 0), cg)),
            pl.BlockSpec((CONV_HALO, CONV_W), lambda i: (jnp.minimum((i + 1) * hb, last), ca)),
            pl.BlockSpec((CONV_HALO, CONV_W), lambda i: (jnp.minimum((i + 1) * hb, last), cg)),
            pl.BlockSpec((CONV_K, CONV_W), lambda i: (0, 0)),
            row, row, row,
            pl.BlockSpec((CONV_W, CONV_W), lambda i: (0, 0)),
        ],
        out_specs=pl.BlockSpec((tm, CONV_W), lambda i: (i, 0)),
        out_shape=jax.ShapeDtypeStruct((t, CONV_W), BF16),
        scratch_shapes=[
            pltpu.VMEM((tm + 2 * CONV_HALO, CONV_W), F32),
            pltpu.VMEM((SUBLANES, tm + 2 * CONV_HALO - SUBLANES, CONV_W), F32),
        ],
        compiler_params=_params("parallel"),
        name="conv_mixer",
    )(u, u, u, u, u, u, dw, vec(dwb), vec(lng), vec(lnb), pw)


def _hgrn_tables(reverse):
    c = HGRN_CHUNK
    levels = []
    h = c // 2
    while h >= 1:
        levels.append(h)
        h //= 2
    r = np.arange(c)
    col = r[None, :]
    seg = np.zeros((len(levels) + 2, c, c), np.float32)
    mask = np.zeros((len(levels) + 1, c, c), np.float32)
    for li, h in enumerate(levels):
        blk = r // (2 * h)
        upper = (r % (2 * h)) >= h
        b = (blk * 2 * h + h - 1)[:, None]
        rr = r[:, None]
        seg[li] = np.where(upper[:, None], (col > b) & (col <= rr), (col > rr) & (col <= b))
        mask[li] = upper[:, None] & (~upper)[None, :] & (blk[:, None] == blk[None, :])
    seg[-2] = col <= r[:, None]
    seg[-1] = col > r[:, None]
    mask[-1] = np.eye(c)
    if reverse:
        seg = seg[:, ::-1, ::-1]
        mask = mask[:, ::-1, ::-1]
    seg = seg.reshape(-1, c)
    seg2 = np.concatenate([seg, seg], axis=1)
    return (jnp.asarray(seg2, dtype=BF16), jnp.asarray(mask, dtype=F32), len(levels))


def _hgrn_kernel(*refs, reverse, final, n_levels, n_chunks):
    if final:
        hq_ref, hf_ref, hi_ref, lb_ref, seg_ref, mask_ref, oprev_ref, hg_ref, gn_ref, o_ref, st_ref = refs
    else:
        hq_ref, hf_ref, hi_ref, lb_ref, seg_ref, mask_ref, o_ref, st_ref = refs
    c = HGRN_CHUNK

    @pl.when(pl.program_id(1) == 0)
    def _():
        st_ref[...] = jnp.zeros_like(st_ref)

    lb = lb_ref[...]
    seg = seg_ref[...]
    total_row = 0 if reverse else c - 1

    def prep(j):
        rows = slice(j * c, (j + 1) * c)
        f = lb + (1.0 - lb) * _sigmoid(hf_ref[rows, :].astype(F32))
        log2f = jnp.log(f) * LOG2_E
        g_hi = log2f.astype(BF16)
        g_lo = (log2f - g_hi.astype(F32)).astype(BF16)
        e_all = jnp.exp2(_dot(seg, jnp.concatenate([g_hi, g_lo], axis=0)))
        decay = e_all[n_levels * c + total_row:n_levels * c + total_row + 1, :]
        return _silu(hq_ref[rows, :].astype(F32)).astype(BF16), (1.0 - f).astype(BF16), e_all.astype(BF16), decay

    def intra(prepped):
        qq, kk, e_all, _ = prepped
        mats = []
        for hd in range(HGRN_HEADS):
            sl = slice(hd * HGRN_D, (hd + 1) * HGRN_D)
            qh, kh = qq[:, sl], kk[:, sl]
            a = mask_ref[n_levels] * _dot_nt(qh, kh)
            for l in range(n_levels):
                el = e_all[l * c:(l + 1) * c, sl]
                a = a + mask_ref[l] * _dot_nt(qh * el, kh * el)
            mats.append(a.astype(BF16))
        return mats

    def outputs(j, prepped, mats):
        qq, kk, e_all, decay = prepped
        rows = slice(j * c, (j + 1) * c)
        if final:
            gate = _silu(hg_ref[rows, :].astype(F32)) * gn_ref[...]
        for hd in range(HGRN_HEADS):
            sl = slice(hd * HGRN_D, (hd + 1) * HGRN_D)
            qh, kh, vh = qq[:, sl], kk[:, sl], hi_ref[rows, sl]
            e_pre = e_all[n_levels * c:(n_levels + 1) * c, sl]
            e_suf = e_all[(n_levels + 1) * c:(n_levels + 2) * c, sl]
            st = st_ref[hd]
            o = _dot(mats[hd], vh) + _dot_nt(qh * e_pre, st.astype(BF16))
            v_t = vh.astype(F32).T.astype(BF16)
            st_ref[hd] = decay[:, sl] * st + _dot(v_t, kh * e_suf)
            if final:
                o = o + oprev_ref[rows, sl]
                o = o * lax.rsqrt(jnp.mean(o * o, axis=-1, keepdims=True) + NORM_EPS)
                o_ref[rows, sl] = (o * gate[:, sl]).astype(o_ref.dtype)
            else:
                o_ref[rows, sl] = o

    order = list(range(n_chunks))[::-1] if reverse else list(range(n_chunks))
    prepped, mats = {}, {}
    for step in range(n_chunks + 2):
        if step < n_chunks:
            prepped[step] = prep(order[step])
        if 0 <= step - 1 < n_chunks:
            mats[step - 1] = intra(prepped[step - 1])
        if 0 <= step - 2 < n_chunks:
            outputs(order[step - 2], prepped.pop(step - 2), mats.pop(step - 2))


def _hgrn_pass(u, lb, batch, n, *, reverse, o_prev=None, gn=None):
    t = u.shape[0]
    n_chunks = _tile(n // HGRN_CHUNK, HGRN_CHUNKS_PER_STEP)
    c = n_chunks * HGRN_CHUNK
    nc = n // c
    final = o_prev is not None
    seg, mask, n_levels = _hgrn_tables(reverse)

    def rows(b, ci):
        return b * nc + (nc - 1 - ci if reverse else ci)

    def col_spec(off):
        return pl.BlockSpec((c, HGRN_W), lambda b, ci: (rows(b, ci), off // HGRN_W))

    const2 = lambda b, ci: (0, 0)
    in_specs = [
        col_spec(OFF_HQ),
        col_spec(OFF_FB if reverse else OFF_FF),
        col_spec(OFF_HI),
        pl.BlockSpec((1, HGRN_W), const2),
        pl.BlockSpec(seg.shape, const2),
        pl.BlockSpec(mask.shape, lambda b, ci: (0, 0, 0)),
    ]
    args = [u, u, u, lb.reshape(1, HGRN_W), seg, mask]
    if final:
        in_specs += [
            pl.BlockSpec((c, HGRN_W), lambda b, ci: (rows(b, ci), 0)),
            col_spec(OFF_HG),
            pl.BlockSpec((1, HGRN_W), const2),
        ]
        args += [o_prev, u, gn.reshape(1, HGRN_W)]
    return pl.pallas_call(
        functools.partial(_hgrn_kernel, reverse=reverse, final=final, n_levels=n_levels, n_chunks=n_chunks),
        grid=(batch, nc),
        in_specs=in_specs,
        out_specs=pl.BlockSpec((c, HGRN_W), lambda b, ci: (rows(b, ci), 0)),
        out_shape=jax.ShapeDtypeStruct((t, HGRN_W), BF16 if final else F32),
        scratch_shapes=[pltpu.VMEM((HGRN_HEADS, HGRN_D, HGRN_D), F32)],
        compiler_params=_params("parallel", "arbitrary"),
        name="hgrn_bwd" if reverse else "hgrn_fwd",
    )(*args)


def _out_proj_kernel(a_ref, c_ref, h_ref, x_ref, w_ref, g_ref, o_ref):
    mix = _dot(a_ref[...], w_ref[0:ATTN_W, :])
    mix = mix + _dot(c_ref[...], w_ref[ATTN_W:ATTN_W + CONV_W, :])
    mix = mix + _dot(h_ref[...], w_ref[ATTN_W + CONV_W:, :])
    o_ref[...] = x_ref[...] + _rms(mix, g_ref[...])


def _out_proj(a, cv, hg, x, w, g):
    t, d = x.shape
    tm = _tile(t, 512)
    mixw = w.shape[0]
    return pl.pallas_call(
        _out_proj_kernel,
        grid=(t // tm,),
        in_specs=[
            pl.BlockSpec((tm, ATTN_W), lambda i: (i, 0)),
            pl.BlockSpec((tm, CONV_W), lambda i: (i, 0)),
            pl.BlockSpec((tm, HGRN_W), lambda i: (i, 0)),
            pl.BlockSpec((tm, d), lambda i: (i, 0)),
            pl.BlockSpec((mixw, d), lambda i: (0, 0)),
            pl.BlockSpec((1, d), lambda i: (0, 0)),
        ],
        out_specs=pl.BlockSpec((tm, d), lambda i: (i, 0)),
        out_shape=jax.ShapeDtypeStruct((t, d), F32),
        compiler_params=_params("parallel"),
        name="out_proj",
    )(a, cv, hg, x, w, g.reshape(1, d))


def _ffn_kernel(x_ref, gpre_ref, wg_ref, wu_ref, wo_ref, gpost_ref, o_ref, xn_ref, acc_ref):
    j = pl.program_id(1)

    @pl.when(j == 0)
    def _():
        xn_ref[...] = _rms(x_ref[...], gpre_ref[...]).astype(BF16)
        acc_ref[...] = jnp.zeros_like(acc_ref)

    xn = xn_ref[...]
    hidden = _silu(_dot(xn, wg_ref[...])) * _dot(xn, wu_ref[...])
    acc_ref[...] += _dot(hidden.astype(BF16), wo_ref[...])

    @pl.when(j == pl.num_programs(1) - 1)
    def _():
        o_ref[...] = x_ref[...] + _rms(acc_ref[...], gpost_ref[...])


def _ffn(x, gpre, w_in, w_out, gpost):
    t, d = x.shape
    f = w_out.shape[0]
    tm, th = _tile(t, 512), _tile(f, 512)
    nh = f // th
    return pl.pallas_call(
        _ffn_kernel,
        grid=(t // tm, nh),
        in_specs=[
            pl.BlockSpec((tm, d), lambda i, j: (i, 0)),
            pl.BlockSpec((1, d), lambda i, j: (0, 0)),
            pl.BlockSpec((d, th), lambda i, j: (0, j)),
            pl.BlockSpec((d, th), lambda i, j: (0, nh + j)),
            pl.BlockSpec((th, d), lambda i, j: (j, 0)),
            pl.BlockSpec((1, d), lambda i, j: (0, 0)),
        ],
        out_specs=pl.BlockSpec((tm, d), lambda i, j: (i, 0)),
        out_shape=jax.ShapeDtypeStruct((t, d), F32),
        scratch_shapes=[pltpu.VMEM((tm, d), BF16), pltpu.VMEM((tm, d), F32)],
        compiler_params=_params("parallel", "arbitrary"),
        name="ffn",
    )(x, gpre.reshape(1, d), w_in, w_in, w_out, gpost.reshape(1, d))


def _ple_kernel(x_ref, p_ref, wg_ref, wp_ref, g_ref, o_ref):
    x = x_ref[...]
    gate = _sigmoid(_dot(x.astype(BF16), wg_ref[...]))
    proj = _dot(p_ref[...].astype(BF16), wp_ref[...])
    o_ref[...] = x + _rms(proj * gate, g_ref[...])


def _ple(x, p, layer, w_gate, w_proj, g):
    t, d = x.shape
    pd = p.shape[-1]
    tm = _tile(t, 512)
    return pl.pallas_call(
        _ple_kernel,
        grid=(t // tm,),
        in_specs=[
            pl.BlockSpec((tm, d), lambda i: (i, 0)),
            pl.BlockSpec((None, tm, pd), lambda i: (layer, i, 0)),
            pl.BlockSpec((d, d), lambda i: (0, 0)),
            pl.BlockSpec((pd, d), lambda i: (0, 0)),
            pl.BlockSpec((1, d), lambda i: (0, 0)),
        ],
        out_specs=pl.BlockSpec((tm, d), lambda i: (i, 0)),
        out_shape=jax.ShapeDtypeStruct((t, d), F32),
        compiler_params=_params("parallel"),
        name="ple",
    )(x, p, w_gate, w_proj, g.reshape(1, d))


def _trunk(x, p, lb, layers):
    depth = p.shape[0]
    batch, n, d = x.shape
    t = batch * n
    x = x.reshape(t, d)
    p = p.reshape(depth, t, p.shape[-1])
    cos_l, sin_l = _rope_tables(n)
    for l, w in enumerate(layers):
        u = _in_proj(x, w["norm_mix_pre"], w["w_in"])
        q_r, k_r, v_t = _attn_prep(u, cos_l, sin_l, w["q_norm"], w["k_norm"], n)
        a_out = _flash(q_r, k_r, v_t, batch, n)
        c_out = _conv(u, w["conv_dw"], w["conv_dw_b"], w["conv_ln_g"], w["conv_ln_b"], w["conv_pw"], n)
        o_fwd = _hgrn_pass(u, lb[l, 0], batch, n, reverse=False)
        h_out = _hgrn_pass(u, lb[l, 1], batch, n, reverse=True, o_prev=o_fwd, gn=w["hgrn_gn"])
        x = _out_proj(a_out, c_out, h_out, x, w["w_out"], w["norm_mix_post"])
        x = _ffn(x, w["norm_ffn_pre"], w["w_ffn_in"], w["w_ffn_out"], w["norm_ffn_post"])
        x = _ple(x, p, l, w["w_ple_gate"], w["w_ple_proj"], w["ple_norm"])
    return x.reshape(batch, n, d)


def kernel(x_prompt, x_sample, p_prompt, p_sample, norm_mix_pre, norm_mix_post, w_in, q_norm, k_norm, conv_dw, conv_dw_b, conv_ln_g, conv_ln_b, conv_pw, hgrn_lb, hgrn_gn, w_out, norm_ffn_pre, norm_ffn_post, w_ffn_in, w_ffn_out, w_ple_gate, w_ple_proj, ple_norm):
    sm = jax.nn.softmax(hgrn_lb.astype(F32), axis=0)
    lb = jnp.cumsum(sm, axis=0) - sm[0]
    vectors = dict(
        norm_mix_pre=norm_mix_pre, norm_mix_post=norm_mix_post, q_norm=q_norm, k_norm=k_norm,
        conv_dw=conv_dw, conv_dw_b=conv_dw_b, conv_ln_g=conv_ln_g, conv_ln_b=conv_ln_b,
        hgrn_gn=hgrn_gn, norm_ffn_pre=norm_ffn_pre, norm_ffn_post=norm_ffn_post, ple_norm=ple_norm,
    )
    matrices = dict(
        w_in=w_in, conv_pw=conv_pw, w_out=w_out, w_ffn_in=w_ffn_in, w_ffn_out=w_ffn_out,
        w_ple_gate=w_ple_gate, w_ple_proj=w_ple_proj,
    )
    layers = [
        {**{k: v[l] for k, v in vectors.items()}, **{k: v[l].astype(BF16) for k, v in matrices.items()}}
        for l in range(p_prompt.shape[0])
    ]
    y_prompt = _trunk(x_prompt, p_prompt, lb, layers)
    y_sample = _trunk(x_sample, p_sample, lb, layers)
    return (y_prompt, y_sample)
```

```python
import functools

import numpy as np
import jax
import jax.numpy as jnp
from jax import lax
from jax.experimental import pallas as pl
from jax.experimental.pallas import tpu as pltpu

F32 = jnp.float32
BF16 = jnp.bfloat16

NORM_EPS = 1e-6
ROPE_THETA = 10000.0
LOG2_E = 1.4426950408889634
GRID_W = 64
HEAD_DIM = 128
N_Q_HEADS = 8
N_KV_HEADS = 2
Q_GROUP = N_Q_HEADS // N_KV_HEADS
ATTN_W = N_Q_HEADS * HEAD_DIM
KV_W = N_KV_HEADS * HEAD_DIM
CONV_W = 512
CONV_K = 31
CONV_HALO = 16
SUBLANES = 8
HGRN_W = 512
HGRN_D = 128
HGRN_HEADS = HGRN_W // HGRN_D
HGRN_CHUNK = 128
HGRN_CHUNKS_PER_STEP = 8
FLASH_ONES_ROWS = 16
FLASH_LOOKAHEAD = 3
FLASH_TQ = 256
FLASH_TK = 8192
FLASH_SUB = 512
FLASH_FIXED_SHIFT_MAX_RANGE = 96.0

OFF_Q = 0
OFF_K = OFF_Q + ATTN_W
OFF_V = OFF_K + KV_W
OFF_CA = OFF_V + KV_W
OFF_CG = OFF_CA + CONV_W
OFF_HQ = OFF_CG + CONV_W
OFF_FF = OFF_HQ + HGRN_W
OFF_FB = OFF_FF + HGRN_W
OFF_HI = OFF_FB + HGRN_W
OFF_HG = OFF_HI + HGRN_W
IN_COLS = OFF_HG + HGRN_W

V7X_VMEM_LIMIT_BYTES = 56 * 1024 * 1024


def _params(*semantics):
    return pltpu.CompilerParams(dimension_semantics=semantics, vmem_limit_bytes=V7X_VMEM_LIMIT_BYTES)


def _tile(dim, pref):
    t = min(dim, pref)
    assert dim % t == 0, (dim, pref)
    return t


def _rms(x, g):
    return x * lax.rsqrt(jnp.mean(x * x, axis=-1, keepdims=True) + NORM_EPS) * g


def _sigmoid(x):
    return 1.0 / (1.0 + jnp.exp(-x))


def _silu(x):
    return x * _sigmoid(x)


def _dot(a, b):
    return jnp.dot(a, b, preferred_element_type=F32)


def _dot_nt(a, b):
    return lax.dot_general(a, b, (((1,), (1,)), ((), ())), preferred_element_type=F32)


def _in_proj_kernel(x_ref, g_ref, w_ref, o_ref, xn_ref):
    @pl.when(pl.program_id(1) == 0)
    def _():
        xn_ref[...] = _rms(x_ref[...], g_ref[...]).astype(BF16)

    o_ref[...] = _dot(xn_ref[...], w_ref[...]).astype(o_ref.dtype)


def _in_proj(x, g, w):
    t, d = x.shape
    n = w.shape[1]
    tm, tn = _tile(t, 1024), _tile(n, 1024)
    return pl.pallas_call(
        _in_proj_kernel,
        grid=(t // tm, n // tn),
        in_specs=[
            pl.BlockSpec((tm, d), lambda i, j: (i, 0)),
            pl.BlockSpec((1, d), lambda i, j: (0, 0)),
            pl.BlockSpec((d, tn), lambda i, j: (0, j)),
        ],
        out_specs=pl.BlockSpec((tm, tn), lambda i, j: (i, j)),
        out_shape=jax.ShapeDtypeStruct((t, n), BF16),
        scratch_shapes=[pltpu.VMEM((tm, d), BF16)],
        compiler_params=_params("parallel", "arbitrary"),
        name="in_proj",
    )(x, g.reshape(1, d), w)


def _rope_tables(n):
    rows = n // GRID_W
    row = jnp.repeat(jnp.arange(rows, dtype=F32), GRID_W)
    col = jnp.tile(jnp.arange(GRID_W, dtype=F32), rows)
    axis_dims = HEAD_DIM // 2
    inv = ROPE_THETA ** (-jnp.arange(0, axis_dims, 2, dtype=F32) / axis_dims)
    ang = jnp.concatenate([row[:, None] * inv, col[:, None] * inv], axis=-1)
    cos, sin = jnp.cos(ang), jnp.sin(ang)
    cos_l = jnp.repeat(cos, 2, axis=-1)
    sin_l = jnp.stack([-sin, sin], axis=-1).reshape(n, HEAD_DIM)
    return cos_l, sin_l


def _attn_prep_kernel(q_ref, k_ref, v_ref, cos_ref, sin_ref, qg_ref, kg_ref, qo_ref, ko_ref, vt_ref):
    cos = cos_ref[...]
    sin = sin_ref[...]
    lane = lax.broadcasted_iota(jnp.int32, cos.shape, 1)
    even = (lane & 1) == 0

    def norm_rope(x, g):
        y = _rms(x.astype(F32), g)
        partner = jnp.where(even, pltpu.roll(y, HEAD_DIM - 1, 1), pltpu.roll(y, 1, 1))
        return y * cos + partner * sin

    scale = HEAD_DIM ** -0.5 * LOG2_E
    for h in range(N_Q_HEADS):
        sl = slice(h * HEAD_DIM, (h + 1) * HEAD_DIM)
        qo_ref[:, sl] = (norm_rope(q_ref[:, sl], qg_ref[...]) * scale).astype(qo_ref.dtype)
    for h in range(N_KV_HEADS):
        sl = slice(h * HEAD_DIM, (h + 1) * HEAD_DIM)
        ko_ref[:, sl] = norm_rope(k_ref[:, sl], kg_ref[...]).astype(ko_ref.dtype)
    vt_ref[...] = v_ref[...].astype(F32).T.astype(vt_ref.dtype)


def _attn_prep(u, cos_l, sin_l, q_gain, k_gain, n):
    t = u.shape[0]
    tm = _tile(n, 512)
    per_seq = n // tm
    return pl.pallas_call(
        _attn_prep_kernel,
        grid=(t // tm,),
        in_specs=[
            pl.BlockSpec((tm, ATTN_W), lambda i: (i, OFF_Q // ATTN_W)),
            pl.BlockSpec((tm, KV_W), lambda i: (i, OFF_K // KV_W)),
            pl.BlockSpec((tm, KV_W), lambda i: (i, OFF_V // KV_W)),
            pl.BlockSpec((tm, HEAD_DIM), lambda i: (i % per_seq, 0)),
            pl.BlockSpec((tm, HEAD_DIM), lambda i: (i % per_seq, 0)),
            pl.BlockSpec((1, HEAD_DIM), lambda i: (0, 0)),
            pl.BlockSpec((1, HEAD_DIM), lambda i: (0, 0)),
        ],
        out_specs=[
            pl.BlockSpec((tm, ATTN_W), lambda i: (i, 0)),
            pl.BlockSpec((tm, KV_W), lambda i: (i, 0)),
            pl.BlockSpec((KV_W, tm), lambda i: (0, i)),
        ],
        out_shape=[
            jax.ShapeDtypeStruct((t, ATTN_W), BF16),
            jax.ShapeDtypeStruct((t, KV_W), BF16),
            jax.ShapeDtypeStruct((KV_W, t), BF16),
        ],
        compiler_params=_params("parallel"),
        name="attn_prep",
    )(u, u, u, cos_l, sin_l, q_gain.reshape(1, HEAD_DIM), k_gain.reshape(1, HEAD_DIM))


def _flash_kernel(shift_ref, q_ref, k_ref, vt_ref, o_ref, m_ref, acc_ref, *, tk, sub, fixed_shift):
    ki = pl.program_id(3)

    @pl.when(ki == 0)
    def _():
        m_ref[...] = jnp.full_like(m_ref, -jnp.inf)
        acc_ref[...] = jnp.zeros_like(acc_ref)

    stages = [(s0, g) for s0 in range(0, tk, sub) for g in range(Q_GROUP)]

    def scores(stage):
        s0, g = stage
        return _dot_nt(k_ref[s0:s0 + sub, :], q_ref[:, g * HEAD_DIM:(g + 1) * HEAD_DIM])

    ones = jnp.ones((FLASH_ONES_ROWS, sub), BF16)
    shift = shift_ref[0, 0]
    partial = [None] * Q_GROUP

    pending = [scores(st) for st in stages[:FLASH_LOOKAHEAD]]
    for i, (s0, g) in enumerate(stages):
        s = pending.pop(0)
        if i + FLASH_LOOKAHEAD < len(stages):
            pending.append(scores(stages[i + FLASH_LOOKAHEAD]))
        vt1 = jnp.concatenate([vt_ref[:, s0:s0 + sub], ones], axis=0)
        if fixed_shift:
            pv = _dot(vt1, jnp.exp2(s - shift).astype(BF16))
            partial[g] = pv if partial[g] is None else partial[g] + pv
        else:
            m_prev = m_ref[g]
            m_new = jnp.maximum(m_prev, jnp.max(s, axis=0, keepdims=True))
            alpha = jnp.exp2(m_prev - m_new)
            p = jnp.exp2((s - m_new).astype(BF16))
            acc_ref[g] = alpha * acc_ref[g] + _dot(vt1, p)
            m_ref[g] = m_new
    if fixed_shift:
        for g in range(Q_GROUP):
            acc_ref[g] += partial[g]

    @pl.when(ki == pl.num_programs(3) - 1)
    def _():
        for g in range(Q_GROUP):
            out = acc_ref[g, 0:HEAD_DIM, :] / acc_ref[g, HEAD_DIM:HEAD_DIM + 1, :]
            o_ref[:, g * HEAD_DIM:(g + 1) * HEAD_DIM] = out.T.astype(o_ref.dtype)


def _flash(q, k, vt, q_gain, k_gain, batch, n):
    t = q.shape[0]
    tq, tk = _tile(n, FLASH_TQ), _tile(n, FLASH_TK)
    sub = _tile(tk, FLASH_SUB)
    nq, nk = n // tq, n // tk
    gw = Q_GROUP * HEAD_DIM
    bound = (HEAD_DIM ** 0.5 * LOG2_E) * jnp.max(jnp.abs(q_gain)) * jnp.max(jnp.abs(k_gain))
    shift = bound.astype(F32).reshape(1, 1)

    def call(fixed_shift):
        return pl.pallas_call(
            functools.partial(_flash_kernel, tk=tk, sub=sub, fixed_shift=fixed_shift),
            grid=(batch, N_KV_HEADS, nq, nk),
            in_specs=[
                pl.BlockSpec(memory_space=pltpu.SMEM),
                pl.BlockSpec((tq, gw), lambda b, h, qi, ki: (b * nq + qi, h)),
                pl.BlockSpec((tk, HEAD_DIM), lambda b, h, qi, ki: (b * nk + ki, h)),
                pl.BlockSpec((HEAD_DIM, tk), lambda b, h, qi, ki: (h, b * nk + ki)),
            ],
            out_specs=pl.BlockSpec((tq, gw), lambda b, h, qi, ki: (b * nq + qi, h)),
            out_shape=jax.ShapeDtypeStruct((t, ATTN_W), BF16),
            scratch_shapes=[
                pltpu.VMEM((Q_GROUP, 1, tq), F32),
                pltpu.VMEM((Q_GROUP, HEAD_DIM + FLASH_ONES_ROWS, tq), F32),
            ],
            compiler_params=_params("parallel", "parallel", "parallel", "arbitrary"),
            name="flash_attn_fixed" if fixed_shift else "flash_attn",
        )(shift, q, k, vt)

    return lax.cond(2.0 * bound <= FLASH_FIXED_SHIFT_MAX_RANGE, lambda: call(True), lambda: call(False))


def _conv_kernel(a_ref, g_ref, ap_ref, gp_ref, an_ref, gn_ref, dw_ref, dwb_ref, lng_ref, lnb_ref, pw_ref,
                 o_ref, h_ref, sh_ref, *, tm, per_seq):
    i = pl.program_id(0)
    pos = i % per_seq

    def glu(a, g):
        return a.astype(F32) * _sigmoid(g.astype(F32))

    prev = jnp.where(pos == 0, 0.0, glu(ap_ref[...], gp_ref[...]))
    nxt = jnp.where(pos == per_seq - 1, 0.0, glu(an_ref[...], gn_ref[...]))
    h_ref[0:CONV_HALO, :] = prev
    h_ref[CONV_HALO:CONV_HALO + tm, :] = glu(a_ref[...], g_ref[...])
    h_ref[CONV_HALO + tm:, :] = nxt

    base = CONV_HALO - CONV_K // 2
    span = tm + 2 * CONV_HALO - SUBLANES
    for b in range(SUBLANES):
        sh_ref[b] = h_ref[b:b + span, :]
    acc = jnp.zeros((tm, CONV_W), F32) + dwb_ref[...]
    for j in range(CONV_K):
        a, b = divmod(base + j, SUBLANES)
        acc = acc + dw_ref[j:j + 1, :] * sh_ref[b, a * SUBLANES:a * SUBLANES + tm, :]
    mu = jnp.mean(acc, axis=-1, keepdims=True)
    cen = acc - mu
    var = jnp.mean(cen * cen, axis=-1, keepdims=True)
    y = cen * lax.rsqrt(var + NORM_EPS) * lng_ref[...] + lnb_ref[...]
    o_ref[...] = _dot(_silu(y).astype(BF16), pw_ref[...]).astype(o_ref.dtype)


def _conv(u, dw, dwb, lng, lnb, pw, n):
    t = u.shape[0]
    tm = _tile(n, 512)
    per_seq = n // tm
    hb = tm // CONV_HALO
    last = t // CONV_HALO - 1
    ca, cg = OFF_CA // CONV_W, OFF_CG // CONV_W
    vec = lambda v: v.reshape(1, CONV_W)
    row = pl.BlockSpec((1, CONV_W), lambda i: (0, 0))
    return pl.pallas_call(
        functools.partial(_conv_kernel, tm=tm, per_seq=per_seq),
        grid=(t // tm,),
        in_specs=[
            pl.BlockSpec((tm, CONV_W), lambda i: (i, ca)),
            pl.BlockSpec((tm, CONV_W), lambda i: (i, cg)),
            pl.BlockSpec((CONV_HALO, CONV_W), lambda i: (jnp.maximum(i * hb - 1, 0), ca)),
            pl.BlockSpec((CONV_HALO, CONV_W), lambda i: (jnp.maximum(i * hb - 1, 0), cg)),
            pl.BlockSpec((CONV_HALO, CONV_W), lambda i: (jnp.minimum((i + 1) * hb, last), ca)),
            pl.BlockSpec((CONV_HALO, CONV_W), lambda i: (jnp.minimum((i + 1) * hb, last), cg)),
            pl.BlockSpec((CONV_K, CONV_W), lambda i: (0, 0)),
            row, row, row,
            pl.BlockSpec((CONV_W, CONV_W), lambda i: (0, 0)),
        ],
        out_specs=pl.BlockSpec((tm, CONV_W), lambda i: (i, 0)),
        out_shape=jax.ShapeDtypeStruct((t, CONV_W), BF16),
        scratch_shapes=[
            pltpu.VMEM((tm + 2 * CONV_HALO, CONV_W), F32),
            pltpu.VMEM((SUBLANES, tm + 2 * CONV_HALO - SUBLANES, CONV_W), F32),
        ],
        compiler_params=_params("parallel"),
        name="conv_mixer",
    )(u, u, u, u, u, u, dw, vec(dwb), vec(lng), vec(lnb), pw)


def _hgrn_tables(reverse):
    c = HGRN_CHUNK
    levels = []
    h = c // 2
    while h >= 1:
        levels.append(h)
        h //= 2
    r = np.arange(c)
    col = r[None, :]
    seg = np.zeros((len(levels) + 2, c, c), np.float32)
    mask = np.zeros((len(levels) + 1, c, c), np.float32)
    for li, h in enumerate(levels):
        blk = r // (2 * h)
        upper = (r % (2 * h)) >= h
        b = (blk * 2 * h + h - 1)[:, None]
        rr = r[:, None]
        seg[li] = np.where(upper[:, None], (col > b) & (col <= rr), (col > rr) & (col <= b))
        mask[li] = upper[:, None] & (~upper)[None, :] & (blk[:, None] == blk[None, :])
    seg[-2] = col <= r[:, None]
    seg[-1] = col > r[:, None]
    mask[-1] = np.eye(c)
    if reverse:
        seg = seg[:, ::-1, ::-1]
        mask = mask[:, ::-1, ::-1]
    seg = seg.reshape(-1, c)
    seg2 = np.concatenate([seg, seg], axis=1)
    return (jnp.asarray(seg2, dtype=BF16), jnp.asarray(mask, dtype=F32), len(levels))


def _hgrn_kernel(*refs, reverse, final, n_levels, n_chunks):
    if final:
        hq_ref, hf_ref, hi_ref, lb_ref, seg_ref, mask_ref, oprev_ref, hg_ref, gn_ref, o_ref, st_ref = refs
    else:
        hq_ref, hf_ref, hi_ref, lb_ref, seg_ref, mask_ref, o_ref, st_ref = refs
    c = HGRN_CHUNK

    @pl.when(pl.program_id(1) == 0)
    def _():
        st_ref[...] = jnp.zeros_like(st_ref)

    lb = lb_ref[...]
    seg = seg_ref[...]
    total_row = 0 if reverse else c - 1

    def prep(j):
        rows = slice(j * c, (j + 1) * c)
        f = lb + (1.0 - lb) * _sigmoid(hf_ref[rows, :].astype(F32))
        log2f = jnp.log(f) * LOG2_E
        g_hi = log2f.astype(BF16)
        g_lo = (log2f - g_hi.astype(F32)).astype(BF16)
        e_all = jnp.exp2(_dot(seg, jnp.concatenate([g_hi, g_lo], axis=0)))
        decay = e_all[n_levels * c + total_row:n_levels * c + total_row + 1, :]
        return _silu(hq_ref[rows, :].astype(F32)).astype(BF16), (1.0 - f).astype(BF16), e_all.astype(BF16), decay

    def intra(prepped):
        qq, kk, e_all, _ = prepped
        mats = []
        for hd in range(HGRN_HEADS):
            sl = slice(hd * HGRN_D, (hd + 1) * HGRN_D)
            qh, kh = qq[:, sl], kk[:, sl]
            a = mask_ref[n_levels] * _dot_nt(qh, kh)
            for l in range(n_levels):
                el = e_all[l * c:(l + 1) * c, sl]
                a = a + mask_ref[l] * _dot_nt(qh * el, kh * el)
            mats.append(a.astype(BF16))
        return mats

    def outputs(j, prepped, mats):
        qq, kk, e_all, decay = prepped
        rows = slice(j * c, (j + 1) * c)
        if final:
            gate = _silu(hg_ref[rows, :].astype(F32)) * gn_ref[...]
        for hd in range(HGRN_HEADS):
            sl = slice(hd * HGRN_D, (hd + 1) * HGRN_D)
            qh, kh, vh = qq[:, sl], kk[:, sl], hi_ref[rows, sl]
            e_pre = e_all[n_levels * c:(n_levels + 1) * c, sl]
            e_suf = e_all[(n_levels + 1) * c:(n_levels + 2) * c, sl]
            st = st_ref[hd]
            o = _dot(mats[hd], vh) + _dot_nt(qh * e_pre, st.astype(BF16))
            v_t = vh.astype(F32).T.astype(BF16)
            st_ref[hd] = decay[:, sl] * st + _dot(v_t, kh * e_suf)
            if final:
                o = o + oprev_ref[rows, sl]
                o = o * lax.rsqrt(jnp.mean(o * o, axis=-1, keepdims=True) + NORM_EPS)
                o_ref[rows, sl] = (o * gate[:, sl]).astype(o_ref.dtype)
            else:
                o_ref[rows, sl] = o

    order = list(range(n_chunks))[::-1] if reverse else list(range(n_chunks))
    prepped, mats = {}, {}
    for step in range(n_chunks + 2):
        if step < n_chunks:
            prepped[step] = prep(order[step])
        if 0 <= step - 1 < n_chunks:
            mats[step - 1] = intra(prepped[step - 1])
        if 0 <= step - 2 < n_chunks:
            outputs(order[step - 2], prepped.pop(step - 2), mats.pop(step - 2))


def _hgrn_pass(u, lb, batch, n, *, reverse, o_prev=None, gn=None):
    t = u.shape[0]
    n_chunks = _tile(n // HGRN_CHUNK, HGRN_CHUNKS_PER_STEP)
    c = n_chunks * HGRN_CHUNK
    nc = n // c
    final = o_prev is not None
    seg, mask, n_levels = _hgrn_tables(reverse)

    def rows(b, ci):
        return b * nc + (nc - 1 - ci if reverse else ci)

    def col_spec(off):
        return pl.BlockSpec((c, HGRN_W), lambda b, ci: (rows(b, ci), off // HGRN_W))

    const2 = lambda b, ci: (0, 0)
    in_specs = [
        col_spec(OFF_HQ),
        col_spec(OFF_FB if reverse else OFF_FF),
        col_spec(OFF_HI),
        pl.BlockSpec((1, HGRN_W), const2),
        pl.BlockSpec(seg.shape, const2),
        pl.BlockSpec(mask.shape, lambda b, ci: (0, 0, 0)),
    ]
    args = [u, u, u, lb.reshape(1, HGRN_W), seg, mask]
    if final:
        in_specs += [
            pl.BlockSpec((c, HGRN_W), lambda b, ci: (rows(b, ci), 0)),
            col_spec(OFF_HG),
            pl.BlockSpec((1, HGRN_W), const2),
        ]
        args += [o_prev, u, gn.reshape(1, HGRN_W)]
    return pl.pallas_call(
        functools.partial(_hgrn_kernel, reverse=reverse, final=final, n_levels=n_levels, n_chunks=n_chunks),
        grid=(batch, nc),
        in_specs=in_specs,
        out_specs=pl.BlockSpec((c, HGRN_W), lambda b, ci: (rows(b, ci), 0)),
        out_shape=jax.ShapeDtypeStruct((t, HGRN_W), BF16 if final else F32),
        scratch_shapes=[pltpu.VMEM((HGRN_HEADS, HGRN_D, HGRN_D), F32)],
        compiler_params=_params("parallel", "arbitrary"),
        name="hgrn_bwd" if reverse else "hgrn_fwd",
    )(*args)


def _out_proj_kernel(a_ref, c_ref, h_ref, x_ref, w_ref, g_ref, o_ref):
    mix = _dot(a_ref[...], w_ref[0:ATTN_W, :])
    mix = mix + _dot(c_ref[...], w_ref[ATTN_W:ATTN_W + CONV_W, :])
    mix = mix + _dot(h_ref[...], w_ref[ATTN_W + CONV_W:, :])
    o_ref[...] = x_ref[...] + _rms(mix, g_ref[...])


def _out_proj(a, cv, hg, x, w, g):
    t, d = x.shape
    tm = _tile(t, 512)
    mixw = w.shape[0]
    return pl.pallas_call(
        _out_proj_kernel,
        grid=(t // tm,),
        in_specs=[
            pl.BlockSpec((tm, ATTN_W), lambda i: (i, 0)),
            pl.BlockSpec((tm, CONV_W), lambda i: (i, 0)),
            pl.BlockSpec((tm, HGRN_W), lambda i: (i, 0)),
            pl.BlockSpec((tm, d), lambda i: (i, 0)),
            pl.BlockSpec((mixw, d), lambda i: (0, 0)),
            pl.BlockSpec((1, d), lambda i: (0, 0)),
        ],
        out_specs=pl.BlockSpec((tm, d), lambda i: (i, 0)),
        out_shape=jax.ShapeDtypeStruct((t, d), F32),
        compiler_params=_params("parallel"),
        name="out_proj",
    )(a, cv, hg, x, w, g.reshape(1, d))


def _ffn_kernel(x_ref, gpre_ref, wg_ref, wu_ref, wo_ref, gpost_ref, o_ref, xn_ref):
    j = pl.program_id(1)

    @pl.when(j == 0)
    def _():
        xn_ref[...] = _rms(x_ref[...], gpre_ref[...]).astype(BF16)
        o_ref[...] = jnp.zeros_like(o_ref)

    xn = xn_ref[...]
    hidden = _silu(_dot(xn, wg_ref[...])) * _dot(xn, wu_ref[...])
    o_ref[...] += _dot(hidden.astype(BF16), wo_ref[...])

    @pl.when(j == pl.num_programs(1) - 1)
    def _():
        o_ref[...] = x_ref[...] + _rms(o_ref[...], gpost_ref[...])


def _ffn(x, gpre, w_in, w_out, gpost):
    t, d = x.shape
    f = w_out.shape[0]
    tm, th = _tile(t, 512), _tile(f, 512)
    nh = f // th
    return pl.pallas_call(
        _ffn_kernel,
        grid=(t // tm, nh),
        in_specs=[
            pl.BlockSpec((tm, d), lambda i, j: (i, 0)),
            pl.BlockSpec((1, d), lambda i, j: (0, 0)),
            pl.BlockSpec((d, th), lambda i, j: (0, j)),
            pl.BlockSpec((d, th), lambda i, j: (0, nh + j)),
            pl.BlockSpec((th, d), lambda i, j: (j, 0)),
            pl.BlockSpec((1, d), lambda i, j: (0, 0)),
        ],
        out_specs=pl.BlockSpec((tm, d), lambda i, j: (i, 0)),
        out_shape=jax.ShapeDtypeStruct((t, d), F32),
        scratch_shapes=[pltpu.VMEM((tm, d), BF16)],
        compiler_params=_params("parallel", "arbitrary"),
        name="ffn",
    )(x, gpre.reshape(1, d), w_in, w_in, w_out, gpost.reshape(1, d))


def _ple_kernel(x_ref, p_ref, wg_ref, wp_ref, g_ref, o_ref):
    x = x_ref[...]
    gate = _sigmoid(_dot(x.astype(BF16), wg_ref[...]))
    proj = _dot(p_ref[...].astype(BF16), wp_ref[...])
    o_ref[...] = x + _rms(proj * gate, g_ref[...])


def _ple(x, p, layer, w_gate, w_proj, g):
    t, d = x.shape
    pd = p.shape[-1]
    tm = _tile(t, 512)
    return pl.pallas_call(
        _ple_kernel,
        grid=(t // tm,),
        in_specs=[
            pl.BlockSpec((tm, d), lambda i: (i, 0)),
            pl.BlockSpec((None, tm, pd), lambda i: (layer, i, 0)),
            pl.BlockSpec((d, d), lambda i: (0, 0)),
            pl.BlockSpec((pd, d), lambda i: (0, 0)),
            pl.BlockSpec((1, d), lambda i: (0, 0)),
        ],
        out_specs=pl.BlockSpec((tm, d), lambda i: (i, 0)),
        out_shape=jax.ShapeDtypeStruct((t, d), F32),
        compiler_params=_params("parallel"),
        name="ple",
    )(x, p, w_gate, w_proj, g.reshape(1, d))


def _trunk(x, p, lb, layers):
    depth = p.shape[0]
    batch, n, d = x.shape
    t = batch * n
    x = x.reshape(t, d)
    p = p.reshape(depth, t, p.shape[-1])
    cos_l, sin_l = _rope_tables(n)
    for l, w in enumerate(layers):
        u = _in_proj(x, w["norm_mix_pre"], w["w_in"])
        q_r, k_r, v_t = _attn_prep(u, cos_l, sin_l, w["q_norm"], w["k_norm"], n)
        a_out = _flash(q_r, k_r, v_t, w["q_norm"], w["k_norm"], batch, n)
        c_out = _conv(u, w["conv_dw"], w["conv_dw_b"], w["conv_ln_g"], w["conv_ln_b"], w["conv_pw"], n)
        o_fwd = _hgrn_pass(u, lb[l, 0], batch, n, reverse=False)
        h_out = _hgrn_pass(u, lb[l, 1], batch, n, reverse=True, o_prev=o_fwd, gn=w["hgrn_gn"])
        x = _out_proj(a_out, c_out, h_out, x, w["w_out"], w["norm_mix_post"])
        x = _ffn(x, w["norm_ffn_pre"], w["w_ffn_in"], w["w_ffn_out"], w["norm_ffn_post"])
        x = _ple(x, p, l, w["w_ple_gate"], w["w_ple_proj"], w["ple_norm"])
    return x.reshape(batch, n, d)


def kernel(x_prompt, x_sample, p_prompt, p_sample, norm_mix_pre, norm_mix_post, w_in, q_norm, k_norm, conv_dw, conv_dw_b, conv_ln_g, conv_ln_b, conv_pw, hgrn_lb, hgrn_gn, w_out, norm_ffn_pre, norm_ffn_post, w_ffn_in, w_ffn_out, w_ple_gate, w_ple_proj, ple_norm):
    sm = jax.nn.softmax(hgrn_lb.astype(F32), axis=0)
    lb = jnp.cumsum(sm, axis=0) - sm[0]
    vectors = dict(
        norm_mix_pre=norm_mix_pre, norm_mix_post=norm_mix_post, q_norm=q_norm, k_norm=k_norm,
        conv_dw=conv_dw, conv_dw_b=conv_dw_b, conv_ln_g=conv_ln_g, conv_ln_b=conv_ln_b,
        hgrn_gn=hgrn_gn, norm_ffn_pre=norm_ffn_pre, norm_ffn_post=norm_ffn_post, ple_norm=ple_norm,
    )
    matrices = dict(
        w_in=w_in, conv_pw=conv_pw, w_out=w_out, w_ffn_in=w_ffn_in, w_ffn_out=w_ffn_out,
        w_ple_gate=w_ple_gate, w_ple_proj=w_ple_proj,
    )
    layers = [
        {**{k: v[l] for k, v in vectors.items()}, **{k: v[l].astype(BF16) for k, v in matrices.items()}}
        for l in range(p_prompt.shape[0])
    ]
    y_prompt = _trunk(x_prompt, p_prompt, lb, layers)
    y_sample = _trunk(x_sample, p_sample, lb, layers)
    return (y_prompt, y_sample)
```

```python
import functools

import numpy as np
import jax
import jax.numpy as jnp
from jax import lax
from jax.experimental import pallas as pl
from jax.experimental.pallas import tpu as pltpu

F32 = jnp.float32
BF16 = jnp.bfloat16

NORM_EPS = 1e-6
ROPE_THETA = 10000.0
LOG2_E = 1.4426950408889634
GRID_W = 64
HEAD_DIM = 128
N_Q_HEADS = 8
N_KV_HEADS = 2
Q_GROUP = N_Q_HEADS // N_KV_HEADS
ATTN_W = N_Q_HEADS * HEAD_DIM
KV_W = N_KV_HEADS * HEAD_DIM
CONV_W = 512
CONV_K = 31
CONV_HALO = 16
SUBLANES = 8
HGRN_W = 512
HGRN_D = 128
HGRN_HEADS = HGRN_W // HGRN_D
HGRN_CHUNK = 128
HGRN_CHUNKS_PER_STEP = 8
FLASH_ONES_ROWS = 16
FLASH_LOOKAHEAD = 3
FLASH_TQ = 256
FLASH_TK = 8192
FLASH_SUB = 512
FLASH_FIXED_SHIFT_MAX_RANGE = 96.0

OFF_Q = 0
OFF_K = OFF_Q + ATTN_W
OFF_V = OFF_K + KV_W
OFF_CA = OFF_V + KV_W
OFF_CG = OFF_CA + CONV_W
OFF_HQ = OFF_CG + CONV_W
OFF_FF = OFF_HQ + HGRN_W
OFF_FB = OFF_FF + HGRN_W
OFF_HI = OFF_FB + HGRN_W
OFF_HG = OFF_HI + HGRN_W
IN_COLS = OFF_HG + HGRN_W

V7X_VMEM_LIMIT_BYTES = 56 * 1024 * 1024


def _params(*semantics):
    return pltpu.CompilerParams(dimension_semantics=semantics, vmem_limit_bytes=V7X_VMEM_LIMIT_BYTES)


def _tile(dim, pref):
    t = min(dim, pref)
    assert dim % t == 0, (dim, pref)
    return t


def _rms(x, g):
    return x * lax.rsqrt(jnp.mean(x * x, axis=-1, keepdims=True) + NORM_EPS) * g


def _sigmoid(x):
    return 1.0 / (1.0 + jnp.exp(-x))


def _silu(x):
    return x * _sigmoid(x)


def _dot(a, b):
    return jnp.dot(a, b, preferred_element_type=F32)


def _dot_nt(a, b):
    return lax.dot_general(a, b, (((1,), (1,)), ((), ())), preferred_element_type=F32)


def _in_proj_kernel(x_ref, g_ref, w_ref, o_ref, xn_ref):
    @pl.when(pl.program_id(1) == 0)
    def _():
        xn_ref[...] = _rms(x_ref[...], g_ref[...]).astype(BF16)

    o_ref[...] = _dot(xn_ref[...], w_ref[...]).astype(o_ref.dtype)


def _in_proj(x, g, w, layer):
    t, d = x.shape
    n = w.shape[-1]
    tm, tn = _tile(t, 1024), _tile(n, 1024)
    return pl.pallas_call(
        _in_proj_kernel,
        grid=(t // tm, n // tn),
        in_specs=[
            pl.BlockSpec((tm, d), lambda i, j: (i, 0)),
            pl.BlockSpec((1, d), lambda i, j: (0, 0)),
            pl.BlockSpec((None, d, tn), lambda i, j: (layer, 0, j)),
        ],
        out_specs=pl.BlockSpec((tm, tn), lambda i, j: (i, j)),
        out_shape=jax.ShapeDtypeStruct((t, n), BF16),
        scratch_shapes=[pltpu.VMEM((tm, d), BF16)],
        compiler_params=_params("parallel", "arbitrary"),
        name="in_proj",
    )(x, g.reshape(1, d), w)


def _rope_tables(n):
    rows = n // GRID_W
    row = jnp.repeat(jnp.arange(rows, dtype=F32), GRID_W)
    col = jnp.tile(jnp.arange(GRID_W, dtype=F32), rows)
    axis_dims = HEAD_DIM // 2
    inv = ROPE_THETA ** (-jnp.arange(0, axis_dims, 2, dtype=F32) / axis_dims)
    ang = jnp.concatenate([row[:, None] * inv, col[:, None] * inv], axis=-1)
    cos, sin = jnp.cos(ang), jnp.sin(ang)
    cos_l = jnp.repeat(cos, 2, axis=-1)
    sin_l = jnp.stack([-sin, sin], axis=-1).reshape(n, HEAD_DIM)
    return cos_l, sin_l


def _attn_prep_kernel(q_ref, k_ref, v_ref, cos_ref, sin_ref, qg_ref, kg_ref, qo_ref, ko_ref, vt_ref):
    cos = cos_ref[...]
    sin = sin_ref[...]
    lane = lax.broadcasted_iota(jnp.int32, cos.shape, 1)
    even = (lane & 1) == 0

    def norm_rope(x, g):
        y = _rms(x.astype(F32), g)
        partner = jnp.where(even, pltpu.roll(y, HEAD_DIM - 1, 1), pltpu.roll(y, 1, 1))
        return y * cos + partner * sin

    scale = HEAD_DIM ** -0.5 * LOG2_E
    for h in range(N_Q_HEADS):
        sl = slice(h * HEAD_DIM, (h + 1) * HEAD_DIM)
        qo_ref[:, sl] = (norm_rope(q_ref[:, sl], qg_ref[...]) * scale).astype(qo_ref.dtype)
    for h in range(N_KV_HEADS):
        sl = slice(h * HEAD_DIM, (h + 1) * HEAD_DIM)
        ko_ref[:, sl] = norm_rope(k_ref[:, sl], kg_ref[...]).astype(ko_ref.dtype)
    vt_ref[...] = v_ref[...].astype(F32).T.astype(vt_ref.dtype)


def _attn_prep(u, cos_l, sin_l, q_gain, k_gain, n):
    t = u.shape[0]
    tm = _tile(n, 512)
    per_seq = n // tm
    return pl.pallas_call(
        _attn_prep_kernel,
        grid=(t // tm,),
        in_specs=[
            pl.BlockSpec((tm, ATTN_W), lambda i: (i, OFF_Q // ATTN_W)),
            pl.BlockSpec((tm, KV_W), lambda i: (i, OFF_K // KV_W)),
            pl.BlockSpec((tm, KV_W), lambda i: (i, OFF_V // KV_W)),
            pl.BlockSpec((tm, HEAD_DIM), lambda i: (i % per_seq, 0)),
            pl.BlockSpec((tm, HEAD_DIM), lambda i: (i % per_seq, 0)),
            pl.BlockSpec((1, HEAD_DIM), lambda i: (0, 0)),
            pl.BlockSpec((1, HEAD_DIM), lambda i: (0, 0)),
        ],
        out_specs=[
            pl.BlockSpec((tm, ATTN_W), lambda i: (i, 0)),
            pl.BlockSpec((tm, KV_W), lambda i: (i, 0)),
            pl.BlockSpec((KV_W, tm), lambda i: (0, i)),
        ],
        out_shape=[
            jax.ShapeDtypeStruct((t, ATTN_W), BF16),
            jax.ShapeDtypeStruct((t, KV_W), BF16),
            jax.ShapeDtypeStruct((KV_W, t), BF16),
        ],
        compiler_params=_params("parallel"),
        name="attn_prep",
    )(u, u, u, cos_l, sin_l, q_gain.reshape(1, HEAD_DIM), k_gain.reshape(1, HEAD_DIM))


def _flash_kernel(shift_ref, q_ref, k_ref, vt_ref, o_ref, m_ref, acc_ref, *, tk, sub, fixed_shift):
    ki = pl.program_id(3)

    @pl.when(ki == 0)
    def _():
        m_ref[...] = jnp.full_like(m_ref, -jnp.inf)
        acc_ref[...] = jnp.zeros_like(acc_ref)

    stages = [(s0, g) for s0 in range(0, tk, sub) for g in range(Q_GROUP)]

    def scores(stage):
        s0, g = stage
        return _dot_nt(k_ref[s0:s0 + sub, :], q_ref[:, g * HEAD_DIM:(g + 1) * HEAD_DIM])

    ones = jnp.ones((FLASH_ONES_ROWS, sub), BF16)
    shift = shift_ref[0, 0]
    partial = [None] * Q_GROUP

    pending = [scores(st) for st in stages[:FLASH_LOOKAHEAD]]
    for i, (s0, g) in enumerate(stages):
        s = pending.pop(0)
        if i + FLASH_LOOKAHEAD < len(stages):
            pending.append(scores(stages[i + FLASH_LOOKAHEAD]))
        vt1 = jnp.concatenate([vt_ref[:, s0:s0 + sub], ones], axis=0)
        if fixed_shift:
            pv = _dot(vt1, jnp.exp2(s - shift).astype(BF16))
            partial[g] = pv if partial[g] is None else partial[g] + pv
        else:
            m_prev = m_ref[g]
            m_new = jnp.maximum(m_prev, jnp.max(s, axis=0, keepdims=True))
            alpha = jnp.exp2(m_prev - m_new)
            p = jnp.exp2((s - m_new).astype(BF16))
            acc_ref[g] = alpha * acc_ref[g] + _dot(vt1, p)
            m_ref[g] = m_new
    if fixed_shift:
        for g in range(Q_GROUP):
            acc_ref[g] += partial[g]

    @pl.when(ki == pl.num_programs(3) - 1)
    def _():
        for g in range(Q_GROUP):
            out = acc_ref[g, 0:HEAD_DIM, :] / acc_ref[g, HEAD_DIM:HEAD_DIM + 1, :]
            o_ref[:, g * HEAD_DIM:(g + 1) * HEAD_DIM] = out.T.astype(o_ref.dtype)


def _flash(q, k, vt, q_gain, k_gain, batch, n):
    t = q.shape[0]
    tq, tk = _tile(n, FLASH_TQ), _tile(n, FLASH_TK)
    sub = _tile(tk, FLASH_SUB)
    nq, nk = n // tq, n // tk
    gw = Q_GROUP * HEAD_DIM
    bound = (HEAD_DIM ** 0.5 * LOG2_E) * jnp.max(jnp.abs(q_gain)) * jnp.max(jnp.abs(k_gain))
    shift = bound.astype(F32).reshape(1, 1)

    def call(fixed_shift):
        return pl.pallas_call(
            functools.partial(_flash_kernel, tk=tk, sub=sub, fixed_shift=fixed_shift),
            grid=(batch, N_KV_HEADS, nq, nk),
            in_specs=[
                pl.BlockSpec(memory_space=pltpu.SMEM),
                pl.BlockSpec((tq, gw), lambda b, h, qi, ki: (b * nq + qi, h)),
                pl.BlockSpec((tk, HEAD_DIM), lambda b, h, qi, ki: (b * nk + ki, h)),
                pl.BlockSpec((HEAD_DIM, tk), lambda b, h, qi, ki: (h, b * nk + ki)),
            ],
            out_specs=pl.BlockSpec((tq, gw), lambda b, h, qi, ki: (b * nq + qi, h)),
            out_shape=jax.ShapeDtypeStruct((t, ATTN_W), BF16),
            scratch_shapes=[
                pltpu.VMEM((Q_GROUP, 1, tq), F32),
                pltpu.VMEM((Q_GROUP, HEAD_DIM + FLASH_ONES_ROWS, tq), F32),
            ],
            compiler_params=_params("parallel", "parallel", "parallel", "arbitrary"),
            name="flash_attn_fixed" if fixed_shift else "flash_attn",
        )(shift, q, k, vt)

    return lax.cond(2.0 * bound <= FLASH_FIXED_SHIFT_MAX_RANGE, lambda: call(True), lambda: call(False))


def _conv_kernel(a_ref, g_ref, ap_ref, gp_ref, an_ref, gn_ref, dw_ref, dwb_ref, lng_ref, lnb_ref, pw_ref,
                 o_ref, h_ref, sh_ref, *, tm, per_seq):
    i = pl.program_id(0)
    pos = i % per_seq

    def glu(a, g):
        return a.astype(F32) * _sigmoid(g.astype(F32))

    prev = jnp.where(pos == 0, 0.0, glu(ap_ref[...], gp_ref[...]))
    nxt = jnp.where(pos == per_seq - 1, 0.0, glu(an_ref[...], gn_ref[...]))
    h_ref[0:CONV_HALO, :] = prev
    h_ref[CONV_HALO:CONV_HALO + tm, :] = glu(a_ref[...], g_ref[...])
    h_ref[CONV_HALO + tm:, :] = nxt

    base = CONV_HALO - CONV_K // 2
    span = tm + 2 * CONV_HALO - SUBLANES
    for b in range(SUBLANES):
        sh_ref[b] = h_ref[b:b + span, :]
    acc = jnp.zeros((tm, CONV_W), F32) + dwb_ref[...]
    for j in range(CONV_K):
        a, b = divmod(base + j, SUBLANES)
        acc = acc + dw_ref[j:j + 1, :] * sh_ref[b, a * SUBLANES:a * SUBLANES + tm, :]
    mu = jnp.mean(acc, axis=-1, keepdims=True)
    cen = acc - mu
    var = jnp.mean(cen * cen, axis=-1, keepdims=True)
    y = cen * lax.rsqrt(var + NORM_EPS) * lng_ref[...] + lnb_ref[...]
    o_ref[...] = _dot(_silu(y).astype(BF16), pw_ref[...]).astype(o_ref.dtype)


def _conv(u, dw, dwb, lng, lnb, pw, layer, n):
    t = u.shape[0]
    tm = _tile(n, 512)
    per_seq = n // tm
    hb = tm // CONV_HALO
    last = t // CONV_HALO - 1
    ca, cg = OFF_CA // CONV_W, OFF_CG // CONV_W
    vec = lambda v: v.reshape(1, CONV_W)
    row = pl.BlockSpec((1, CONV_W), lambda i: (0, 0))
    return pl.pallas_call(
        functools.partial(_conv_kernel, tm=tm, per_seq=per_seq),
        grid=(t // tm,),
        in_specs=[
            pl.BlockSpec((tm, CONV_W), lambda i: (i, ca)),
            pl.BlockSpec((tm, CONV_W), lambda i: (i, cg)),
            pl.BlockSpec((CONV_HALO, CONV_W), lambda i: (jnp.maximum(i * hb - 1, 0), ca)),
            pl.BlockSpec((CONV_HALO, CONV_W), lambda i: (jnp.maximum(i * hb - 1, 0), cg)),
            pl.BlockSpec((CONV_HALO, CONV_W), lambda i: (jnp.minimum((i + 1) * hb, last), ca)),
            pl.BlockSpec((CONV_HALO, CONV_W), lambda i: (jnp.minimum((i + 1) * hb, last), cg)),
            pl.BlockSpec((CONV_K, CONV_W), lambda i: (0, 0)),
            row, row, row,
            pl.BlockSpec((None, CONV_W, CONV_W), lambda i: (layer, 0, 0)),
        ],
        out_specs=pl.BlockSpec((tm, CONV_W), lambda i: (i, 0)),
        out_shape=jax.ShapeDtypeStruct((t, CONV_W), BF16),
        scratch_shapes=[
            pltpu.VMEM((tm + 2 * CONV_HALO, CONV_W), F32),
            pltpu.VMEM((SUBLANES, tm + 2 * CONV_HALO - SUBLANES, CONV_W), F32),
        ],
        compiler_params=_params("parallel"),
        name="conv_mixer",
    )(u, u, u, u, u, u, dw, vec(dwb), vec(lng), vec(lnb), pw)


def _hgrn_tables(reverse):
    c = HGRN_CHUNK
    levels = []
    h = c // 2
    while h >= 1:
        levels.append(h)
        h //= 2
    r = np.arange(c)
    col = r[None, :]
    seg = np.zeros((len(levels) + 2, c, c), np.float32)
    mask = np.zeros((len(levels) + 1, c, c), np.float32)
    for li, h in enumerate(levels):
        blk = r // (2 * h)
        upper = (r % (2 * h)) >= h
        b = (blk * 2 * h + h - 1)[:, None]
        rr = r[:, None]
        seg[li] = np.where(upper[:, None], (col > b) & (col <= rr), (col > rr) & (col <= b))
        mask[li] = upper[:, None] & (~upper)[None, :] & (blk[:, None] == blk[None, :])
    seg[-2] = col <= r[:, None]
    seg[-1] = col > r[:, None]
    mask[-1] = np.eye(c)
    if reverse:
        seg = seg[:, ::-1, ::-1]
        mask = mask[:, ::-1, ::-1]
    seg = seg.reshape(-1, c)
    seg2 = np.concatenate([seg, seg], axis=1)
    return (jnp.asarray(seg2, dtype=BF16), jnp.asarray(mask, dtype=F32), len(levels))


def _hgrn_kernel(*refs, reverse, final, n_levels, n_chunks):
    if final:
        hq_ref, hf_ref, hi_ref, lb_ref, seg_ref, mask_ref, oprev_ref, hg_ref, gn_ref, o_ref, st_ref = refs
    else:
        hq_ref, hf_ref, hi_ref, lb_ref, seg_ref, mask_ref, o_ref, st_ref = refs
    c = HGRN_CHUNK

    @pl.when(pl.program_id(1) == 0)
    def _():
        st_ref[...] = jnp.zeros_like(st_ref)

    lb = lb_ref[...]
    seg = seg_ref[...]
    total_row = 0 if reverse else c - 1

    def prep(j):
        rows = slice(j * c, (j + 1) * c)
        f = lb + (1.0 - lb) * _sigmoid(hf_ref[rows, :].astype(F32))
        log2f = jnp.log(f) * LOG2_E
        g_hi = log2f.astype(BF16)
        g_lo = (log2f - g_hi.astype(F32)).astype(BF16)
        e_all = jnp.exp2(_dot(seg, jnp.concatenate([g_hi, g_lo], axis=0)))
        decay = e_all[n_levels * c + total_row:n_levels * c + total_row + 1, :]
        return _silu(hq_ref[rows, :].astype(F32)).astype(BF16), (1.0 - f).astype(BF16), e_all.astype(BF16), decay

    def intra(prepped):
        qq, kk, e_all, _ = prepped
        mats = []
        for hd in range(HGRN_HEADS):
            sl = slice(hd * HGRN_D, (hd + 1) * HGRN_D)
            qh, kh = qq[:, sl], kk[:, sl]
            a = mask_ref[n_levels] * _dot_nt(qh, kh)
            for l in range(n_levels):
                el = e_all[l * c:(l + 1) * c, sl]
                a = a + mask_ref[l] * _dot_nt(qh * el, kh * el)
            mats.append(a.astype(BF16))
        return mats

    def outputs(j, prepped, mats):
        qq, kk, e_all, decay = prepped
        rows = slice(j * c, (j + 1) * c)
        if final:
            gate = _silu(hg_ref[rows, :].astype(F32)) * gn_ref[...]
        for hd in range(HGRN_HEADS):
            sl = slice(hd * HGRN_D, (hd + 1) * HGRN_D)
            qh, kh, vh = qq[:, sl], kk[:, sl], hi_ref[rows, sl]
            e_pre = e_all[n_levels * c:(n_levels + 1) * c, sl]
            e_suf = e_all[(n_levels + 1) * c:(n_levels + 2) * c, sl]
            st = st_ref[hd]
            o = _dot(mats[hd], vh) + _dot_nt(qh * e_pre, st.astype(BF16))
            v_t = vh.astype(F32).T.astype(BF16)
            st_ref[hd] = decay[:, sl] * st + _dot(v_t, kh * e_suf)
            if final:
                o = o + oprev_ref[rows, sl]
                o = o * lax.rsqrt(jnp.mean(o * o, axis=-1, keepdims=True) + NORM_EPS)
                o_ref[rows, sl] = (o * gate[:, sl]).astype(o_ref.dtype)
            else:
                o_ref[rows, sl] = o

    order = list(range(n_chunks))[::-1] if reverse else list(range(n_chunks))
    prepped, mats = {}, {}
    for step in range(n_chunks + 2):
        if step < n_chunks:
            prepped[step] = prep(order[step])
        if 0 <= step - 1 < n_chunks:
            mats[step - 1] = intra(prepped[step - 1])
        if 0 <= step - 2 < n_chunks:
            outputs(order[step - 2], prepped.pop(step - 2), mats.pop(step - 2))


def _hgrn_pass(u, lb, batch, n, *, reverse, o_prev=None, gn=None):
    t = u.shape[0]
    n_chunks = _tile(n // HGRN_CHUNK, HGRN_CHUNKS_PER_STEP)
    c = n_chunks * HGRN_CHUNK
    nc = n // c
    final = o_prev is not None
    seg, mask, n_levels = _hgrn_tables(reverse)

    def rows(b, ci):
        return b * nc + (nc - 1 - ci if reverse else ci)

    def col_spec(off):
        return pl.BlockSpec((c, HGRN_W), lambda b, ci: (rows(b, ci), off // HGRN_W))

    const2 = lambda b, ci: (0, 0)
    in_specs = [
        col_spec(OFF_HQ),
        col_spec(OFF_FB if reverse else OFF_FF),
        col_spec(OFF_HI),
        pl.BlockSpec((1, HGRN_W), const2),
        pl.BlockSpec(seg.shape, const2),
        pl.BlockSpec(mask.shape, lambda b, ci: (0, 0, 0)),
    ]
    args = [u, u, u, lb.reshape(1, HGRN_W), seg, mask]
    if final:
        in_specs += [
            pl.BlockSpec((c, HGRN_W), lambda b, ci: (rows(b, ci), 0)),
            col_spec(OFF_HG),
            pl.BlockSpec((1, HGRN_W), const2),
        ]
        args += [o_prev, u, gn.reshape(1, HGRN_W)]
    return pl.pallas_call(
        functools.partial(_hgrn_kernel, reverse=reverse, final=final, n_levels=n_levels, n_chunks=n_chunks),
        grid=(batch, nc),
        in_specs=in_specs,
        out_specs=pl.BlockSpec((c, HGRN_W), lambda b, ci: (rows(b, ci), 0)),
        out_shape=jax.ShapeDtypeStruct((t, HGRN_W), BF16 if final else F32),
        scratch_shapes=[pltpu.VMEM((HGRN_HEADS, HGRN_D, HGRN_D), F32)],
        compiler_params=_params("parallel", "arbitrary"),
        name="hgrn_bwd" if reverse else "hgrn_fwd",
    )(*args)


def _out_proj_kernel(a_ref, c_ref, h_ref, x_ref, w_ref, g_ref, gnext_ref, o_ref, xn_ref):
    mix = _dot(a_ref[...], w_ref[0:ATTN_W, :])
    mix = mix + _dot(c_ref[...], w_ref[ATTN_W:ATTN_W + CONV_W, :])
    mix = mix + _dot(h_ref[...], w_ref[ATTN_W + CONV_W:, :])
    x = x_ref[...] + _rms(mix, g_ref[...])
    o_ref[...] = x
    xn_ref[...] = _rms(x, gnext_ref[...]).astype(xn_ref.dtype)


def _out_proj(a, cv, hg, x, w, layer, g, g_next):
    t, d = x.shape
    tm = _tile(t, 512)
    mixw = w.shape[1]
    return pl.pallas_call(
        _out_proj_kernel,
        grid=(t // tm,),
        in_specs=[
            pl.BlockSpec((tm, ATTN_W), lambda i: (i, 0)),
            pl.BlockSpec((tm, CONV_W), lambda i: (i, 0)),
            pl.BlockSpec((tm, HGRN_W), lambda i: (i, 0)),
            pl.BlockSpec((tm, d), lambda i: (i, 0)),
            pl.BlockSpec((None, mixw, d), lambda i: (layer, 0, 0)),
            pl.BlockSpec((1, d), lambda i: (0, 0)),
            pl.BlockSpec((1, d), lambda i: (0, 0)),
        ],
        out_specs=[pl.BlockSpec((tm, d), lambda i: (i, 0)), pl.BlockSpec((tm, d), lambda i: (i, 0))],
        out_shape=[jax.ShapeDtypeStruct((t, d), F32), jax.ShapeDtypeStruct((t, d), BF16)],
        compiler_params=_params("parallel"),
        name="out_proj",
    )(a, cv, hg, x, w, g.reshape(1, d), g_next.reshape(1, d))


def _ffn_kernel(xn_ref, wg_ref, wu_ref, wo_ref, gpost_ref, o_ref, acc_ref):
    j = pl.program_id(1)

    @pl.when(j == 0)
    def _():
        acc_ref[...] = jnp.zeros_like(acc_ref)

    xn = xn_ref[...]
    hidden = _silu(_dot(xn, wg_ref[...])) * _dot(xn, wu_ref[...])
    acc_ref[...] += _dot(hidden.astype(BF16), wo_ref[...])

    @pl.when(j == pl.num_programs(1) - 1)
    def _():
        o_ref[...] = _rms(acc_ref[...], gpost_ref[...]).astype(o_ref.dtype)


def _ffn(xn, w_in, w_out, layer, gpost):
    t, d = xn.shape
    f = w_out.shape[1]
    tm, th = _tile(t, 1024), _tile(f, 512)
    nh = f // th
    return pl.pallas_call(
        _ffn_kernel,
        grid=(t // tm, nh),
        in_specs=[
            pl.BlockSpec((tm, d), lambda i, j: (i, 0)),
            pl.BlockSpec((None, d, th), lambda i, j: (layer, 0, j)),
            pl.BlockSpec((None, d, th), lambda i, j: (layer, 0, nh + j)),
            pl.BlockSpec((None, th, d), lambda i, j: (layer, j, 0)),
            pl.BlockSpec((1, d), lambda i, j: (0, 0)),
        ],
        out_specs=pl.BlockSpec((tm, d), lambda i, j: (i, 0)),
        out_shape=jax.ShapeDtypeStruct((t, d), BF16),
        scratch_shapes=[pltpu.VMEM((tm, d), F32)],
        compiler_params=_params("parallel", "arbitrary"),
        name="ffn",
    )(xn, w_in, w_in, w_out, gpost.reshape(1, d))


def _ple_kernel(x_ref, y_ref, p_ref, wg_ref, wp_ref, g_ref, o_ref):
    x = x_ref[...] + y_ref[...].astype(F32)
    gate = _sigmoid(_dot(x.astype(BF16), wg_ref[...]))
    proj = _dot(p_ref[...].astype(BF16), wp_ref[...])
    o_ref[...] = x + _rms(proj * gate, g_ref[...])


def _ple(x, y, p, layer, w_gate, w_proj, g):
    t, d = x.shape
    pd = p.shape[-1]
    tm = _tile(t, 512)
    return pl.pallas_call(
        _ple_kernel,
        grid=(t // tm,),
        in_specs=[
            pl.BlockSpec((tm, d), lambda i: (i, 0)),
            pl.BlockSpec((tm, d), lambda i: (i, 0)),
            pl.BlockSpec((None, tm, pd), lambda i: (layer, i, 0)),
            pl.BlockSpec((None, d, d), lambda i: (layer, 0, 0)),
            pl.BlockSpec((None, pd, d), lambda i: (layer, 0, 0)),
            pl.BlockSpec((1, d), lambda i: (0, 0)),
        ],
        out_specs=pl.BlockSpec((tm, d), lambda i: (i, 0)),
        out_shape=jax.ShapeDtypeStruct((t, d), F32),
        compiler_params=_params("parallel"),
        name="ple",
    )(x, y, p, w_gate, w_proj, g.reshape(1, d))


def _trunk(x, p, lb, layers, mats):
    depth = p.shape[0]
    batch, n, d = x.shape
    t = batch * n
    x = x.reshape(t, d)
    p = p.reshape(depth, t, p.shape[-1])
    cos_l, sin_l = _rope_tables(n)
    for l, w in enumerate(layers):
        u = _in_proj(x, w["norm_mix_pre"], mats["w_in"], l)
        q_r, k_r, v_t = _attn_prep(u, cos_l, sin_l, w["q_norm"], w["k_norm"], n)
        a_out = _flash(q_r, k_r, v_t, w["q_norm"], w["k_norm"], batch, n)
        c_out = _conv(u, w["conv_dw"], w["conv_dw_b"], w["conv_ln_g"], w["conv_ln_b"], mats["conv_pw"], l, n)
        o_fwd = _hgrn_pass(u, lb[l, 0], batch, n, reverse=False)
        h_out = _hgrn_pass(u, lb[l, 1], batch, n, reverse=True, o_prev=o_fwd, gn=w["hgrn_gn"])
        x, xn = _out_proj(a_out, c_out, h_out, x, mats["w_out"], l, w["norm_mix_post"], w["norm_ffn_pre"])
        y = _ffn(xn, mats["w_ffn_in"], mats["w_ffn_out"], l, w["norm_ffn_post"])
        x = _ple(x, y, p, l, mats["w_ple_gate"], mats["w_ple_proj"], w["ple_norm"])
    return x.reshape(batch, n, d)


def kernel(x_prompt, x_sample, p_prompt, p_sample, norm_mix_pre, norm_mix_post, w_in, q_norm, k_norm, conv_dw, conv_dw_b, conv_ln_g, conv_ln_b, conv_pw, hgrn_lb, hgrn_gn, w_out, norm_ffn_pre, norm_ffn_post, w_ffn_in, w_ffn_out, w_ple_gate, w_ple_proj, ple_norm):
    sm = jax.nn.softmax(hgrn_lb.astype(F32), axis=0)
    lb = jnp.cumsum(sm, axis=0) - sm[0]
    vectors = dict(
        norm_mix_pre=norm_mix_pre, norm_mix_post=norm_mix_post, q_norm=q_norm, k_norm=k_norm,
        conv_dw=conv_dw, conv_dw_b=conv_dw_b, conv_ln_g=conv_ln_g, conv_ln_b=conv_ln_b,
        hgrn_gn=hgrn_gn, norm_ffn_pre=norm_ffn_pre, norm_ffn_post=norm_ffn_post, ple_norm=ple_norm,
    )
    matrices = dict(
        w_in=w_in, conv_pw=conv_pw, w_out=w_out, w_ffn_in=w_ffn_in, w_ffn_out=w_ffn_out,
        w_ple_gate=w_ple_gate, w_ple_proj=w_ple_proj,
    )
    mats = {k: v.astype(BF16) for k, v in matrices.items()}
    layers = [{k: v[l] for k, v in vectors.items()} for l in range(p_prompt.shape[0])]
    y_prompt = _trunk(x_prompt, p_prompt, lb, layers, mats)
    y_sample = _trunk(x_sample, p_sample, lb, layers, mats)
    return (y_prompt, y_sample)
```

```python
import functools

import numpy as np
import jax
import jax.numpy as jnp
from jax import lax
from jax.experimental import pallas as pl
from jax.experimental.pallas import tpu as pltpu

F32 = jnp.float32
BF16 = jnp.bfloat16

NORM_EPS = 1e-6
ROPE_THETA = 10000.0
LOG2_E = 1.4426950408889634
GRID_W = 64
HEAD_DIM = 128
N_Q_HEADS = 8
N_KV_HEADS = 2
Q_GROUP = N_Q_HEADS // N_KV_HEADS
ATTN_W = N_Q_HEADS * HEAD_DIM
KV_W = N_KV_HEADS * HEAD_DIM
CONV_W = 512
CONV_K = 31
CONV_HALO = 16
SUBLANES = 8
HGRN_W = 512
HGRN_D = 128
HGRN_HEADS = HGRN_W // HGRN_D
HGRN_CHUNK = 128
HGRN_CHUNKS_PER_STEP = 8
FLASH_ONES_ROWS = 16
FLASH_LOOKAHEAD = 3
FLASH_TQ = 256
FLASH_TK = 8192
FLASH_SUB = 512
FLASH_FIXED_SHIFT_MAX_RANGE = 96.0

OFF_Q = 0
OFF_K = OFF_Q + ATTN_W
OFF_V = OFF_K + KV_W
OFF_CA = OFF_V + KV_W
OFF_CG = OFF_CA + CONV_W
OFF_HQ = OFF_CG + CONV_W
OFF_FF = OFF_HQ + HGRN_W
OFF_FB = OFF_FF + HGRN_W
OFF_HI = OFF_FB + HGRN_W
OFF_HG = OFF_HI + HGRN_W
IN_COLS = OFF_HG + HGRN_W

V7X_VMEM_LIMIT_BYTES = 56 * 1024 * 1024


def _params(*semantics):
    return pltpu.CompilerParams(dimension_semantics=semantics, vmem_limit_bytes=V7X_VMEM_LIMIT_BYTES)


def _tile(dim, pref):
    t = min(dim, pref)
    assert dim % t == 0, (dim, pref)
    return t


def _rms(x, g):
    return x * lax.rsqrt(jnp.mean(x * x, axis=-1, keepdims=True) + NORM_EPS) * g


def _sigmoid(x):
    return 1.0 / (1.0 + jnp.exp(-x))


def _silu(x):
    return x * _sigmoid(x)


def _dot(a, b):
    return jnp.dot(a, b, preferred_element_type=F32)


def _dot_nt(a, b):
    return lax.dot_general(a, b, (((1,), (1,)), ((), ())), preferred_element_type=F32)


def _in_proj_kernel(x_ref, g_ref, w_ref, o_ref, xn_ref):
    @pl.when(pl.program_id(1) == 0)
    def _():
        xn_ref[...] = _rms(x_ref[...], g_ref[...]).astype(BF16)

    o_ref[...] = _dot(xn_ref[...], w_ref[...]).astype(o_ref.dtype)


def _in_proj(x, g, w, layer):
    t, d = x.shape
    n = w.shape[-1]
    tm, tn = _tile(t, 1024), _tile(n, 1024)
    return pl.pallas_call(
        _in_proj_kernel,
        grid=(t // tm, n // tn),
        in_specs=[
            pl.BlockSpec((tm, d), lambda i, j: (i, 0)),
            pl.BlockSpec((1, d), lambda i, j: (0, 0)),
            pl.BlockSpec((None, d, tn), lambda i, j: (layer, 0, j)),
        ],
        out_specs=pl.BlockSpec((tm, tn), lambda i, j: (i, j)),
        out_shape=jax.ShapeDtypeStruct((t, n), BF16),
        scratch_shapes=[pltpu.VMEM((tm, d), BF16)],
        compiler_params=_params("parallel", "arbitrary"),
        name="in_proj",
    )(x, g.reshape(1, d), w)


def _rope_tables(n):
    rows = n // GRID_W
    row = jnp.repeat(jnp.arange(rows, dtype=F32), GRID_W)
    col = jnp.tile(jnp.arange(GRID_W, dtype=F32), rows)
    axis_dims = HEAD_DIM // 2
    inv = ROPE_THETA ** (-jnp.arange(0, axis_dims, 2, dtype=F32) / axis_dims)
    ang = jnp.concatenate([row[:, None] * inv, col[:, None] * inv], axis=-1)
    cos, sin = jnp.cos(ang), jnp.sin(ang)
    cos_l = jnp.repeat(cos, 2, axis=-1)
    sin_l = jnp.stack([-sin, sin], axis=-1).reshape(n, HEAD_DIM)
    return cos_l, sin_l


def _attn_prep_kernel(q_ref, k_ref, v_ref, cos_ref, sin_ref, qg_ref, kg_ref, qk_ref, vt_ref):
    cos = cos_ref[...]
    sin = sin_ref[...]
    lane = lax.broadcasted_iota(jnp.int32, cos.shape, 1)
    even = (lane & 1) == 0

    def norm_rope(x, g):
        y = _rms(x.astype(F32), g)
        partner = jnp.where(even, pltpu.roll(y, HEAD_DIM - 1, 1), pltpu.roll(y, 1, 1))
        return y * cos + partner * sin

    scale = HEAD_DIM ** -0.5 * LOG2_E
    for h in range(N_Q_HEADS):
        sl = slice(h * HEAD_DIM, (h + 1) * HEAD_DIM)
        qk_ref[:, sl] = (norm_rope(q_ref[:, sl], qg_ref[...]) * scale).astype(qk_ref.dtype)
    for h in range(N_KV_HEADS):
        sl = slice(h * HEAD_DIM, (h + 1) * HEAD_DIM)
        so = slice(ATTN_W + h * HEAD_DIM, ATTN_W + (h + 1) * HEAD_DIM)
        qk_ref[:, so] = norm_rope(k_ref[:, sl], kg_ref[...]).astype(qk_ref.dtype)
    rows = HEAD_DIM + FLASH_ONES_ROWS
    for h in range(N_KV_HEADS):
        vt_ref[h * rows:h * rows + HEAD_DIM, :] = (
            v_ref[:, h * HEAD_DIM:(h + 1) * HEAD_DIM].astype(F32).T.astype(vt_ref.dtype))
        vt_ref[h * rows + HEAD_DIM:(h + 1) * rows, :] = jnp.ones((FLASH_ONES_ROWS, v_ref.shape[0]), vt_ref.dtype)


def _attn_prep(u, cos_l, sin_l, q_gain, k_gain, n):
    t = u.shape[0]
    tm = _tile(n, 512)
    per_seq = n // tm
    return pl.pallas_call(
        _attn_prep_kernel,
        grid=(t // tm,),
        in_specs=[
            pl.BlockSpec((tm, ATTN_W), lambda i: (i, OFF_Q // ATTN_W)),
            pl.BlockSpec((tm, KV_W), lambda i: (i, OFF_K // KV_W)),
            pl.BlockSpec((tm, KV_W), lambda i: (i, OFF_V // KV_W)),
            pl.BlockSpec((tm, HEAD_DIM), lambda i: (i % per_seq, 0)),
            pl.BlockSpec((tm, HEAD_DIM), lambda i: (i % per_seq, 0)),
            pl.BlockSpec((1, HEAD_DIM), lambda i: (0, 0)),
            pl.BlockSpec((1, HEAD_DIM), lambda i: (0, 0)),
        ],
        out_specs=[
            pl.BlockSpec((tm, ATTN_W + KV_W), lambda i: (i, 0)),
            pl.BlockSpec((N_KV_HEADS * (HEAD_DIM + FLASH_ONES_ROWS), tm), lambda i: (0, i)),
        ],
        out_shape=[
            jax.ShapeDtypeStruct((t, ATTN_W + KV_W), BF16),
            jax.ShapeDtypeStruct((N_KV_HEADS * (HEAD_DIM + FLASH_ONES_ROWS), t), BF16),
        ],
        compiler_params=_params("parallel"),
        name="attn_prep",
    )(u, u, u, cos_l, sin_l, q_gain.reshape(1, HEAD_DIM), k_gain.reshape(1, HEAD_DIM))


def _flash_kernel(shift_ref, q_ref, k_ref, vt_ref, o_ref, m_ref, acc_ref, *, tk, sub, fixed_shift):
    ki = pl.program_id(3)

    @pl.when(ki == 0)
    def _():
        m_ref[...] = jnp.full_like(m_ref, -jnp.inf)
        acc_ref[...] = jnp.zeros_like(acc_ref)

    stages = [(s0, g) for s0 in range(0, tk, sub) for g in range(Q_GROUP)]

    def scores(stage):
        s0, g = stage
        return _dot_nt(k_ref[s0:s0 + sub, :], q_ref[:, g * HEAD_DIM:(g + 1) * HEAD_DIM])

    shift = shift_ref[0, 0]
    partial = [None] * Q_GROUP

    pending = [scores(st) for st in stages[:FLASH_LOOKAHEAD]]
    for i, (s0, g) in enumerate(stages):
        s = pending.pop(0)
        if i + FLASH_LOOKAHEAD < len(stages):
            pending.append(scores(stages[i + FLASH_LOOKAHEAD]))
        vt1 = vt_ref[:, s0:s0 + sub]
        if fixed_shift:
            pv = _dot(vt1, jnp.exp2(s - shift).astype(BF16))
            partial[g] = pv if partial[g] is None else partial[g] + pv
        else:
            m_prev = m_ref[g]
            m_new = jnp.maximum(m_prev, jnp.max(s, axis=0, keepdims=True))
            alpha = jnp.exp2(m_prev - m_new)
            p = jnp.exp2((s - m_new).astype(BF16))
            acc_ref[g] = alpha * acc_ref[g] + _dot(vt1, p)
            m_ref[g] = m_new
    if fixed_shift:
        for g in range(Q_GROUP):
            acc_ref[g] += partial[g]

    @pl.when(ki == pl.num_programs(3) - 1)
    def _():
        for g in range(Q_GROUP):
            out = acc_ref[g, 0:HEAD_DIM, :] / acc_ref[g, HEAD_DIM:HEAD_DIM + 1, :]
            o_ref[:, g * HEAD_DIM:(g + 1) * HEAD_DIM] = out.T.astype(o_ref.dtype)


def _flash(qk, vt, q_gain, k_gain, batch, n):
    t = qk.shape[0]
    tq, tk = _tile(n, FLASH_TQ), _tile(n, FLASH_TK)
    sub = _tile(tk, FLASH_SUB)
    nq, nk = n // tq, n // tk
    gw = Q_GROUP * HEAD_DIM
    bound = (HEAD_DIM ** 0.5 * LOG2_E) * jnp.max(jnp.abs(q_gain)) * jnp.max(jnp.abs(k_gain))
    shift = bound.astype(F32).reshape(1, 1)

    def call(fixed_shift):
        return pl.pallas_call(
            functools.partial(_flash_kernel, tk=tk, sub=sub, fixed_shift=fixed_shift),
            grid=(batch, N_KV_HEADS, nq, nk),
            in_specs=[
                pl.BlockSpec(memory_space=pltpu.SMEM),
                pl.BlockSpec((tq, gw), lambda b, h, qi, ki: (b * nq + qi, h)),
                pl.BlockSpec((tk, HEAD_DIM), lambda b, h, qi, ki: (b * nk + ki, N_Q_HEADS + h)),
                pl.BlockSpec((HEAD_DIM + FLASH_ONES_ROWS, tk), lambda b, h, qi, ki: (h, b * nk + ki)),
            ],
            out_specs=pl.BlockSpec((tq, gw), lambda b, h, qi, ki: (b * nq + qi, h)),
            out_shape=jax.ShapeDtypeStruct((t, ATTN_W), BF16),
            scratch_shapes=[
                pltpu.VMEM((Q_GROUP, 1, tq), F32),
                pltpu.VMEM((Q_GROUP, HEAD_DIM + FLASH_ONES_ROWS, tq), F32),
            ],
            compiler_params=_params("parallel", "parallel", "parallel", "arbitrary"),
            name="flash_attn_fixed" if fixed_shift else "flash_attn",
        )(shift, qk, qk, vt)

    return lax.cond(2.0 * bound <= FLASH_FIXED_SHIFT_MAX_RANGE, lambda: call(True), lambda: call(False))


def _conv_kernel(a_ref, g_ref, ap_ref, gp_ref, an_ref, gn_ref, dw_ref, dwb_ref, lng_ref, lnb_ref, pw_ref,
                 o_ref, h_ref, sh_ref, *, tm, per_seq):
    i = pl.program_id(0)
    pos = i % per_seq

    def glu(a, g):
        return a.astype(F32) * _sigmoid(g.astype(F32))

    prev = jnp.where(pos == 0, 0.0, glu(ap_ref[...], gp_ref[...]))
    nxt = jnp.where(pos == per_seq - 1, 0.0, glu(an_ref[...], gn_ref[...]))
    h_ref[0:CONV_HALO, :] = prev
    h_ref[CONV_HALO:CONV_HALO + tm, :] = glu(a_ref[...], g_ref[...])
    h_ref[CONV_HALO + tm:, :] = nxt

    base = CONV_HALO - CONV_K // 2
    span = tm + 2 * CONV_HALO - SUBLANES
    for b in range(SUBLANES):
        sh_ref[b] = h_ref[b:b + span, :]
    acc = jnp.zeros((tm, CONV_W), F32) + dwb_ref[...]
    for j in range(CONV_K):
        a, b = divmod(base + j, SUBLANES)
        acc = acc + dw_ref[j:j + 1, :] * sh_ref[b, a * SUBLANES:a * SUBLANES + tm, :]
    mu = jnp.mean(acc, axis=-1, keepdims=True)
    cen = acc - mu
    var = jnp.mean(cen * cen, axis=-1, keepdims=True)
    y = cen * lax.rsqrt(var + NORM_EPS) * lng_ref[...] + lnb_ref[...]
    o_ref[...] = _dot(_silu(y).astype(BF16), pw_ref[...]).astype(o_ref.dtype)


def _conv(u, dw, dwb, lng, lnb, pw, layer, n):
    t = u.shape[0]
    tm = _tile(n, 512)
    per_seq = n // tm
    hb = tm // CONV_HALO
    last = t // CONV_HALO - 1
    ca, cg = OFF_CA // CONV_W, OFF_CG // CONV_W
    vec = lambda v: v.reshape(1, CONV_W)
    row = pl.BlockSpec((1, CONV_W), lambda i: (0, 0))
    return pl.pallas_call(
        functools.partial(_conv_kernel, tm=tm, per_seq=per_seq),
        grid=(t // tm,),
        in_specs=[
            pl.BlockSpec((tm, CONV_W), lambda i: (i, ca)),
            pl.BlockSpec((tm, CONV_W), lambda i: (i, cg)),
            pl.BlockSpec((CONV_HALO, CONV_W), lambda i: (jnp.maximum(i * hb - 1, 0), ca)),
            pl.BlockSpec((CONV_HALO, CONV_W), lambda i: (jnp.maximum(i * hb - 1, 0), cg)),
            pl.BlockSpec((CONV_HALO, CONV_W), lambda i: (jnp.minimum((i + 1) * hb, last), ca)),
            pl.BlockSpec((CONV_HALO, CONV_W), lambda i: (jnp.minimum((i + 1) * hb, last), cg)),
            pl.BlockSpec((CONV_K, CONV_W), lambda i: (0, 0)),
            row, row, row,
            pl.BlockSpec((None, CONV_W, CONV_W), lambda i: (layer, 0, 0)),
        ],
        out_specs=pl.BlockSpec((tm, CONV_W), lambda i: (i, 0)),
        out_shape=jax.ShapeDtypeStruct((t, CONV_W), BF16),
        scratch_shapes=[
            pltpu.VMEM((tm + 2 * CONV_HALO, CONV_W), F32),
            pltpu.VMEM((SUBLANES, tm + 2 * CONV_HALO - SUBLANES, CONV_W), F32),
        ],
        compiler_params=_params("parallel"),
        name="conv_mixer",
    )(u, u, u, u, u, u, dw, vec(dwb), vec(lng), vec(lnb), pw)


def _hgrn_tables(reverse):
    c = HGRN_CHUNK
    levels = []
    h = c // 2
    while h >= 1:
        levels.append(h)
        h //= 2
    r = np.arange(c)
    col = r[None, :]
    seg = np.zeros((len(levels) + 2, c, c), np.float32)
    mask = np.zeros((len(levels) + 1, c, c), np.float32)
    for li, h in enumerate(levels):
        blk = r // (2 * h)
        upper = (r % (2 * h)) >= h
        b = (blk * 2 * h + h - 1)[:, None]
        rr = r[:, None]
        seg[li] = np.where(upper[:, None], (col > b) & (col <= rr), (col > rr) & (col <= b))
        mask[li] = upper[:, None] & (~upper)[None, :] & (blk[:, None] == blk[None, :])
    seg[-2] = col <= r[:, None]
    seg[-1] = col > r[:, None]
    mask[-1] = np.eye(c)
    if reverse:
        seg = seg[:, ::-1, ::-1]
        mask = mask[:, ::-1, ::-1]
    seg = seg.reshape(-1, c)
    seg2 = np.concatenate([seg, seg], axis=1)
    return (jnp.asarray(seg2, dtype=BF16), jnp.asarray(mask, dtype=F32), len(levels))


def _hgrn_kernel(*refs, reverse, final, n_levels, n_chunks):
    if final:
        hq_ref, hf_ref, hi_ref, lb_ref, seg_ref, mask_ref, oprev_ref, hg_ref, gn_ref, o_ref, st_ref = refs
    else:
        hq_ref, hf_ref, hi_ref, lb_ref, seg_ref, mask_ref, o_ref, st_ref = refs
    c = HGRN_CHUNK

    @pl.when(pl.program_id(1) == 0)
    def _():
        st_ref[...] = jnp.zeros_like(st_ref)

    lb = lb_ref[...]
    seg = seg_ref[...]
    total_row = 0 if reverse else c - 1

    def prep(j):
        rows = slice(j * c, (j + 1) * c)
        f = lb + (1.0 - lb) * _sigmoid(hf_ref[rows, :].astype(F32))
        log2f = jnp.log(f) * LOG2_E
        g_hi = log2f.astype(BF16)
        g_lo = (log2f - g_hi.astype(F32)).astype(BF16)
        e_all = jnp.exp2(_dot(seg, jnp.concatenate([g_hi, g_lo], axis=0)))
        decay = e_all[n_levels * c + total_row:n_levels * c + total_row + 1, :]
        return _silu(hq_ref[rows, :].astype(F32)).astype(BF16), (1.0 - f).astype(BF16), e_all.astype(BF16), decay

    def intra(prepped):
        qq, kk, e_all, _ = prepped
        mats = []
        for hd in range(HGRN_HEADS):
            sl = slice(hd * HGRN_D, (hd + 1) * HGRN_D)
            qh, kh = qq[:, sl], kk[:, sl]
            a = mask_ref[n_levels] * _dot_nt(qh, kh)
            for l in range(n_levels):
                el = e_all[l * c:(l + 1) * c, sl]
                a = a + mask_ref[l] * _dot_nt(qh * el, kh * el)
            mats.append(a.astype(BF16))
        return mats

    def outputs(j, prepped, mats):
        qq, kk, e_all, decay = prepped
        rows = slice(j * c, (j + 1) * c)
        if final:
            gate = _silu(hg_ref[rows, :].astype(F32)) * gn_ref[...]
        for hd in range(HGRN_HEADS):
            sl = slice(hd * HGRN_D, (hd + 1) * HGRN_D)
            qh, kh, vh = qq[:, sl], kk[:, sl], hi_ref[rows, sl]
            e_pre = e_all[n_levels * c:(n_levels + 1) * c, sl]
            e_suf = e_all[(n_levels + 1) * c:(n_levels + 2) * c, sl]
            st = st_ref[hd]
            o = _dot(mats[hd], vh) + _dot_nt(qh * e_pre, st.astype(BF16))
            v_t = vh.astype(F32).T.astype(BF16)
            st_ref[hd] = decay[:, sl] * st + _dot(v_t, kh * e_suf)
            if final:
                o = o + oprev_ref[rows, sl]
                o = o * lax.rsqrt(jnp.mean(o * o, axis=-1, keepdims=True) + NORM_EPS)
                o_ref[rows, sl] = (o * gate[:, sl]).astype(o_ref.dtype)
            else:
                o_ref[rows, sl] = o

    order = list(range(n_chunks))[::-1] if reverse else list(range(n_chunks))
    prepped, mats = {}, {}
    for step in range(n_chunks + 2):
        if step < n_chunks:
            prepped[step] = prep(order[step])
        if 0 <= step - 1 < n_chunks:
            mats[step - 1] = intra(prepped[step - 1])
        if 0 <= step - 2 < n_chunks:
            outputs(order[step - 2], prepped.pop(step - 2), mats.pop(step - 2))


def _hgrn_pass(u, lb, batch, n, *, reverse, o_prev=None, gn=None):
    t = u.shape[0]
    n_chunks = _tile(n // HGRN_CHUNK, HGRN_CHUNKS_PER_STEP)
    c = n_chunks * HGRN_CHUNK
    nc = n // c
    final = o_prev is not None
    seg, mask, n_levels = _hgrn_tables(reverse)

    def rows(b, ci):
        return b * nc + (nc - 1 - ci if reverse else ci)

    def col_spec(off):
        return pl.BlockSpec((c, HGRN_W), lambda b, ci: (rows(b, ci), off // HGRN_W))

    const2 = lambda b, ci: (0, 0)
    in_specs = [
        col_spec(OFF_HQ),
        col_spec(OFF_FB if reverse else OFF_FF),
        col_spec(OFF_HI),
        pl.BlockSpec((1, HGRN_W), const2),
        pl.BlockSpec(seg.shape, const2),
        pl.BlockSpec(mask.shape, lambda b, ci: (0, 0, 0)),
    ]
    args = [u, u, u, lb.reshape(1, HGRN_W), seg, mask]
    if final:
        in_specs += [
            pl.BlockSpec((c, HGRN_W), lambda b, ci: (rows(b, ci), 0)),
            col_spec(OFF_HG),
            pl.BlockSpec((1, HGRN_W), const2),
        ]
        args += [o_prev, u, gn.reshape(1, HGRN_W)]
    return pl.pallas_call(
        functools.partial(_hgrn_kernel, reverse=reverse, final=final, n_levels=n_levels, n_chunks=n_chunks),
        grid=(batch, nc),
        in_specs=in_specs,
        out_specs=pl.BlockSpec((c, HGRN_W), lambda b, ci: (rows(b, ci), 0)),
        out_shape=jax.ShapeDtypeStruct((t, HGRN_W), BF16 if final else F32),
        scratch_shapes=[pltpu.VMEM((HGRN_HEADS, HGRN_D, HGRN_D), F32)],
        compiler_params=_params("parallel", "arbitrary"),
        name="hgrn_bwd" if reverse else "hgrn_fwd",
    )(*args)


def _out_proj_kernel(a_ref, c_ref, h_ref, x_ref, w_ref, g_ref, gnext_ref, o_ref, xn_ref):
    mix = _dot(a_ref[...], w_ref[0:ATTN_W, :])
    mix = mix + _dot(c_ref[...], w_ref[ATTN_W:ATTN_W + CONV_W, :])
    mix = mix + _dot(h_ref[...], w_ref[ATTN_W + CONV_W:, :])
    x = x_ref[...] + _rms(mix, g_ref[...])
    o_ref[...] = x
    xn_ref[...] = _rms(x, gnext_ref[...]).astype(xn_ref.dtype)


def _out_proj(a, cv, hg, x, w, layer, g, g_next):
    t, d = x.shape
    tm = _tile(t, 512)
    mixw = w.shape[1]
    return pl.pallas_call(
        _out_proj_kernel,
        grid=(t // tm,),
        in_specs=[
            pl.BlockSpec((tm, ATTN_W), lambda i: (i, 0)),
            pl.BlockSpec((tm, CONV_W), lambda i: (i, 0)),
            pl.BlockSpec((tm, HGRN_W), lambda i: (i, 0)),
            pl.BlockSpec((tm, d), lambda i: (i, 0)),
            pl.BlockSpec((None, mixw, d), lambda i: (layer, 0, 0)),
            pl.BlockSpec((1, d), lambda i: (0, 0)),
            pl.BlockSpec((1, d), lambda i: (0, 0)),
        ],
        out_specs=[pl.BlockSpec((tm, d), lambda i: (i, 0)), pl.BlockSpec((tm, d), lambda i: (i, 0))],
        out_shape=[jax.ShapeDtypeStruct((t, d), F32), jax.ShapeDtypeStruct((t, d), BF16)],
        compiler_params=_params("parallel"),
        name="out_proj",
    )(a, cv, hg, x, w, g.reshape(1, d), g_next.reshape(1, d))


def _ffn_kernel(xn_ref, wg_ref, wu_ref, wo_ref, gpost_ref, o_ref, acc_ref):
    j = pl.program_id(1)

    @pl.when(j == 0)
    def _():
        acc_ref[...] = jnp.zeros_like(acc_ref)

    xn = xn_ref[...]
    hidden = _silu(_dot(xn, wg_ref[...])) * _dot(xn, wu_ref[...])
    acc_ref[...] += _dot(hidden.astype(BF16), wo_ref[...])

    @pl.when(j == pl.num_programs(1) - 1)
    def _():
        o_ref[...] = _rms(acc_ref[...], gpost_ref[...]).astype(o_ref.dtype)


def _ffn(xn, w_in, w_out, layer, gpost):
    t, d = xn.shape
    f = w_out.shape[1]
    tm, th = _tile(t, 1024), _tile(f, 512)
    nh = f // th
    return pl.pallas_call(
        _ffn_kernel,
        grid=(t // tm, nh),
        in_specs=[
            pl.BlockSpec((tm, d), lambda i, j: (i, 0)),
            pl.BlockSpec((None, d, th), lambda i, j: (layer, 0, j)),
            pl.BlockSpec((None, d, th), lambda i, j: (layer, 0, nh + j)),
            pl.BlockSpec((None, th, d), lambda i, j: (layer, j, 0)),
            pl.BlockSpec((1, d), lambda i, j: (0, 0)),
        ],
        out_specs=pl.BlockSpec((tm, d), lambda i, j: (i, 0)),
        out_shape=jax.ShapeDtypeStruct((t, d), BF16),
        scratch_shapes=[pltpu.VMEM((tm, d), F32)],
        compiler_params=_params("parallel", "arbitrary"),
        name="ffn",
    )(xn, w_in, w_in, w_out, gpost.reshape(1, d))


def _ple_kernel(x_ref, y_ref, p_ref, wg_ref, wp_ref, g_ref, o_ref):
    x = x_ref[...] + y_ref[...].astype(F32)
    gate = _sigmoid(_dot(x.astype(BF16), wg_ref[...]))
    proj = _dot(p_ref[...].astype(BF16), wp_ref[...])
    o_ref[...] = x + _rms(proj * gate, g_ref[...])


def _ple(x, y, p, layer, w_gate, w_proj, g):
    t, d = x.shape
    pd = p.shape[-1]
    tm = _tile(t, 512)
    return pl.pallas_call(
        _ple_kernel,
        grid=(t // tm,),
        in_specs=[
            pl.BlockSpec((tm, d), lambda i: (i, 0)),
            pl.BlockSpec((tm, d), lambda i: (i, 0)),
            pl.BlockSpec((None, tm, pd), lambda i: (layer, i, 0)),
            pl.BlockSpec((None, d, d), lambda i: (layer, 0, 0)),
            pl.BlockSpec((None, pd, d), lambda i: (layer, 0, 0)),
            pl.BlockSpec((1, d), lambda i: (0, 0)),
        ],
        out_specs=pl.BlockSpec((tm, d), lambda i: (i, 0)),
        out_shape=jax.ShapeDtypeStruct((t, d), F32),
        compiler_params=_params("parallel"),
        name="ple",
    )(x, y, p, w_gate, w_proj, g.reshape(1, d))


def _trunk(x, p, lb, layers, mats):
    depth = p.shape[0]
    batch, n, d = x.shape
    t = batch * n
    x = x.reshape(t, d)
    p = p.reshape(depth, t, p.shape[-1])
    cos_l, sin_l = _rope_tables(n)
    for l, w in enumerate(layers):
        u = _in_proj(x, w["norm_mix_pre"], mats["w_in"], l)
        qk_r, v_t = _attn_prep(u, cos_l, sin_l, w["q_norm"], w["k_norm"], n)
        a_out = _flash(qk_r, v_t, w["q_norm"], w["k_norm"], batch, n)
        c_out = _conv(u, w["conv_dw"], w["conv_dw_b"], w["conv_ln_g"], w["conv_ln_b"], mats["conv_pw"], l, n)
        o_fwd = _hgrn_pass(u, lb[l, 0], batch, n, reverse=False)
        h_out = _hgrn_pass(u, lb[l, 1], batch, n, reverse=True, o_prev=o_fwd, gn=w["hgrn_gn"])
        x, xn = _out_proj(a_out, c_out, h_out, x, mats["w_out"], l, w["norm_mix_post"], w["norm_ffn_pre"])
        y = _ffn(xn, mats["w_ffn_in"], mats["w_ffn_out"], l, w["norm_ffn_post"])
        x = _ple(x, y, p, l, mats["w_ple_gate"], mats["w_ple_proj"], w["ple_norm"])
    return x.reshape(batch, n, d)


def kernel(x_prompt, x_sample, p_prompt, p_sample, norm_mix_pre, norm_mix_post, w_in, q_norm, k_norm, conv_dw, conv_dw_b, conv_ln_g, conv_ln_b, conv_pw, hgrn_lb, hgrn_gn, w_out, norm_ffn_pre, norm_ffn_post, w_ffn_in, w_ffn_out, w_ple_gate, w_ple_proj, ple_norm):
    sm = jax.nn.softmax(hgrn_lb.astype(F32), axis=0)
    lb = jnp.cumsum(sm, axis=0) - sm[0]
    vectors = dict(
        norm_mix_pre=norm_mix_pre, norm_mix_post=norm_mix_post, q_norm=q_norm, k_norm=k_norm,
        conv_dw=conv_dw, conv_dw_b=conv_dw_b, conv_ln_g=conv_ln_g, conv_ln_b=conv_ln_b,
        hgrn_gn=hgrn_gn, norm_ffn_pre=norm_ffn_pre, norm_ffn_post=norm_ffn_post, ple_norm=ple_norm,
    )
    matrices = dict(
        w_in=w_in, conv_pw=conv_pw, w_out=w_out, w_ffn_in=w_ffn_in, w_ffn_out=w_ffn_out,
        w_ple_gate=w_ple_gate, w_ple_proj=w_ple_proj,
    )
    mats = {k: v.astype(BF16) for k, v in matrices.items()}
    layers = [{k: v[l] for k, v in vectors.items()} for l in range(p_prompt.shape[0])]
    y_prompt = _trunk(x_prompt, p_prompt, lb, layers, mats)
    y_sample = _trunk(x_sample, p_sample, lb, layers, mats)
    return (y_prompt, y_sample)
```

```python
import functools

import numpy as np
import jax
import jax.numpy as jnp
from jax import lax
from jax.experimental import pallas as pl
from jax.experimental.pallas import tpu as pltpu

F32 = jnp.float32
BF16 = jnp.bfloat16

NORM_EPS = 1e-6
ROPE_THETA = 10000.0
LOG2_E = 1.4426950408889634
GRID_W = 64
HEAD_DIM = 128
N_Q_HEADS = 8
N_KV_HEADS = 2
Q_GROUP = N_Q_HEADS // N_KV_HEADS
ATTN_W = N_Q_HEADS * HEAD_DIM
KV_W = N_KV_HEADS * HEAD_DIM
CONV_W = 512
CONV_K = 31
CONV_HALO = 16
SUBLANES = 8
HGRN_W = 512
HGRN_D = 128
HGRN_HEADS = HGRN_W // HGRN_D
HGRN_CHUNK = 128
HGRN_CHUNKS_PER_STEP = 8
FLASH_ONES_ROWS = 16
FLASH_LOOKAHEAD = 3
FLASH_TQ = 512
FLASH_TK = 8192
FLASH_SUB = 512
FLASH_FIXED_SHIFT_MAX_RANGE = 96.0

OFF_Q = 0
OFF_K = OFF_Q + ATTN_W
OFF_V = OFF_K + KV_W
OFF_CA = OFF_V + KV_W
OFF_CG = OFF_CA + CONV_W
OFF_HQ = OFF_CG + CONV_W
OFF_FF = OFF_HQ + HGRN_W
OFF_FB = OFF_FF + HGRN_W
OFF_HI = OFF_FB + HGRN_W
OFF_HG = OFF_HI + HGRN_W
IN_COLS = OFF_HG + HGRN_W

V7X_VMEM_LIMIT_BYTES = 56 * 1024 * 1024


def _params(*semantics):
    return pltpu.CompilerParams(dimension_semantics=semantics, vmem_limit_bytes=V7X_VMEM_LIMIT_BYTES)


def _tile(dim, pref):
    t = min(dim, pref)
    assert dim % t == 0, (dim, pref)
    return t


def _row_halves(rows):
    half = rows // 2
    return [slice(0, half), slice(half, rows)] if half % 16 == 0 else [slice(0, rows)]


def _rms(x, g):
    return x * lax.rsqrt(jnp.mean(x * x, axis=-1, keepdims=True) + NORM_EPS) * g


def _sigmoid(x):
    return 1.0 / (1.0 + jnp.exp(-x))


def _silu(x):
    return x * _sigmoid(x)


def _dot(a, b):
    return jnp.dot(a, b, preferred_element_type=F32)


def _dot_nt(a, b):
    return lax.dot_general(a, b, (((1,), (1,)), ((), ())), preferred_element_type=F32)


def _in_proj_kernel(x_ref, g_ref, w_ref, o_ref, xn_ref):
    @pl.when(pl.program_id(1) == 0)
    def _():
        xn_ref[...] = _rms(x_ref[...], g_ref[...]).astype(BF16)

    o_ref[...] = _dot(xn_ref[...], w_ref[...]).astype(o_ref.dtype)


def _in_proj(x, g, w, layer):
    t, d = x.shape
    n = w.shape[-1]
    tm, tn = _tile(t, 1024), _tile(n, 1024)
    return pl.pallas_call(
        _in_proj_kernel,
        grid=(t // tm, n // tn),
        in_specs=[
            pl.BlockSpec((tm, d), lambda i, j: (i, 0)),
            pl.BlockSpec((1, d), lambda i, j: (0, 0)),
            pl.BlockSpec((None, d, tn), lambda i, j: (layer, 0, j)),
        ],
        out_specs=pl.BlockSpec((tm, tn), lambda i, j: (i, j)),
        out_shape=jax.ShapeDtypeStruct((t, n), BF16),
        scratch_shapes=[pltpu.VMEM((tm, d), BF16)],
        compiler_params=_params("parallel", "arbitrary"),
        name="in_proj",
    )(x, g.reshape(1, d), w)


def _rope_tables(n):
    rows = n // GRID_W
    row = jnp.repeat(jnp.arange(rows, dtype=F32), GRID_W)
    col = jnp.tile(jnp.arange(GRID_W, dtype=F32), rows)
    axis_dims = HEAD_DIM // 2
    inv = ROPE_THETA ** (-jnp.arange(0, axis_dims, 2, dtype=F32) / axis_dims)
    ang = jnp.concatenate([row[:, None] * inv, col[:, None] * inv], axis=-1)
    cos, sin = jnp.cos(ang), jnp.sin(ang)
    cos_l = jnp.repeat(cos, 2, axis=-1)
    sin_l = jnp.stack([-sin, sin], axis=-1).reshape(n, HEAD_DIM)
    return cos_l, sin_l


def _attn_prep_kernel(q_ref, k_ref, v_ref, cos_ref, sin_ref, qg_ref, kg_ref, qk_ref, vt_ref):
    cos = cos_ref[...]
    sin = sin_ref[...]
    lane = lax.broadcasted_iota(jnp.int32, cos.shape, 1)
    even = (lane & 1) == 0

    def norm_rope(x, g):
        y = _rms(x.astype(F32), g)
        partner = jnp.where(even, pltpu.roll(y, HEAD_DIM - 1, 1), pltpu.roll(y, 1, 1))
        return y * cos + partner * sin

    scale = HEAD_DIM ** -0.5 * LOG2_E
    for h in range(N_Q_HEADS):
        sl = slice(h * HEAD_DIM, (h + 1) * HEAD_DIM)
        qk_ref[:, sl] = (norm_rope(q_ref[:, sl], qg_ref[...]) * scale).astype(qk_ref.dtype)
    for h in range(N_KV_HEADS):
        sl = slice(h * HEAD_DIM, (h + 1) * HEAD_DIM)
        so = slice(ATTN_W + h * HEAD_DIM, ATTN_W + (h + 1) * HEAD_DIM)
        qk_ref[:, so] = norm_rope(k_ref[:, sl], kg_ref[...]).astype(qk_ref.dtype)
    rows = HEAD_DIM + FLASH_ONES_ROWS
    for h in range(N_KV_HEADS):
        vt_ref[h * rows:h * rows + HEAD_DIM, :] = (
            v_ref[:, h * HEAD_DIM:(h + 1) * HEAD_DIM].astype(F32).T.astype(vt_ref.dtype))
        vt_ref[h * rows + HEAD_DIM:(h + 1) * rows, :] = jnp.ones((FLASH_ONES_ROWS, v_ref.shape[0]), vt_ref.dtype)


def _attn_prep(u, cos_l, sin_l, q_gain, k_gain, n):
    t = u.shape[0]
    tm = _tile(n, 512)
    per_seq = n // tm
    return pl.pallas_call(
        _attn_prep_kernel,
        grid=(t // tm,),
        in_specs=[
            pl.BlockSpec((tm, ATTN_W), lambda i: (i, OFF_Q // ATTN_W)),
            pl.BlockSpec((tm, KV_W), lambda i: (i, OFF_K // KV_W)),
            pl.BlockSpec((tm, KV_W), lambda i: (i, OFF_V // KV_W)),
            pl.BlockSpec((tm, HEAD_DIM), lambda i: (i % per_seq, 0)),
            pl.BlockSpec((tm, HEAD_DIM), lambda i: (i % per_seq, 0)),
            pl.BlockSpec((1, HEAD_DIM), lambda i: (0, 0)),
            pl.BlockSpec((1, HEAD_DIM), lambda i: (0, 0)),
        ],
        out_specs=[
            pl.BlockSpec((tm, ATTN_W + KV_W), lambda i: (i, 0)),
            pl.BlockSpec((N_KV_HEADS * (HEAD_DIM + FLASH_ONES_ROWS), tm), lambda i: (0, i)),
        ],
        out_shape=[
            jax.ShapeDtypeStruct((t, ATTN_W + KV_W), BF16),
            jax.ShapeDtypeStruct((N_KV_HEADS * (HEAD_DIM + FLASH_ONES_ROWS), t), BF16),
        ],
        compiler_params=_params("parallel"),
        name="attn_prep",
    )(u, u, u, cos_l, sin_l, q_gain.reshape(1, HEAD_DIM), k_gain.reshape(1, HEAD_DIM))


def _flash_kernel(shift_ref, q_ref, k_ref, vt_ref, o_ref, m_ref, acc_ref, *, tk, sub, fixed_shift):
    ki = pl.program_id(3)

    @pl.when(ki == 0)
    def _():
        m_ref[...] = jnp.full_like(m_ref, -jnp.inf)
        acc_ref[...] = jnp.zeros_like(acc_ref)

    stages = [(s0, g) for s0 in range(0, tk, sub) for g in range(Q_GROUP)]

    def scores(stage):
        s0, g = stage
        return _dot_nt(k_ref[s0:s0 + sub, :], q_ref[:, g * HEAD_DIM:(g + 1) * HEAD_DIM])

    shift = shift_ref[0, 0]
    partial = [None] * Q_GROUP

    pending = [scores(st) for st in stages[:FLASH_LOOKAHEAD]]
    for i, (s0, g) in enumerate(stages):
        s = pending.pop(0)
        if i + FLASH_LOOKAHEAD < len(stages):
            pending.append(scores(stages[i + FLASH_LOOKAHEAD]))
        vt1 = vt_ref[:, s0:s0 + sub]
        if fixed_shift:
            pv = _dot(vt1, jnp.exp2(s - shift).astype(BF16))
            partial[g] = pv if partial[g] is None else partial[g] + pv
        else:
            m_prev = m_ref[g]
            m_new = jnp.maximum(m_prev, jnp.max(s, axis=0, keepdims=True))
            alpha = jnp.exp2(m_prev - m_new)
            p = jnp.exp2((s - m_new).astype(BF16))
            acc_ref[g] = alpha * acc_ref[g] + _dot(vt1, p)
            m_ref[g] = m_new
    if fixed_shift:
        for g in range(Q_GROUP):
            acc_ref[g] += partial[g]

    @pl.when(ki == pl.num_programs(3) - 1)
    def _():
        for g in range(Q_GROUP):
            out = acc_ref[g, 0:HEAD_DIM, :] / acc_ref[g, HEAD_DIM:HEAD_DIM + 1, :]
            o_ref[:, g * HEAD_DIM:(g + 1) * HEAD_DIM] = out.T.astype(o_ref.dtype)


def _flash(qk, vt, q_gain, k_gain, batch, n):
    t = qk.shape[0]
    tq, tk = _tile(n, FLASH_TQ), _tile(n, FLASH_TK)
    sub = _tile(tk, FLASH_SUB)
    nq, nk = n // tq, n // tk
    gw = Q_GROUP * HEAD_DIM
    bound = (HEAD_DIM ** 0.5 * LOG2_E) * jnp.max(jnp.abs(q_gain)) * jnp.max(jnp.abs(k_gain))
    shift = bound.astype(F32).reshape(1, 1)

    def call(fixed_shift):
        return pl.pallas_call(
            functools.partial(_flash_kernel, tk=tk, sub=sub, fixed_shift=fixed_shift),
            grid=(batch, N_KV_HEADS, nq, nk),
            in_specs=[
                pl.BlockSpec(memory_space=pltpu.SMEM),
                pl.BlockSpec((tq, gw), lambda b, h, qi, ki: (b * nq + qi, h)),
                pl.BlockSpec((tk, HEAD_DIM), lambda b, h, qi, ki: (b * nk + ki, N_Q_HEADS + h)),
                pl.BlockSpec((HEAD_DIM + FLASH_ONES_ROWS, tk), lambda b, h, qi, ki: (h, b * nk + ki)),
            ],
            out_specs=pl.BlockSpec((tq, gw), lambda b, h, qi, ki: (b * nq + qi, h)),
            out_shape=jax.ShapeDtypeStruct((t, ATTN_W), BF16),
            scratch_shapes=[
                pltpu.VMEM((Q_GROUP, 1, tq), F32),
                pltpu.VMEM((Q_GROUP, HEAD_DIM + FLASH_ONES_ROWS, tq), F32),
            ],
            compiler_params=_params("parallel", "parallel", "parallel", "arbitrary"),
            name="flash_attn_fixed" if fixed_shift else "flash_attn",
        )(shift, qk, qk, vt)

    return lax.cond(2.0 * bound <= FLASH_FIXED_SHIFT_MAX_RANGE, lambda: call(True), lambda: call(False))


def _conv_kernel(a_ref, g_ref, ap_ref, gp_ref, an_ref, gn_ref, dw_ref, dwb_ref, lng_ref, lnb_ref, pw_ref,
                 o_ref, h_ref, sh_ref, *, tm, per_seq):
    i = pl.program_id(0)
    pos = i % per_seq

    def glu(a, g):
        return a.astype(F32) * _sigmoid(g.astype(F32))

    prev = jnp.where(pos == 0, 0.0, glu(ap_ref[...], gp_ref[...]))
    nxt = jnp.where(pos == per_seq - 1, 0.0, glu(an_ref[...], gn_ref[...]))
    h_ref[0:CONV_HALO, :] = prev
    h_ref[CONV_HALO:CONV_HALO + tm, :] = glu(a_ref[...], g_ref[...])
    h_ref[CONV_HALO + tm:, :] = nxt

    base = CONV_HALO - CONV_K // 2
    span = tm + 2 * CONV_HALO - SUBLANES
    for b in range(SUBLANES):
        sh_ref[b] = h_ref[b:b + span, :]
    acc = jnp.zeros((tm, CONV_W), F32) + dwb_ref[...]
    for j in range(CONV_K):
        a, b = divmod(base + j, SUBLANES)
        acc = acc + dw_ref[j:j + 1, :] * sh_ref[b, a * SUBLANES:a * SUBLANES + tm, :]
    mu = jnp.mean(acc, axis=-1, keepdims=True)
    cen = acc - mu
    var = jnp.mean(cen * cen, axis=-1, keepdims=True)
    y = cen * lax.rsqrt(var + NORM_EPS) * lng_ref[...] + lnb_ref[...]
    o_ref[...] = _dot(_silu(y).astype(BF16), pw_ref[...]).astype(o_ref.dtype)


def _conv(u, dw, dwb, lng, lnb, pw, layer, n):
    t = u.shape[0]
    tm = _tile(n, 512)
    per_seq = n // tm
    hb = tm // CONV_HALO
    last = t // CONV_HALO - 1
    ca, cg = OFF_CA // CONV_W, OFF_CG // CONV_W
    vec = lambda v: v.reshape(1, CONV_W)
    row = pl.BlockSpec((1, CONV_W), lambda i: (0, 0))
    return pl.pallas_call(
        functools.partial(_conv_kernel, tm=tm, per_seq=per_seq),
        grid=(t // tm,),
        in_specs=[
            pl.BlockSpec((tm, CONV_W), lambda i: (i, ca)),
            pl.BlockSpec((tm, CONV_W), lambda i: (i, cg)),
            pl.BlockSpec((CONV_HALO, CONV_W), lambda i: (jnp.maximum(i * hb - 1, 0), ca)),
            pl.BlockSpec((CONV_HALO, CONV_W), lambda i: (jnp.maximum(i * hb - 1, 0), cg)),
            pl.BlockSpec((CONV_HALO, CONV_W), lambda i: (jnp.minimum((i + 1) * hb, last), ca)),
            pl.BlockSpec((CONV_HALO, CONV_W), lambda i: (jnp.minimum((i + 1) * hb, last), cg)),
            pl.BlockSpec((CONV_K, CONV_W), lambda i: (0, 0)),
            row, row, row,
            pl.BlockSpec((None, CONV_W, CONV_W), lambda i: (layer, 0, 0)),
        ],
        out_specs=pl.BlockSpec((tm, CONV_W), lambda i: (i, 0)),
        out_shape=jax.ShapeDtypeStruct((t, CONV_W), BF16),
        scratch_shapes=[
            pltpu.VMEM((tm + 2 * CONV_HALO, CONV_W), F32),
            pltpu.VMEM((SUBLANES, tm + 2 * CONV_HALO - SUBLANES, CONV_W), F32),
        ],
        compiler_params=_params("parallel"),
        name="conv_mixer",
    )(u, u, u, u, u, u, dw, vec(dwb), vec(lng), vec(lnb), pw)


def _hgrn_tables(reverse):
    c = HGRN_CHUNK
    levels = []
    h = c // 2
    while h >= 1:
        levels.append(h)
        h //= 2
    r = np.arange(c)
    col = r[None, :]
    seg = np.zeros((len(levels) + 2, c, c), np.float32)
    mask = np.zeros((len(levels) + 1, c, c), np.float32)
    for li, h in enumerate(levels):
        blk = r // (2 * h)
        upper = (r % (2 * h)) >= h
        b = (blk * 2 * h + h - 1)[:, None]
        rr = r[:, None]
        seg[li] = np.where(upper[:, None], (col > b) & (col <= rr), (col > rr) & (col <= b))
        mask[li] = upper[:, None] & (~upper)[None, :] & (blk[:, None] == blk[None, :])
    seg[-2] = col <= r[:, None]
    seg[-1] = col > r[:, None]
    mask[-1] = np.eye(c)
    if reverse:
        seg = seg[:, ::-1, ::-1]
        mask = mask[:, ::-1, ::-1]
    seg = seg.reshape(-1, c)
    seg2 = np.concatenate([seg, seg], axis=1)
    return (jnp.asarray(seg2, dtype=BF16), jnp.asarray(mask, dtype=F32), len(levels))


def _hgrn_kernel(*refs, reverse, final, n_levels, n_chunks):
    if final:
        hq_ref, hf_ref, hi_ref, lb_ref, seg_ref, mask_ref, oprev_ref, hg_ref, gn_ref, o_ref, st_ref = refs
    else:
        hq_ref, hf_ref, hi_ref, lb_ref, seg_ref, mask_ref, o_ref, st_ref = refs
    c = HGRN_CHUNK

    @pl.when(pl.program_id(1) == 0)
    def _():
        st_ref[...] = jnp.zeros_like(st_ref)

    lb = lb_ref[...]
    seg = seg_ref[...]
    total_row = 0 if reverse else c - 1

    def prep(j):
        rows = slice(j * c, (j + 1) * c)
        f = lb + (1.0 - lb) * _sigmoid(hf_ref[rows, :].astype(F32))
        log2f = jnp.log(f) * LOG2_E
        g_hi = log2f.astype(BF16)
        g_lo = (log2f - g_hi.astype(F32)).astype(BF16)
        e_all = jnp.exp2(_dot(seg, jnp.concatenate([g_hi, g_lo], axis=0)))
        decay = e_all[n_levels * c + total_row:n_levels * c + total_row + 1, :]
        return _silu(hq_ref[rows, :].astype(F32)).astype(BF16), (1.0 - f).astype(BF16), e_all.astype(BF16), decay

    def intra(prepped):
        qq, kk, e_all, _ = prepped
        mats = []
        for hd in range(HGRN_HEADS):
            sl = slice(hd * HGRN_D, (hd + 1) * HGRN_D)
            qh, kh = qq[:, sl], kk[:, sl]
            a = mask_ref[n_levels] * _dot_nt(qh, kh)
            for l in range(n_levels):
                el = e_all[l * c:(l + 1) * c, sl]
                a = a + mask_ref[l] * _dot_nt(qh * el, kh * el)
            mats.append(a.astype(BF16))
        return mats

    def outputs(j, prepped, mats):
        qq, kk, e_all, decay = prepped
        rows = slice(j * c, (j + 1) * c)
        if final:
            gate = _silu(hg_ref[rows, :].astype(F32)) * gn_ref[...]
        for hd in range(HGRN_HEADS):
            sl = slice(hd * HGRN_D, (hd + 1) * HGRN_D)
            qh, kh, vh = qq[:, sl], kk[:, sl], hi_ref[rows, sl]
            e_pre = e_all[n_levels * c:(n_levels + 1) * c, sl]
            e_suf = e_all[(n_levels + 1) * c:(n_levels + 2) * c, sl]
            st = st_ref[hd]
            o = _dot(mats[hd], vh) + _dot_nt(qh * e_pre, st.astype(BF16))
            v_t = vh.astype(F32).T.astype(BF16)
            st_ref[hd] = decay[:, sl] * st + _dot(v_t, kh * e_suf)
            if final:
                o = o + oprev_ref[rows, sl]
                o = o * lax.rsqrt(jnp.mean(o * o, axis=-1, keepdims=True) + NORM_EPS)
                o_ref[rows, sl] = (o * gate[:, sl]).astype(o_ref.dtype)
            else:
                o_ref[rows, sl] = o

    order = list(range(n_chunks))[::-1] if reverse else list(range(n_chunks))
    prepped, mats = {}, {}
    for step in range(n_chunks + 2):
        if step < n_chunks:
            prepped[step] = prep(order[step])
        if 0 <= step - 1 < n_chunks:
            mats[step - 1] = intra(prepped[step - 1])
        if 0 <= step - 2 < n_chunks:
            outputs(order[step - 2], prepped.pop(step - 2), mats.pop(step - 2))


def _hgrn_pass(u, lb, batch, n, *, reverse, o_prev=None, gn=None):
    t = u.shape[0]
    n_chunks = _tile(n // HGRN_CHUNK, HGRN_CHUNKS_PER_STEP)
    c = n_chunks * HGRN_CHUNK
    nc = n // c
    final = o_prev is not None
    seg, mask, n_levels = _hgrn_tables(reverse)

    def rows(b, ci):
        return b * nc + (nc - 1 - ci if reverse else ci)

    def col_spec(off):
        return pl.BlockSpec((c, HGRN_W), lambda b, ci: (rows(b, ci), off // HGRN_W))

    const2 = lambda b, ci: (0, 0)
    in_specs = [
        col_spec(OFF_HQ),
        col_spec(OFF_FB if reverse else OFF_FF),
        col_spec(OFF_HI),
        pl.BlockSpec((1, HGRN_W), const2),
        pl.BlockSpec(seg.shape, const2),
        pl.BlockSpec(mask.shape, lambda b, ci: (0, 0, 0)),
    ]
    args = [u, u, u, lb.reshape(1, HGRN_W), seg, mask]
    if final:
        in_specs += [
            pl.BlockSpec((c, HGRN_W), lambda b, ci: (rows(b, ci), 0)),
            col_spec(OFF_HG),
            pl.BlockSpec((1, HGRN_W), const2),
        ]
        args += [o_prev, u, gn.reshape(1, HGRN_W)]
    return pl.pallas_call(
        functools.partial(_hgrn_kernel, reverse=reverse, final=final, n_levels=n_levels, n_chunks=n_chunks),
        grid=(batch, nc),
        in_specs=in_specs,
        out_specs=pl.BlockSpec((c, HGRN_W), lambda b, ci: (rows(b, ci), 0)),
        out_shape=jax.ShapeDtypeStruct((t, HGRN_W), BF16 if final else F32),
        scratch_shapes=[pltpu.VMEM((HGRN_HEADS, HGRN_D, HGRN_D), F32)],
        compiler_params=_params("parallel", "arbitrary"),
        name="hgrn_bwd" if reverse else "hgrn_fwd",
    )(*args)


def _out_proj_kernel(a_ref, c_ref, h_ref, x_ref, w_ref, g_ref, gnext_ref, o_ref, xn_ref):
    halves = _row_halves(x_ref.shape[0])
    mixes = []
    for rows in halves:
        mix = _dot(a_ref[rows, :], w_ref[0:ATTN_W, :])
        mix = mix + _dot(c_ref[rows, :], w_ref[ATTN_W:ATTN_W + CONV_W, :])
        mixes.append(mix + _dot(h_ref[rows, :], w_ref[ATTN_W + CONV_W:, :]))
    for rows, mix in zip(halves, mixes):
        x = x_ref[rows, :] + _rms(mix, g_ref[...])
        o_ref[rows, :] = x
        xn_ref[rows, :] = _rms(x, gnext_ref[...]).astype(xn_ref.dtype)


def _out_proj(a, cv, hg, x, w, layer, g, g_next):
    t, d = x.shape
    tm = _tile(t, 512)
    mixw = w.shape[1]
    return pl.pallas_call(
        _out_proj_kernel,
        grid=(t // tm,),
        in_specs=[
            pl.BlockSpec((tm, ATTN_W), lambda i: (i, 0)),
            pl.BlockSpec((tm, CONV_W), lambda i: (i, 0)),
            pl.BlockSpec((tm, HGRN_W), lambda i: (i, 0)),
            pl.BlockSpec((tm, d), lambda i: (i, 0)),
            pl.BlockSpec((None, mixw, d), lambda i: (layer, 0, 0)),
            pl.BlockSpec((1, d), lambda i: (0, 0)),
            pl.BlockSpec((1, d), lambda i: (0, 0)),
        ],
        out_specs=[pl.BlockSpec((tm, d), lambda i: (i, 0)), pl.BlockSpec((tm, d), lambda i: (i, 0))],
        out_shape=[jax.ShapeDtypeStruct((t, d), F32), jax.ShapeDtypeStruct((t, d), BF16)],
        compiler_params=_params("parallel"),
        name="out_proj",
    )(a, cv, hg, x, w, g.reshape(1, d), g_next.reshape(1, d))


def _ffn_kernel(xn_ref, wg_ref, wu_ref, wo_ref, gpost_ref, o_ref, acc_ref):
    j = pl.program_id(1)

    @pl.when(j == 0)
    def _():
        acc_ref[...] = jnp.zeros_like(acc_ref)

    xn = xn_ref[...]
    hidden = _silu(_dot(xn, wg_ref[...])) * _dot(xn, wu_ref[...])
    acc_ref[...] += _dot(hidden.astype(BF16), wo_ref[...])

    @pl.when(j == pl.num_programs(1) - 1)
    def _():
        o_ref[...] = _rms(acc_ref[...], gpost_ref[...]).astype(o_ref.dtype)


def _ffn(xn, w_in, w_out, layer, gpost):
    t, d = xn.shape
    f = w_out.shape[1]
    tm, th = _tile(t, 1024), _tile(f, 512)
    nh = f // th
    return pl.pallas_call(
        _ffn_kernel,
        grid=(t // tm, nh),
        in_specs=[
            pl.BlockSpec((tm, d), lambda i, j: (i, 0)),
            pl.BlockSpec((None, d, th), lambda i, j: (layer, 0, j)),
            pl.BlockSpec((None, d, th), lambda i, j: (layer, 0, nh + j)),
            pl.BlockSpec((None, th, d), lambda i, j: (layer, j, 0)),
            pl.BlockSpec((1, d), lambda i, j: (0, 0)),
        ],
        out_specs=pl.BlockSpec((tm, d), lambda i, j: (i, 0)),
        out_shape=jax.ShapeDtypeStruct((t, d), BF16),
        scratch_shapes=[pltpu.VMEM((tm, d), F32)],
        compiler_params=_params("parallel", "arbitrary"),
        name="ffn",
    )(xn, w_in, w_in, w_out, gpost.reshape(1, d))


def _ple_kernel(x_ref, y_ref, p_ref, wg_ref, wp_ref, g_ref, o_ref):
    halves = _row_halves(x_ref.shape[0])
    parts = []
    for rows in halves:
        x = x_ref[rows, :] + y_ref[rows, :].astype(F32)
        parts.append((x, _dot(x.astype(BF16), wg_ref[...]), _dot(p_ref[rows, :].astype(BF16), wp_ref[...])))
    for rows, (x, gate_logit, proj) in zip(halves, parts):
        o_ref[rows, :] = x + _rms(proj * _sigmoid(gate_logit), g_ref[...])


def _ple(x, y, p, layer, w_gate, w_proj, g):
    t, d = x.shape
    pd = p.shape[-1]
    tm = _tile(t, 512)
    return pl.pallas_call(
        _ple_kernel,
        grid=(t // tm,),
        in_specs=[
            pl.BlockSpec((tm, d), lambda i: (i, 0)),
            pl.BlockSpec((tm, d), lambda i: (i, 0)),
            pl.BlockSpec((None, tm, pd), lambda i: (layer, i, 0)),
            pl.BlockSpec((None, d, d), lambda i: (layer, 0, 0)),
            pl.BlockSpec((None, pd, d), lambda i: (layer, 0, 0)),
            pl.BlockSpec((1, d), lambda i: (0, 0)),
        ],
        out_specs=pl.BlockSpec((tm, d), lambda i: (i, 0)),
        out_shape=jax.ShapeDtypeStruct((t, d), F32),
        compiler_params=_params("parallel"),
        name="ple",
    )(x, y, p, w_gate, w_proj, g.reshape(1, d))


def _trunk(x, p, lb, layers, mats):
    depth = p.shape[0]
    batch, n, d = x.shape
    t = batch * n
    x = x.reshape(t, d)
    p = p.reshape(depth, t, p.shape[-1])
    cos_l, sin_l = _rope_tables(n)
    for l, w in enumerate(layers):
        u = _in_proj(x, w["norm_mix_pre"], mats["w_in"], l)
        qk_r, v_t = _attn_prep(u, cos_l, sin_l, w["q_norm"], w["k_norm"], n)
        a_out = _flash(qk_r, v_t, w["q_norm"], w["k_norm"], batch, n)
        c_out = _conv(u, w["conv_dw"], w["conv_dw_b"], w["conv_ln_g"], w["conv_ln_b"], mats["conv_pw"], l, n)
        o_fwd = _hgrn_pass(u, lb[l, 0], batch, n, reverse=False)
        h_out = _hgrn_pass(u, lb[l, 1], batch, n, reverse=True, o_prev=o_fwd, gn=w["hgrn_gn"])
        x, xn = _out_proj(a_out, c_out, h_out, x, mats["w_out"], l, w["norm_mix_post"], w["norm_ffn_pre"])
        y = _ffn(xn, mats["w_ffn_in"], mats["w_ffn_out"], l, w["norm_ffn_post"])
        x = _ple(x, y, p, l, mats["w_ple_gate"], mats["w_ple_proj"], w["ple_norm"])
    return x.reshape(batch, n, d)


def kernel(x_prompt, x_sample, p_prompt, p_sample, norm_mix_pre, norm_mix_post, w_in, q_norm, k_norm, conv_dw, conv_dw_b, conv_ln_g, conv_ln_b, conv_pw, hgrn_lb, hgrn_gn, w_out, norm_ffn_pre, norm_ffn_post, w_ffn_in, w_ffn_out, w_ple_gate, w_ple_proj, ple_norm):
    sm = jax.nn.softmax(hgrn_lb.astype(F32), axis=0)
    lb = jnp.cumsum(sm, axis=0) - sm[0]
    vectors = dict(
        norm_mix_pre=norm_mix_pre, norm_mix_post=norm_mix_post, q_norm=q_norm, k_norm=k_norm,
        conv_dw=conv_dw, conv_dw_b=conv_dw_b, conv_ln_g=conv_ln_g, conv_ln_b=conv_ln_b,
        hgrn_gn=hgrn_gn, norm_ffn_pre=norm_ffn_pre, norm_ffn_post=norm_ffn_post, ple_norm=ple_norm,
    )
    matrices = dict(
        w_in=w_in, conv_pw=conv_pw, w_out=w_out, w_ffn_in=w_ffn_in, w_ffn_out=w_ffn_out,
        w_ple_gate=w_ple_gate, w_ple_proj=w_ple_proj,
    )
    mats = {k: v.astype(BF16) for k, v in matrices.items()}
    layers = [{k: v[l] for k, v in vectors.items()} for l in range(p_prompt.shape[0])]
    y_prompt = _trunk(x_prompt, p_prompt, lb, layers, mats)
    y_sample = _trunk(x_sample, p_sample, lb, layers, mats)
    return (y_prompt, y_sample)
```

```python
import functools

import numpy as np
import jax
import jax.numpy as jnp
from jax import lax
from jax.experimental import pallas as pl
from jax.experimental.pallas import tpu as pltpu

F32 = jnp.float32
BF16 = jnp.bfloat16

NORM_EPS = 1e-6
ROPE_THETA = 10000.0
LOG2_E = 1.4426950408889634
GRID_W = 64
HEAD_DIM = 128
N_Q_HEADS = 8
N_KV_HEADS = 2
Q_GROUP = N_Q_HEADS // N_KV_HEADS
ATTN_W = N_Q_HEADS * HEAD_DIM
KV_W = N_KV_HEADS * HEAD_DIM
CONV_W = 512
CONV_K = 31
CONV_HALO = 16
SUBLANES = 8
HGRN_W = 512
HGRN_D = 128
HGRN_HEADS = HGRN_W // HGRN_D
HGRN_CHUNK = 128
HGRN_CHUNKS_PER_STEP = 16
FLASH_ONES_ROWS = 16
FLASH_LOOKAHEAD = 3
FLASH_TQ = 512
FLASH_TK = 8192
FLASH_SUB = 512
FLASH_FIXED_SHIFT_MAX_RANGE = 96.0

OFF_Q = 0
OFF_K = OFF_Q + ATTN_W
OFF_V = OFF_K + KV_W
OFF_CA = OFF_V + KV_W
OFF_CG = OFF_CA + CONV_W
OFF_HQ = OFF_CG + CONV_W
OFF_FF = OFF_HQ + HGRN_W
OFF_FB = OFF_FF + HGRN_W
OFF_HI = OFF_FB + HGRN_W
OFF_HG = OFF_HI + HGRN_W
IN_COLS = OFF_HG + HGRN_W

V7X_VMEM_LIMIT_BYTES = 56 * 1024 * 1024


def _params(*semantics):
    return pltpu.CompilerParams(dimension_semantics=semantics, vmem_limit_bytes=V7X_VMEM_LIMIT_BYTES)


def _tile(dim, pref):
    t = min(dim, pref)
    assert dim % t == 0, (dim, pref)
    return t


def _row_halves(rows):
    half = rows // 2
    return [slice(0, half), slice(half, rows)] if half % 16 == 0 else [slice(0, rows)]


def _rms(x, g):
    return x * lax.rsqrt(jnp.mean(x * x, axis=-1, keepdims=True) + NORM_EPS) * g


def _sigmoid(x):
    return 1.0 / (1.0 + jnp.exp(-x))


def _silu(x):
    return x * _sigmoid(x)


def _dot(a, b):
    return jnp.dot(a, b, preferred_element_type=F32)


def _dot_nt(a, b):
    return lax.dot_general(a, b, (((1,), (1,)), ((), ())), preferred_element_type=F32)


def _in_proj_kernel(x_ref, g_ref, w_ref, o_ref, xn_ref):
    @pl.when(pl.program_id(1) == 0)
    def _():
        xn_ref[...] = _rms(x_ref[...], g_ref[...]).astype(BF16)

    o_ref[...] = _dot(xn_ref[...], w_ref[...]).astype(o_ref.dtype)


def _in_proj(x, g, w, layer):
    t, d = x.shape
    n = w.shape[-1]
    tm, tn = _tile(t, 1024), _tile(n, 1280)
    return pl.pallas_call(
        _in_proj_kernel,
        grid=(t // tm, n // tn),
        in_specs=[
            pl.BlockSpec((tm, d), lambda i, j: (i, 0)),
            pl.BlockSpec((1, d), lambda i, j: (0, 0)),
            pl.BlockSpec((None, d, tn), lambda i, j: (layer, 0, j)),
        ],
        out_specs=pl.BlockSpec((tm, tn), lambda i, j: (i, j)),
        out_shape=jax.ShapeDtypeStruct((t, n), BF16),
        scratch_shapes=[pltpu.VMEM((tm, d), BF16)],
        compiler_params=_params("parallel", "arbitrary"),
        name="in_proj",
    )(x, g.reshape(1, d), w)


def _rope_tables(n):
    rows = n // GRID_W
    row = jnp.repeat(jnp.arange(rows, dtype=F32), GRID_W)
    col = jnp.tile(jnp.arange(GRID_W, dtype=F32), rows)
    axis_dims = HEAD_DIM // 2
    inv = ROPE_THETA ** (-jnp.arange(0, axis_dims, 2, dtype=F32) / axis_dims)
    ang = jnp.concatenate([row[:, None] * inv, col[:, None] * inv], axis=-1)
    cos, sin = jnp.cos(ang), jnp.sin(ang)
    cos_l = jnp.repeat(cos, 2, axis=-1)
    sin_l = jnp.stack([-sin, sin], axis=-1).reshape(n, HEAD_DIM)
    return cos_l, sin_l


def _attn_prep_kernel(q_ref, k_ref, v_ref, cos_ref, sin_ref, qg_ref, kg_ref, qk_ref, vt_ref):
    cos = cos_ref[...]
    sin = sin_ref[...]
    lane = lax.broadcasted_iota(jnp.int32, cos.shape, 1)
    even = (lane & 1) == 0

    def norm_rope(x, g):
        y = _rms(x.astype(F32), g)
        partner = jnp.where(even, pltpu.roll(y, HEAD_DIM - 1, 1), pltpu.roll(y, 1, 1))
        return y * cos + partner * sin

    scale = HEAD_DIM ** -0.5 * LOG2_E
    for h in range(N_Q_HEADS):
        sl = slice(h * HEAD_DIM, (h + 1) * HEAD_DIM)
        qk_ref[:, sl] = (norm_rope(q_ref[:, sl], qg_ref[...]) * scale).astype(qk_ref.dtype)
    for h in range(N_KV_HEADS):
        sl = slice(h * HEAD_DIM, (h + 1) * HEAD_DIM)
        so = slice(ATTN_W + h * HEAD_DIM, ATTN_W + (h + 1) * HEAD_DIM)
        qk_ref[:, so] = norm_rope(k_ref[:, sl], kg_ref[...]).astype(qk_ref.dtype)
    rows = HEAD_DIM + FLASH_ONES_ROWS
    for h in range(N_KV_HEADS):
        vt_ref[h * rows:h * rows + HEAD_DIM, :] = (
            v_ref[:, h * HEAD_DIM:(h + 1) * HEAD_DIM].astype(F32).T.astype(vt_ref.dtype))
        vt_ref[h * rows + HEAD_DIM:(h + 1) * rows, :] = jnp.ones((FLASH_ONES_ROWS, v_ref.shape[0]), vt_ref.dtype)


def _attn_prep(u, cos_l, sin_l, q_gain, k_gain, n):
    t = u.shape[0]
    tm = _tile(n, 512)
    per_seq = n // tm
    return pl.pallas_call(
        _attn_prep_kernel,
        grid=(t // tm,),
        in_specs=[
            pl.BlockSpec((tm, ATTN_W), lambda i: (i, OFF_Q // ATTN_W)),
            pl.BlockSpec((tm, KV_W), lambda i: (i, OFF_K // KV_W)),
            pl.BlockSpec((tm, KV_W), lambda i: (i, OFF_V // KV_W)),
            pl.BlockSpec((tm, HEAD_DIM), lambda i: (i % per_seq, 0)),
            pl.BlockSpec((tm, HEAD_DIM), lambda i: (i % per_seq, 0)),
            pl.BlockSpec((1, HEAD_DIM), lambda i: (0, 0)),
            pl.BlockSpec((1, HEAD_DIM), lambda i: (0, 0)),
        ],
        out_specs=[
            pl.BlockSpec((tm, ATTN_W + KV_W), lambda i: (i, 0)),
            pl.BlockSpec((N_KV_HEADS * (HEAD_DIM + FLASH_ONES_ROWS), tm), lambda i: (0, i)),
        ],
        out_shape=[
            jax.ShapeDtypeStruct((t, ATTN_W + KV_W), BF16),
            jax.ShapeDtypeStruct((N_KV_HEADS * (HEAD_DIM + FLASH_ONES_ROWS), t), BF16),
        ],
        compiler_params=_params("parallel"),
        name="attn_prep",
    )(u, u, u, cos_l, sin_l, q_gain.reshape(1, HEAD_DIM), k_gain.reshape(1, HEAD_DIM))


def _flash_kernel(shift_ref, q_ref, k_ref, vt_ref, o_ref, m_ref, acc_ref, *, tk, sub, fixed_shift):
    ki = pl.program_id(3)

    @pl.when(ki == 0)
    def _():
        m_ref[...] = jnp.full_like(m_ref, -jnp.inf)
        acc_ref[...] = jnp.zeros_like(acc_ref)

    stages = [(s0, g) for s0 in range(0, tk, sub) for g in range(Q_GROUP)]

    def scores(stage):
        s0, g = stage
        return _dot_nt(k_ref[s0:s0 + sub, :], q_ref[:, g * HEAD_DIM:(g + 1) * HEAD_DIM])

    shift = shift_ref[0, 0]
    partial = [None] * Q_GROUP

    pending = [scores(st) for st in stages[:FLASH_LOOKAHEAD]]
    for i, (s0, g) in enumerate(stages):
        s = pending.pop(0)
        if i + FLASH_LOOKAHEAD < len(stages):
            pending.append(scores(stages[i + FLASH_LOOKAHEAD]))
        vt1 = vt_ref[:, s0:s0 + sub]
        if fixed_shift:
            pv = _dot(vt1, jnp.exp2(s - shift).astype(BF16))
            partial[g] = pv if partial[g] is None else partial[g] + pv
        else:
            m_prev = m_ref[g]
            m_new = jnp.maximum(m_prev, jnp.max(s, axis=0, keepdims=True))
            alpha = jnp.exp2(m_prev - m_new)
            p = jnp.exp2((s - m_new).astype(BF16))
            acc_ref[g] = alpha * acc_ref[g] + _dot(vt1, p)
            m_ref[g] = m_new
    if fixed_shift:
        for g in range(Q_GROUP):
            acc_ref[g] += partial[g]

    @pl.when(ki == pl.num_programs(3) - 1)
    def _():
        for g in range(Q_GROUP):
            out = acc_ref[g, 0:HEAD_DIM, :] / acc_ref[g, HEAD_DIM:HEAD_DIM + 1, :]
            o_ref[:, g * HEAD_DIM:(g + 1) * HEAD_DIM] = out.T.astype(o_ref.dtype)


def _flash(qk, vt, q_gain, k_gain, batch, n):
    t = qk.shape[0]
    tq, tk = _tile(n, FLASH_TQ), _tile(n, FLASH_TK)
    sub = _tile(tk, FLASH_SUB)
    nq, nk = n // tq, n // tk
    gw = Q_GROUP * HEAD_DIM
    bound = (HEAD_DIM ** 0.5 * LOG2_E) * jnp.max(jnp.abs(q_gain)) * jnp.max(jnp.abs(k_gain))
    shift = bound.astype(F32).reshape(1, 1)

    def call(fixed_shift):
        return pl.pallas_call(
            functools.partial(_flash_kernel, tk=tk, sub=sub, fixed_shift=fixed_shift),
            grid=(batch, N_KV_HEADS, nq, nk),
            in_specs=[
                pl.BlockSpec(memory_space=pltpu.SMEM),
                pl.BlockSpec((tq, gw), lambda b, h, qi, ki: (b * nq + qi, h)),
                pl.BlockSpec((tk, HEAD_DIM), lambda b, h, qi, ki: (b * nk + ki, N_Q_HEADS + h)),
                pl.BlockSpec((HEAD_DIM + FLASH_ONES_ROWS, tk), lambda b, h, qi, ki: (h, b * nk + ki)),
            ],
            out_specs=pl.BlockSpec((tq, gw), lambda b, h, qi, ki: (b * nq + qi, h)),
            out_shape=jax.ShapeDtypeStruct((t, ATTN_W), BF16),
            scratch_shapes=[
                pltpu.VMEM((Q_GROUP, 1, tq), F32),
                pltpu.VMEM((Q_GROUP, HEAD_DIM + FLASH_ONES_ROWS, tq), F32),
            ],
            compiler_params=_params("parallel", "parallel", "parallel", "arbitrary"),
            name="flash_attn_fixed" if fixed_shift else "flash_attn",
        )(shift, qk, qk, vt)

    return lax.cond(2.0 * bound <= FLASH_FIXED_SHIFT_MAX_RANGE, lambda: call(True), lambda: call(False))


def _conv_kernel(a_ref, g_ref, ap_ref, gp_ref, an_ref, gn_ref, dw_ref, dwb_ref, lng_ref, lnb_ref, pw_ref,
                 o_ref, h_ref, sh_ref, *, tm, per_seq):
    i = pl.program_id(0)
    pos = i % per_seq

    def glu(a, g):
        return a.astype(F32) * _sigmoid(g.astype(F32))

    prev = jnp.where(pos == 0, 0.0, glu(ap_ref[...], gp_ref[...]))
    nxt = jnp.where(pos == per_seq - 1, 0.0, glu(an_ref[...], gn_ref[...]))
    h_ref[0:CONV_HALO, :] = prev
    h_ref[CONV_HALO:CONV_HALO + tm, :] = glu(a_ref[...], g_ref[...])
    h_ref[CONV_HALO + tm:, :] = nxt

    base = CONV_HALO - CONV_K // 2
    span = tm + 2 * CONV_HALO - SUBLANES
    for b in range(SUBLANES):
        sh_ref[b] = h_ref[b:b + span, :]
    acc = jnp.zeros((tm, CONV_W), F32) + dwb_ref[...]
    for j in range(CONV_K):
        a, b = divmod(base + j, SUBLANES)
        acc = acc + dw_ref[j:j + 1, :] * sh_ref[b, a * SUBLANES:a * SUBLANES + tm, :]
    mu = jnp.mean(acc, axis=-1, keepdims=True)
    cen = acc - mu
    var = jnp.mean(cen * cen, axis=-1, keepdims=True)
    y = cen * lax.rsqrt(var + NORM_EPS) * lng_ref[...] + lnb_ref[...]
    o_ref[...] = _dot(_silu(y).astype(BF16), pw_ref[...]).astype(o_ref.dtype)


def _conv(u, dw, dwb, lng, lnb, pw, layer, n):
    t = u.shape[0]
    tm = _tile(n, 512)
    per_seq = n // tm
    hb = tm // CONV_HALO
    last = t // CONV_HALO - 1
    ca, cg = OFF_CA // CONV_W, OFF_CG // CONV_W
    vec = lambda v: v.reshape(1, CONV_W)
    row = pl.BlockSpec((1, CONV_W), lambda i: (0, 0))
    return pl.pallas_call(
        functools.partial(_conv_kernel, tm=tm, per_seq=per_seq),
        grid=(t // tm,),
        in_specs=[
            pl.BlockSpec((tm, CONV_W), lambda i: (i, ca)),
            pl.BlockSpec((tm, CONV_W), lambda i: (i, cg)),
            pl.BlockSpec((CONV_HALO, CONV_W), lambda i: (jnp.maximum(i * hb - 1, 0), ca)),
            pl.BlockSpec((CONV_HALO, CONV_W), lambda i: (jnp.maximum(i * hb - 1, 0), cg)),
            pl.BlockSpec((CONV_HALO, CONV_W), lambda i: (jnp.minimum((i + 1) * hb, last), ca)),
            pl.BlockSpec((CONV_HALO, CONV_W), lambda i: (jnp.minimum((i + 1) * hb, last), cg)),
            pl.BlockSpec((CONV_K, CONV_W), lambda i: (0, 0)),
            row, row, row,
            pl.BlockSpec((None, CONV_W, CONV_W), lambda i: (layer, 0, 0)),
        ],
        out_specs=pl.BlockSpec((tm, CONV_W), lambda i: (i, 0)),
        out_shape=jax.ShapeDtypeStruct((t, CONV_W), BF16),
        scratch_shapes=[
            pltpu.VMEM((tm + 2 * CONV_HALO, CONV_W), F32),
            pltpu.VMEM((SUBLANES, tm + 2 * CONV_HALO - SUBLANES, CONV_W), F32),
        ],
        compiler_params=_params("parallel"),
        name="conv_mixer",
    )(u, u, u, u, u, u, dw, vec(dwb), vec(lng), vec(lnb), pw)


def _hgrn_tables(reverse):
    c = HGRN_CHUNK
    levels = []
    h = c // 2
    while h >= 1:
        levels.append(h)
        h //= 2
    r = np.arange(c)
    col = r[None, :]
    seg = np.zeros((len(levels) + 2, c, c), np.float32)
    mask = np.zeros((len(levels) + 1, c, c), np.float32)
    for li, h in enumerate(levels):
        blk = r // (2 * h)
        upper = (r % (2 * h)) >= h
        b = (blk * 2 * h + h - 1)[:, None]
        rr = r[:, None]
        seg[li] = np.where(upper[:, None], (col > b) & (col <= rr), (col > rr) & (col <= b))
        mask[li] = upper[:, None] & (~upper)[None, :] & (blk[:, None] == blk[None, :])
    seg[-2] = col <= r[:, None]
    seg[-1] = col > r[:, None]
    mask[-1] = np.eye(c)
    if reverse:
        seg = seg[:, ::-1, ::-1]
        mask = mask[:, ::-1, ::-1]
    seg = seg.reshape(-1, c)
    seg2 = np.concatenate([seg, seg], axis=1)
    return (jnp.asarray(seg2, dtype=BF16), jnp.asarray(mask, dtype=F32), len(levels))


def _hgrn_kernel(*refs, reverse, final, n_levels, n_chunks):
    if final:
        hq_ref, hf_ref, hi_ref, lb_ref, seg_ref, mask_ref, oprev_ref, hg_ref, gn_ref, o_ref, st_ref = refs
    else:
        hq_ref, hf_ref, hi_ref, lb_ref, seg_ref, mask_ref, o_ref, st_ref = refs
    c = HGRN_CHUNK

    @pl.when(pl.program_id(1) == 0)
    def _():
        st_ref[...] = jnp.zeros_like(st_ref)

    lb = lb_ref[...]
    seg = seg_ref[...]
    total_row = 0 if reverse else c - 1

    def prep(j):
        rows = slice(j * c, (j + 1) * c)
        f = lb + (1.0 - lb) * _sigmoid(hf_ref[rows, :].astype(F32))
        log2f = jnp.log(f) * LOG2_E
        g_hi = log2f.astype(BF16)
        g_lo = (log2f - g_hi.astype(F32)).astype(BF16)
        e_all = jnp.exp2(_dot(seg, jnp.concatenate([g_hi, g_lo], axis=0)))
        decay = e_all[n_levels * c + total_row:n_levels * c + total_row + 1, :]
        return _silu(hq_ref[rows, :].astype(F32)).astype(BF16), (1.0 - f).astype(BF16), e_all.astype(BF16), decay

    def intra(prepped):
        qq, kk, e_all, _ = prepped
        mats = []
        for hd in range(HGRN_HEADS):
            sl = slice(hd * HGRN_D, (hd + 1) * HGRN_D)
            qh, kh = qq[:, sl], kk[:, sl]
            a = mask_ref[n_levels] * _dot_nt(qh, kh)
            for l in range(n_levels):
                el = e_all[l * c:(l + 1) * c, sl]
                a = a + mask_ref[l] * _dot_nt(qh * el, kh * el)
            mats.append(a.astype(BF16))
        return mats

    def outputs(j, prepped, mats):
        qq, kk, e_all, decay = prepped
        rows = slice(j * c, (j + 1) * c)
        if final:
            gate = _silu(hg_ref[rows, :].astype(F32)) * gn_ref[...]
        for hd in range(HGRN_HEADS):
            sl = slice(hd * HGRN_D, (hd + 1) * HGRN_D)
            qh, kh, vh = qq[:, sl], kk[:, sl], hi_ref[rows, sl]
            e_pre = e_all[n_levels * c:(n_levels + 1) * c, sl]
            e_suf = e_all[(n_levels + 1) * c:(n_levels + 2) * c, sl]
            st = st_ref[hd]
            o = _dot(mats[hd], vh) + _dot_nt(qh * e_pre, st.astype(BF16))
            v_t = vh.astype(F32).T.astype(BF16)
            st_ref[hd] = decay[:, sl] * st + _dot(v_t, kh * e_suf)
            if final:
                o = o + oprev_ref[rows, sl]
                o = o * lax.rsqrt(jnp.mean(o * o, axis=-1, keepdims=True) + NORM_EPS)
                o_ref[rows, sl] = (o * gate[:, sl]).astype(o_ref.dtype)
            else:
                o_ref[rows, sl] = o

    order = list(range(n_chunks))[::-1] if reverse else list(range(n_chunks))
    prepped, mats = {}, {}
    for step in range(n_chunks + 2):
        if step < n_chunks:
            prepped[step] = prep(order[step])
        if 0 <= step - 1 < n_chunks:
            mats[step - 1] = intra(prepped[step - 1])
        if 0 <= step - 2 < n_chunks:
            outputs(order[step - 2], prepped.pop(step - 2), mats.pop(step - 2))


def _hgrn_pass(u, lb, batch, n, *, reverse, o_prev=None, gn=None):
    t = u.shape[0]
    n_chunks = _tile(n // HGRN_CHUNK, HGRN_CHUNKS_PER_STEP)
    c = n_chunks * HGRN_CHUNK
    nc = n // c
    final = o_prev is not None
    seg, mask, n_levels = _hgrn_tables(reverse)

    def rows(b, ci):
        return b * nc + (nc - 1 - ci if reverse else ci)

    def col_spec(off):
        return pl.BlockSpec((c, HGRN_W), lambda b, ci: (rows(b, ci), off // HGRN_W))

    const2 = lambda b, ci: (0, 0)
    in_specs = [
        col_spec(OFF_HQ),
        col_spec(OFF_FB if reverse else OFF_FF),
        col_spec(OFF_HI),
        pl.BlockSpec((1, HGRN_W), const2),
        pl.BlockSpec(seg.shape, const2),
        pl.BlockSpec(mask.shape, lambda b, ci: (0, 0, 0)),
    ]
    args = [u, u, u, lb.reshape(1, HGRN_W), seg, mask]
    if final:
        in_specs += [
            pl.BlockSpec((c, HGRN_W), lambda b, ci: (rows(b, ci), 0)),
            col_spec(OFF_HG),
            pl.BlockSpec((1, HGRN_W), const2),
        ]
        args += [o_prev, u, gn.reshape(1, HGRN_W)]
    return pl.pallas_call(
        functools.partial(_hgrn_kernel, reverse=reverse, final=final, n_levels=n_levels, n_chunks=n_chunks),
        grid=(batch, nc),
        in_specs=in_specs,
        out_specs=pl.BlockSpec((c, HGRN_W), lambda b, ci: (rows(b, ci), 0)),
        out_shape=jax.ShapeDtypeStruct((t, HGRN_W), BF16 if final else F32),
        scratch_shapes=[pltpu.VMEM((HGRN_HEADS, HGRN_D, HGRN_D), F32)],
        compiler_params=_params("parallel", "arbitrary"),
        name="hgrn_bwd" if reverse else "hgrn_fwd",
    )(*args)


def _out_proj_kernel(a_ref, c_ref, h_ref, x_ref, w_ref, g_ref, gnext_ref, o_ref, xn_ref):
    halves = _row_halves(x_ref.shape[0])
    mixes = []
    for rows in halves:
        mix = _dot(a_ref[rows, :], w_ref[0:ATTN_W, :])
        mix = mix + _dot(c_ref[rows, :], w_ref[ATTN_W:ATTN_W + CONV_W, :])
        mixes.append(mix + _dot(h_ref[rows, :], w_ref[ATTN_W + CONV_W:, :]))
    for rows, mix in zip(halves, mixes):
        x = x_ref[rows, :] + _rms(mix, g_ref[...])
        o_ref[rows, :] = x
        xn_ref[rows, :] = _rms(x, gnext_ref[...]).astype(xn_ref.dtype)


def _out_proj(a, cv, hg, x, w, layer, g, g_next):
    t, d = x.shape
    tm = _tile(t, 512)
    mixw = w.shape[1]
    return pl.pallas_call(
        _out_proj_kernel,
        grid=(t // tm,),
        in_specs=[
            pl.BlockSpec((tm, ATTN_W), lambda i: (i, 0)),
            pl.BlockSpec((tm, CONV_W), lambda i: (i, 0)),
            pl.BlockSpec((tm, HGRN_W), lambda i: (i, 0)),
            pl.BlockSpec((tm, d), lambda i: (i, 0)),
            pl.BlockSpec((None, mixw, d), lambda i: (layer, 0, 0)),
            pl.BlockSpec((1, d), lambda i: (0, 0)),
            pl.BlockSpec((1, d), lambda i: (0, 0)),
        ],
        out_specs=[pl.BlockSpec((tm, d), lambda i: (i, 0)), pl.BlockSpec((tm, d), lambda i: (i, 0))],
        out_shape=[jax.ShapeDtypeStruct((t, d), F32), jax.ShapeDtypeStruct((t, d), BF16)],
        compiler_params=_params("parallel"),
        name="out_proj",
    )(a, cv, hg, x, w, g.reshape(1, d), g_next.reshape(1, d))


def _ffn_kernel(xn_ref, wg_ref, wu_ref, wo_ref, gpost_ref, o_ref, acc_ref):
    j = pl.program_id(1)

    @pl.when(j == 0)
    def _():
        acc_ref[...] = jnp.zeros_like(acc_ref)

    xn = xn_ref[...]
    hidden = _silu(_dot(xn, wg_ref[...])) * _dot(xn, wu_ref[...])
    acc_ref[...] += _dot(hidden.astype(BF16), wo_ref[...])

    @pl.when(j == pl.num_programs(1) - 1)
    def _():
        o_ref[...] = _rms(acc_ref[...], gpost_ref[...]).astype(o_ref.dtype)


def _ffn(xn, w_in, w_out, layer, gpost):
    t, d = xn.shape
    f = w_out.shape[1]
    tm, th = _tile(t, 1024), _tile(f, 512)
    nh = f // th
    return pl.pallas_call(
        _ffn_kernel,
        grid=(t // tm, nh),
        in_specs=[
            pl.BlockSpec((tm, d), lambda i, j: (i, 0)),
            pl.BlockSpec((None, d, th), lambda i, j: (layer, 0, j)),
            pl.BlockSpec((None, d, th), lambda i, j: (layer, 0, nh + j)),
            pl.BlockSpec((None, th, d), lambda i, j: (layer, j, 0)),
            pl.BlockSpec((1, d), lambda i, j: (0, 0)),
        ],
        out_specs=pl.BlockSpec((tm, d), lambda i, j: (i, 0)),
        out_shape=jax.ShapeDtypeStruct((t, d), BF16),
        scratch_shapes=[pltpu.VMEM((tm, d), F32)],
        compiler_params=_params("parallel", "arbitrary"),
        name="ffn",
    )(xn, w_in, w_in, w_out, gpost.reshape(1, d))


def _ple_kernel(x_ref, y_ref, p_ref, wg_ref, wp_ref, g_ref, o_ref):
    halves = _row_halves(x_ref.shape[0])
    parts = []
    for rows in halves:
        x = x_ref[rows, :] + y_ref[rows, :].astype(F32)
        parts.append((x, _dot(x.astype(BF16), wg_ref[...]), _dot(p_ref[rows, :].astype(BF16), wp_ref[...])))
    for rows, (x, gate_logit, proj) in zip(halves, parts):
        o_ref[rows, :] = x + _rms(proj * _sigmoid(gate_logit), g_ref[...])


def _ple(x, y, p, layer, w_gate, w_proj, g):
    t, d = x.shape
    pd = p.shape[-1]
    tm = _tile(t, 512)
    return pl.pallas_call(
        _ple_kernel,
        grid=(t // tm,),
        in_specs=[
            pl.BlockSpec((tm, d), lambda i: (i, 0)),
            pl.BlockSpec((tm, d), lambda i: (i, 0)),
            pl.BlockSpec((None, tm, pd), lambda i: (layer, i, 0)),
            pl.BlockSpec((None, d, d), lambda i: (layer, 0, 0)),
            pl.BlockSpec((None, pd, d), lambda i: (layer, 0, 0)),
            pl.BlockSpec((1, d), lambda i: (0, 0)),
        ],
        out_specs=pl.BlockSpec((tm, d), lambda i: (i, 0)),
        out_shape=jax.ShapeDtypeStruct((t, d), F32),
        compiler_params=_params("parallel"),
        name="ple",
    )(x, y, p, w_gate, w_proj, g.reshape(1, d))


def _trunk(x, p, lb, layers, mats):
    depth = p.shape[0]
    batch, n, d = x.shape
    t = batch * n
    x = x.reshape(t, d)
    p = p.reshape(depth, t, p.shape[-1])
    cos_l, sin_l = _rope_tables(n)
    for l, w in enumerate(layers):
        u = _in_proj(x, w["norm_mix_pre"], mats["w_in"], l)
        qk_r, v_t = _attn_prep(u, cos_l, sin_l, w["q_norm"], w["k_norm"], n)
        a_out = _flash(qk_r, v_t, w["q_norm"], w["k_norm"], batch, n)
        c_out = _conv(u, w["conv_dw"], w["conv_dw_b"], w["conv_ln_g"], w["conv_ln_b"], mats["conv_pw"], l, n)
        o_fwd = _hgrn_pass(u, lb[l, 0], batch, n, reverse=False)
        h_out = _hgrn_pass(u, lb[l, 1], batch, n, reverse=True, o_prev=o_fwd, gn=w["hgrn_gn"])
        x, xn = _out_proj(a_out, c_out, h_out, x, mats["w_out"], l, w["norm_mix_post"], w["norm_ffn_pre"])
        y = _ffn(xn, mats["w_ffn_in"], mats["w_ffn_out"], l, w["norm_ffn_post"])
        x = _ple(x, y, p, l, mats["w_ple_gate"], mats["w_ple_proj"], w["ple_norm"])
    return x.reshape(batch, n, d)


def kernel(x_prompt, x_sample, p_prompt, p_sample, norm_mix_pre, norm_mix_post, w_in, q_norm, k_norm, conv_dw, conv_dw_b, conv_ln_g, conv_ln_b, conv_pw, hgrn_lb, hgrn_gn, w_out, norm_ffn_pre, norm_ffn_post, w_ffn_in, w_ffn_out, w_ple_gate, w_ple_proj, ple_norm):
    sm = jax.nn.softmax(hgrn_lb.astype(F32), axis=0)
    lb = jnp.cumsum(sm, axis=0) - sm[0]
    vectors = dict(
        norm_mix_pre=norm_mix_pre, norm_mix_post=norm_mix_post, q_norm=q_norm, k_norm=k_norm,
        conv_dw=conv_dw, conv_dw_b=conv_dw_b, conv_ln_g=conv_ln_g, conv_ln_b=conv_ln_b,
        hgrn_gn=hgrn_gn, norm_ffn_pre=norm_ffn_pre, norm_ffn_post=norm_ffn_post, ple_norm=ple_norm,
    )
    matrices = dict(
        w_in=w_in, conv_pw=conv_pw, w_out=w_out, w_ffn_in=w_ffn_in, w_ffn_out=w_ffn_out,
        w_ple_gate=w_ple_gate, w_ple_proj=w_ple_proj,
    )
    mats = {k: v.astype(BF16) for k, v in matrices.items()}
    layers = [{k: v[l] for k, v in vectors.items()} for l in range(p_prompt.shape[0])]
    y_prompt = _trunk(x_prompt, p_prompt, lb, layers, mats)
    y_sample = _trunk(x_sample, p_sample, lb, layers, mats)
    return (y_prompt, y_sample)
```

```python
import functools

import numpy as np
import jax
import jax.numpy as jnp
from jax import lax
from jax.experimental import pallas as pl
from jax.experimental.pallas import tpu as pltpu

F32 = jnp.float32
BF16 = jnp.bfloat16

NORM_EPS = 1e-6
ROPE_THETA = 10000.0
LOG2_E = 1.4426950408889634
GRID_W = 64
HEAD_DIM = 128
N_Q_HEADS = 8
N_KV_HEADS = 2
Q_GROUP = N_Q_HEADS // N_KV_HEADS
ATTN_W = N_Q_HEADS * HEAD_DIM
KV_W = N_KV_HEADS * HEAD_DIM
CONV_W = 512
CONV_K = 31
CONV_HALO = 16
SUBLANES = 8
HGRN_W = 512
HGRN_D = 128
HGRN_HEADS = HGRN_W // HGRN_D
HGRN_CHUNK = 128
HGRN_CHUNKS_PER_STEP = 8
FLASH_ONES_ROWS = 16
FLASH_LOOKAHEAD = 3
FLASH_TQ = 512
FLASH_TK = 8192
FLASH_SUB = 512
FLASH_FIXED_SHIFT_MAX_RANGE = 96.0

OFF_Q = 0
OFF_K = OFF_Q + ATTN_W
OFF_V = OFF_K + KV_W
OFF_CA = OFF_V + KV_W
OFF_CG = OFF_CA + CONV_W
OFF_HQ = OFF_CG + CONV_W
OFF_FF = OFF_HQ + HGRN_W
OFF_FB = OFF_FF + HGRN_W
OFF_HI = OFF_FB + HGRN_W
OFF_HG = OFF_HI + HGRN_W
IN_COLS = OFF_HG + HGRN_W

V7X_VMEM_LIMIT_BYTES = 56 * 1024 * 1024


def _params(*semantics):
    return pltpu.CompilerParams(dimension_semantics=semantics, vmem_limit_bytes=V7X_VMEM_LIMIT_BYTES)


def _tile(dim, pref):
    t = min(dim, pref)
    assert dim % t == 0, (dim, pref)
    return t


def _row_halves(rows):
    half = rows // 2
    return [slice(0, half), slice(half, rows)] if half % 16 == 0 else [slice(0, rows)]


def _rms(x, g):
    return x * lax.rsqrt(jnp.mean(x * x, axis=-1, keepdims=True) + NORM_EPS) * g


def _sigmoid(x):
    return 1.0 / (1.0 + jnp.exp(-x))


def _silu(x):
    return x * _sigmoid(x)


def _dot(a, b):
    return jnp.dot(a, b, preferred_element_type=F32)


def _dot_nt(a, b):
    return lax.dot_general(a, b, (((1,), (1,)), ((), ())), preferred_element_type=F32)


def _in_proj_kernel(x_ref, g_ref, w_ref, o_ref, xn_ref):
    @pl.when(pl.program_id(1) == 0)
    def _():
        xn_ref[...] = _rms(x_ref[...], g_ref[...]).astype(BF16)

    o_ref[...] = _dot(xn_ref[...], w_ref[...]).astype(o_ref.dtype)


def _in_proj(x, g, w, layer):
    t, d = x.shape
    n = w.shape[-1]
    tm, tn = _tile(t, 1024), _tile(n, 1280)
    return pl.pallas_call(
        _in_proj_kernel,
        grid=(t // tm, n // tn),
        in_specs=[
            pl.BlockSpec((tm, d), lambda i, j: (i, 0)),
            pl.BlockSpec((1, d), lambda i, j: (0, 0)),
            pl.BlockSpec((None, d, tn), lambda i, j: (layer, 0, j)),
        ],
        out_specs=pl.BlockSpec((tm, tn), lambda i, j: (i, j)),
        out_shape=jax.ShapeDtypeStruct((t, n), BF16),
        scratch_shapes=[pltpu.VMEM((tm, d), BF16)],
        compiler_params=_params("parallel", "arbitrary"),
        name="in_proj",
    )(x, g.reshape(1, d), w)


def _rope_tables(n):
    rows = n // GRID_W
    row = jnp.repeat(jnp.arange(rows, dtype=F32), GRID_W)
    col = jnp.tile(jnp.arange(GRID_W, dtype=F32), rows)
    axis_dims = HEAD_DIM // 2
    inv = ROPE_THETA ** (-jnp.arange(0, axis_dims, 2, dtype=F32) / axis_dims)
    ang = jnp.concatenate([row[:, None] * inv, col[:, None] * inv], axis=-1)
    cos, sin = jnp.cos(ang), jnp.sin(ang)
    cos_l = jnp.repeat(cos, 2, axis=-1)
    sin_l = jnp.stack([-sin, sin], axis=-1).reshape(n, HEAD_DIM)
    return cos_l, sin_l


def _attn_prep_kernel(q_ref, k_ref, v_ref, cos_ref, sin_ref, qg_ref, kg_ref, qk_ref, vt_ref):
    cos = cos_ref[...]
    sin = sin_ref[...]
    lane = lax.broadcasted_iota(jnp.int32, cos.shape, 1)
    even = (lane & 1) == 0

    def norm_rope(x, g):
        y = _rms(x.astype(F32), g)
        partner = jnp.where(even, pltpu.roll(y, HEAD_DIM - 1, 1), pltpu.roll(y, 1, 1))
        return y * cos + partner * sin

    scale = HEAD_DIM ** -0.5 * LOG2_E
    for h in range(N_Q_HEADS):
        sl = slice(h * HEAD_DIM, (h + 1) * HEAD_DIM)
        qk_ref[:, sl] = (norm_rope(q_ref[:, sl], qg_ref[...]) * scale).astype(qk_ref.dtype)
    for h in range(N_KV_HEADS):
        sl = slice(h * HEAD_DIM, (h + 1) * HEAD_DIM)
        so = slice(ATTN_W + h * HEAD_DIM, ATTN_W + (h + 1) * HEAD_DIM)
        qk_ref[:, so] = norm_rope(k_ref[:, sl], kg_ref[...]).astype(qk_ref.dtype)
    rows = HEAD_DIM + FLASH_ONES_ROWS
    for h in range(N_KV_HEADS):
        vt_ref[h * rows:h * rows + HEAD_DIM, :] = (
            v_ref[:, h * HEAD_DIM:(h + 1) * HEAD_DIM].astype(F32).T.astype(vt_ref.dtype))
        vt_ref[h * rows + HEAD_DIM:(h + 1) * rows, :] = jnp.ones((FLASH_ONES_ROWS, v_ref.shape[0]), vt_ref.dtype)


def _attn_prep(u, cos_l, sin_l, q_gain, k_gain, n):
    t = u.shape[0]
    tm = _tile(n, 512)
    per_seq = n // tm
    return pl.pallas_call(
        _attn_prep_kernel,
        grid=(t // tm,),
        in_specs=[
            pl.BlockSpec((tm, ATTN_W), lambda i: (i, OFF_Q // ATTN_W)),
            pl.BlockSpec((tm, KV_W), lambda i: (i, OFF_K // KV_W)),
            pl.BlockSpec((tm, KV_W), lambda i: (i, OFF_V // KV_W)),
            pl.BlockSpec((tm, HEAD_DIM), lambda i: (i % per_seq, 0)),
            pl.BlockSpec((tm, HEAD_DIM), lambda i: (i % per_seq, 0)),
            pl.BlockSpec((1, HEAD_DIM), lambda i: (0, 0)),
            pl.BlockSpec((1, HEAD_DIM), lambda i: (0, 0)),
        ],
        out_specs=[
            pl.BlockSpec((tm, ATTN_W + KV_W), lambda i: (i, 0)),
            pl.BlockSpec((N_KV_HEADS * (HEAD_DIM + FLASH_ONES_ROWS), tm), lambda i: (0, i)),
        ],
        out_shape=[
            jax.ShapeDtypeStruct((t, ATTN_W + KV_W), BF16),
            jax.ShapeDtypeStruct((N_KV_HEADS * (HEAD_DIM + FLASH_ONES_ROWS), t), BF16),
        ],
        compiler_params=_params("parallel"),
        name="attn_prep",
    )(u, u, u, cos_l, sin_l, q_gain.reshape(1, HEAD_DIM), k_gain.reshape(1, HEAD_DIM))


def _flash_kernel(shift_ref, q_ref, k_ref, vt_ref, o_ref, m_ref, acc_ref, *, tk, sub, fixed_shift):
    ki = pl.program_id(3)

    @pl.when(ki == 0)
    def _():
        m_ref[...] = jnp.full_like(m_ref, -jnp.inf)
        acc_ref[...] = jnp.zeros_like(acc_ref)

    stages = [(s0, g) for s0 in range(0, tk, sub) for g in range(Q_GROUP)]

    def scores(stage):
        s0, g = stage
        return _dot_nt(k_ref[s0:s0 + sub, :], q_ref[:, g * HEAD_DIM:(g + 1) * HEAD_DIM])

    shift = shift_ref[0, 0]
    partial = [None] * Q_GROUP

    pending = [scores(st) for st in stages[:FLASH_LOOKAHEAD]]
    for i, (s0, g) in enumerate(stages):
        s = pending.pop(0)
        if i + FLASH_LOOKAHEAD < len(stages):
            pending.append(scores(stages[i + FLASH_LOOKAHEAD]))
        vt1 = vt_ref[:, s0:s0 + sub]
        if fixed_shift:
            pv = _dot(vt1, jnp.exp2(s - shift).astype(BF16))
            partial[g] = pv if partial[g] is None else partial[g] + pv
        else:
            m_prev = m_ref[g]
            m_new = jnp.maximum(m_prev, jnp.max(s, axis=0, keepdims=True))
            alpha = jnp.exp2(m_prev - m_new)
            p = jnp.exp2((s - m_new).astype(BF16))
            acc_ref[g] = alpha * acc_ref[g] + _dot(vt1, p)
            m_ref[g] = m_new
    if fixed_shift:
        for g in range(Q_GROUP):
            acc_ref[g] += partial[g]

    @pl.when(ki == pl.num_programs(3) - 1)
    def _():
        for g in range(Q_GROUP):
            out = acc_ref[g, 0:HEAD_DIM, :] / acc_ref[g, HEAD_DIM:HEAD_DIM + 1, :]
            o_ref[:, g * HEAD_DIM:(g + 1) * HEAD_DIM] = out.T.astype(o_ref.dtype)


def _flash(qk, vt, q_gain, k_gain, batch, n):
    t = qk.shape[0]
    tq, tk = _tile(n, FLASH_TQ), _tile(n, FLASH_TK)
    sub = _tile(tk, FLASH_SUB)
    nq, nk = n // tq, n // tk
    gw = Q_GROUP * HEAD_DIM
    bound = (HEAD_DIM ** 0.5 * LOG2_E) * jnp.max(jnp.abs(q_gain)) * jnp.max(jnp.abs(k_gain))
    shift = bound.astype(F32).reshape(1, 1)

    def call(fixed_shift):
        return pl.pallas_call(
            functools.partial(_flash_kernel, tk=tk, sub=sub, fixed_shift=fixed_shift),
            grid=(batch, N_KV_HEADS, nq, nk),
            in_specs=[
                pl.BlockSpec(memory_space=pltpu.SMEM),
                pl.BlockSpec((tq, gw), lambda b, h, qi, ki: (b * nq + qi, h)),
                pl.BlockSpec((tk, HEAD_DIM), lambda b, h, qi, ki: (b * nk + ki, N_Q_HEADS + h)),
                pl.BlockSpec((HEAD_DIM + FLASH_ONES_ROWS, tk), lambda b, h, qi, ki: (h, b * nk + ki)),
            ],
            out_specs=pl.BlockSpec((tq, gw), lambda b, h, qi, ki: (b * nq + qi, h)),
            out_shape=jax.ShapeDtypeStruct((t, ATTN_W), BF16),
            scratch_shapes=[
                pltpu.VMEM((Q_GROUP, 1, tq), F32),
                pltpu.VMEM((Q_GROUP, HEAD_DIM + FLASH_ONES_ROWS, tq), F32),
            ],
            compiler_params=_params("parallel", "parallel", "parallel", "arbitrary"),
            name="flash_attn_fixed" if fixed_shift else "flash_attn",
        )(shift, qk, qk, vt)

    return lax.cond(2.0 * bound <= FLASH_FIXED_SHIFT_MAX_RANGE, lambda: call(True), lambda: call(False))


def _conv_kernel(a_ref, g_ref, ap_ref, gp_ref, an_ref, gn_ref, dw_ref, dwb_ref, lng_ref, lnb_ref, pw_ref,
                 o_ref, h_ref, sh_ref, *, tm, per_seq):
    i = pl.program_id(0)
    pos = i % per_seq

    def glu(a, g):
        return a.astype(F32) * _sigmoid(g.astype(F32))

    prev = jnp.where(pos == 0, 0.0, glu(ap_ref[...], gp_ref[...]))
    nxt = jnp.where(pos == per_seq - 1, 0.0, glu(an_ref[...], gn_ref[...]))
    h_ref[0:CONV_HALO, :] = prev
    h_ref[CONV_HALO:CONV_HALO + tm, :] = glu(a_ref[...], g_ref[...])
    h_ref[CONV_HALO + tm:, :] = nxt

    base = CONV_HALO - CONV_K // 2
    span = tm + 2 * CONV_HALO - SUBLANES
    for b in range(SUBLANES):
        sh_ref[b] = h_ref[b:b + span, :]
    acc = jnp.zeros((tm, CONV_W), F32) + dwb_ref[...]
    for j in range(CONV_K):
        a, b = divmod(base + j, SUBLANES)
        acc = acc + dw_ref[j:j + 1, :] * sh_ref[b, a * SUBLANES:a * SUBLANES + tm, :]
    mu = jnp.mean(acc, axis=-1, keepdims=True)
    cen = acc - mu
    var = jnp.mean(cen * cen, axis=-1, keepdims=True)
    y = cen * lax.rsqrt(var + NORM_EPS) * lng_ref[...] + lnb_ref[...]
    o_ref[...] = _dot(_silu(y).astype(BF16), pw_ref[...]).astype(o_ref.dtype)


def _conv(u, dw, dwb, lng, lnb, pw, layer, n):
    t = u.shape[0]
    tm = _tile(n, 1024)
    per_seq = n // tm
    hb = tm // CONV_HALO
    last = t // CONV_HALO - 1
    ca, cg = OFF_CA // CONV_W, OFF_CG // CONV_W
    vec = lambda v: v.reshape(1, CONV_W)
    row = pl.BlockSpec((1, CONV_W), lambda i: (0, 0))
    return pl.pallas_call(
        functools.partial(_conv_kernel, tm=tm, per_seq=per_seq),
        grid=(t // tm,),
        in_specs=[
            pl.BlockSpec((tm, CONV_W), lambda i: (i, ca)),
            pl.BlockSpec((tm, CONV_W), lambda i: (i, cg)),
            pl.BlockSpec((CONV_HALO, CONV_W), lambda i: (jnp.maximum(i * hb - 1, 0), ca)),
            pl.BlockSpec((CONV_HALO, CONV_W), lambda i: (jnp.maximum(i * hb - 1, 0), cg)),
            pl.BlockSpec((CONV_HALO, CONV_W), lambda i: (jnp.minimum((i + 1) * hb, last), ca)),
            pl.BlockSpec((CONV_HALO, CONV_W), lambda i: (jnp.minimum((i + 1) * hb, last), cg)),
            pl.BlockSpec((CONV_K, CONV_W), lambda i: (0, 0)),
            row, row, row,
            pl.BlockSpec((None, CONV_W, CONV_W), lambda i: (layer, 0, 0)),
        ],
        out_specs=pl.BlockSpec((tm, CONV_W), lambda i: (i, 0)),
        out_shape=jax.ShapeDtypeStruct((t, CONV_W), BF16),
        scratch_shapes=[
            pltpu.VMEM((tm + 2 * CONV_HALO, CONV_W), F32),
            pltpu.VMEM((SUBLANES, tm + 2 * CONV_HALO - SUBLANES, CONV_W), F32),
        ],
        compiler_params=_params("parallel"),
        name="conv_mixer",
    )(u, u, u, u, u, u, dw, vec(dwb), vec(lng), vec(lnb), pw)


def _hgrn_tables(reverse):
    c = HGRN_CHUNK
    levels = []
    h = c // 2
    while h >= 1:
        levels.append(h)
        h //= 2
    r = np.arange(c)
    col = r[None, :]
    seg = np.zeros((len(levels) + 2, c, c), np.float32)
    mask = np.zeros((len(levels) + 1, c, c), np.float32)
    for li, h in enumerate(levels):
        blk = r // (2 * h)
        upper = (r % (2 * h)) >= h
        b = (blk * 2 * h + h - 1)[:, None]
        rr = r[:, None]
        seg[li] = np.where(upper[:, None], (col > b) & (col <= rr), (col > rr) & (col <= b))
        mask[li] = upper[:, None] & (~upper)[None, :] & (blk[:, None] == blk[None, :])
    seg[-2] = col <= r[:, None]
    seg[-1] = col > r[:, None]
    mask[-1] = np.eye(c)
    if reverse:
        seg = seg[:, ::-1, ::-1]
        mask = mask[:, ::-1, ::-1]
    seg = seg.reshape(-1, c)
    seg2 = np.concatenate([seg, seg], axis=1)
    return (jnp.asarray(seg2, dtype=BF16), jnp.asarray(mask, dtype=F32), len(levels))


def _hgrn_kernel(*refs, reverse, final, n_levels, n_chunks):
    if final:
        hq_ref, hf_ref, hi_ref, lb_ref, seg_ref, mask_ref, oprev_ref, hg_ref, gn_ref, o_ref, st_ref = refs
    else:
        hq_ref, hf_ref, hi_ref, lb_ref, seg_ref, mask_ref, o_ref, st_ref = refs
    c = HGRN_CHUNK

    @pl.when(pl.program_id(1) == 0)
    def _():
        st_ref[...] = jnp.zeros_like(st_ref)

    lb = lb_ref[...]
    seg = seg_ref[...]
    total_row = 0 if reverse else c - 1

    def prep(j):
        rows = slice(j * c, (j + 1) * c)
        f = lb + (1.0 - lb) * _sigmoid(hf_ref[rows, :].astype(F32))
        log2f = jnp.log(f) * LOG2_E
        g_hi = log2f.astype(BF16)
        g_lo = (log2f - g_hi.astype(F32)).astype(BF16)
        e_all = jnp.exp2(_dot(seg, jnp.concatenate([g_hi, g_lo], axis=0)))
        decay = e_all[n_levels * c + total_row:n_levels * c + total_row + 1, :]
        return _silu(hq_ref[rows, :].astype(F32)).astype(BF16), (1.0 - f).astype(BF16), e_all.astype(BF16), decay

    def intra(prepped):
        qq, kk, e_all, _ = prepped
        mats = []
        for hd in range(HGRN_HEADS):
            sl = slice(hd * HGRN_D, (hd + 1) * HGRN_D)
            qh, kh = qq[:, sl], kk[:, sl]
            a = mask_ref[n_levels] * _dot_nt(qh, kh)
            for l in range(n_levels):
                el = e_all[l * c:(l + 1) * c, sl]
                a = a + mask_ref[l] * _dot_nt(qh * el, kh * el)
            mats.append(a.astype(BF16))
        return mats

    def outputs(j, prepped, mats):
        qq, kk, e_all, decay = prepped
        rows = slice(j * c, (j + 1) * c)
        if final:
            gate = _silu(hg_ref[rows, :].astype(F32)) * gn_ref[...]
        for hd in range(HGRN_HEADS):
            sl = slice(hd * HGRN_D, (hd + 1) * HGRN_D)
            qh, kh, vh = qq[:, sl], kk[:, sl], hi_ref[rows, sl]
            e_pre = e_all[n_levels * c:(n_levels + 1) * c, sl]
            e_suf = e_all[(n_levels + 1) * c:(n_levels + 2) * c, sl]
            st = st_ref[hd]
            o = _dot(mats[hd], vh) + _dot_nt(qh * e_pre, st.astype(BF16))
            v_t = vh.astype(F32).T.astype(BF16)
            st_ref[hd] = decay[:, sl] * st + _dot(v_t, kh * e_suf)
            if final:
                o = o + oprev_ref[rows, sl]
                o = o * lax.rsqrt(jnp.mean(o * o, axis=-1, keepdims=True) + NORM_EPS)
                o_ref[rows, sl] = (o * gate[:, sl]).astype(o_ref.dtype)
            else:
                o_ref[rows, sl] = o

    order = list(range(n_chunks))[::-1] if reverse else list(range(n_chunks))
    prepped, mats = {}, {}
    for step in range(n_chunks + 2):
        if step < n_chunks:
            prepped[step] = prep(order[step])
        if 0 <= step - 1 < n_chunks:
            mats[step - 1] = intra(prepped[step - 1])
        if 0 <= step - 2 < n_chunks:
            outputs(order[step - 2], prepped.pop(step - 2), mats.pop(step - 2))


def _hgrn_pass(u, lb, batch, n, *, reverse, o_prev=None, gn=None):
    t = u.shape[0]
    n_chunks = _tile(n // HGRN_CHUNK, HGRN_CHUNKS_PER_STEP)
    c = n_chunks * HGRN_CHUNK
    nc = n // c
    final = o_prev is not None
    seg, mask, n_levels = _hgrn_tables(reverse)

    def rows(b, ci):
        return b * nc + (nc - 1 - ci if reverse else ci)

    def col_spec(off):
        return pl.BlockSpec((c, HGRN_W), lambda b, ci: (rows(b, ci), off // HGRN_W))

    const2 = lambda b, ci: (0, 0)
    in_specs = [
        col_spec(OFF_HQ),
        col_spec(OFF_FB if reverse else OFF_FF),
        col_spec(OFF_HI),
        pl.BlockSpec((1, HGRN_W), const2),
        pl.BlockSpec(seg.shape, const2),
        pl.BlockSpec(mask.shape, lambda b, ci: (0, 0, 0)),
    ]
    args = [u, u, u, lb.reshape(1, HGRN_W), seg, mask]
    if final:
        in_specs += [
            pl.BlockSpec((c, HGRN_W), lambda b, ci: (rows(b, ci), 0)),
            col_spec(OFF_HG),
            pl.BlockSpec((1, HGRN_W), const2),
        ]
        args += [o_prev, u, gn.reshape(1, HGRN_W)]
    return pl.pallas_call(
        functools.partial(_hgrn_kernel, reverse=reverse, final=final, n_levels=n_levels, n_chunks=n_chunks),
        grid=(batch, nc),
        in_specs=in_specs,
        out_specs=pl.BlockSpec((c, HGRN_W), lambda b, ci: (rows(b, ci), 0)),
        out_shape=jax.ShapeDtypeStruct((t, HGRN_W), BF16 if final else F32),
        scratch_shapes=[pltpu.VMEM((HGRN_HEADS, HGRN_D, HGRN_D), F32)],
        compiler_params=_params("parallel", "arbitrary"),
        name="hgrn_bwd" if reverse else "hgrn_fwd",
    )(*args)


def _out_proj_kernel(a_ref, c_ref, h_ref, x_ref, w_ref, g_ref, gnext_ref, o_ref, xn_ref):
    halves = _row_halves(x_ref.shape[0])
    mixes = []
    for rows in halves:
        mix = _dot(a_ref[rows, :], w_ref[0:ATTN_W, :])
        mix = mix + _dot(c_ref[rows, :], w_ref[ATTN_W:ATTN_W + CONV_W, :])
        mixes.append(mix + _dot(h_ref[rows, :], w_ref[ATTN_W + CONV_W:, :]))
    for rows, mix in zip(halves, mixes):
        x = x_ref[rows, :] + _rms(mix, g_ref[...])
        o_ref[rows, :] = x
        xn_ref[rows, :] = _rms(x, gnext_ref[...]).astype(xn_ref.dtype)


def _out_proj(a, cv, hg, x, w, layer, g, g_next):
    t, d = x.shape
    tm = _tile(t, 512)
    mixw = w.shape[1]
    return pl.pallas_call(
        _out_proj_kernel,
        grid=(t // tm,),
        in_specs=[
            pl.BlockSpec((tm, ATTN_W), lambda i: (i, 0)),
            pl.BlockSpec((tm, CONV_W), lambda i: (i, 0)),
            pl.BlockSpec((tm, HGRN_W), lambda i: (i, 0)),
            pl.BlockSpec((tm, d), lambda i: (i, 0)),
            pl.BlockSpec((None, mixw, d), lambda i: (layer, 0, 0)),
            pl.BlockSpec((1, d), lambda i: (0, 0)),
            pl.BlockSpec((1, d), lambda i: (0, 0)),
        ],
        out_specs=[pl.BlockSpec((tm, d), lambda i: (i, 0)), pl.BlockSpec((tm, d), lambda i: (i, 0))],
        out_shape=[jax.ShapeDtypeStruct((t, d), F32), jax.ShapeDtypeStruct((t, d), BF16)],
        compiler_params=_params("parallel"),
        name="out_proj",
    )(a, cv, hg, x, w, g.reshape(1, d), g_next.reshape(1, d))


def _ffn_kernel(xn_ref, wg_ref, wu_ref, wo_ref, gpost_ref, o_ref, acc_ref):
    j = pl.program_id(1)

    @pl.when(j == 0)
    def _():
        acc_ref[...] = jnp.zeros_like(acc_ref)

    xn = xn_ref[...]
    hidden = _silu(_dot(xn, wg_ref[...])) * _dot(xn, wu_ref[...])
    acc_ref[...] += _dot(hidden.astype(BF16), wo_ref[...])

    @pl.when(j == pl.num_programs(1) - 1)
    def _():
        o_ref[...] = _rms(acc_ref[...], gpost_ref[...]).astype(o_ref.dtype)


def _ffn(xn, w_in, w_out, layer, gpost):
    t, d = xn.shape
    f = w_out.shape[1]
    tm, th = _tile(t, 1024), _tile(f, 512)
    nh = f // th
    return pl.pallas_call(
        _ffn_kernel,
        grid=(t // tm, nh),
        in_specs=[
            pl.BlockSpec((tm, d), lambda i, j: (i, 0)),
            pl.BlockSpec((None, d, th), lambda i, j: (layer, 0, j)),
            pl.BlockSpec((None, d, th), lambda i, j: (layer, 0, nh + j)),
            pl.BlockSpec((None, th, d), lambda i, j: (layer, j, 0)),
            pl.BlockSpec((1, d), lambda i, j: (0, 0)),
        ],
        out_specs=pl.BlockSpec((tm, d), lambda i, j: (i, 0)),
        out_shape=jax.ShapeDtypeStruct((t, d), BF16),
        scratch_shapes=[pltpu.VMEM((tm, d), F32)],
        compiler_params=_params("parallel", "arbitrary"),
        name="ffn",
    )(xn, w_in, w_in, w_out, gpost.reshape(1, d))


def _ple_kernel(x_ref, y_ref, p_ref, wg_ref, wp_ref, g_ref, o_ref):
    halves = _row_halves(x_ref.shape[0])
    parts = []
    for rows in halves:
        x = x_ref[rows, :] + y_ref[rows, :].astype(F32)
        parts.append((x, _dot(x.astype(BF16), wg_ref[...]), _dot(p_ref[rows, :].astype(BF16), wp_ref[...])))
    for rows, (x, gate_logit, proj) in zip(halves, parts):
        o_ref[rows, :] = x + _rms(proj * _sigmoid(gate_logit), g_ref[...])


def _ple(x, y, p, layer, w_gate, w_proj, g):
    t, d = x.shape
    pd = p.shape[-1]
    tm = _tile(t, 512)
    return pl.pallas_call(
        _ple_kernel,
        grid=(t // tm,),
        in_specs=[
            pl.BlockSpec((tm, d), lambda i: (i, 0)),
            pl.BlockSpec((tm, d), lambda i: (i, 0)),
            pl.BlockSpec((None, tm, pd), lambda i: (layer, i, 0)),
            pl.BlockSpec((None, d, d), lambda i: (layer, 0, 0)),
            pl.BlockSpec((None, pd, d), lambda i: (layer, 0, 0)),
            pl.BlockSpec((1, d), lambda i: (0, 0)),
        ],
        out_specs=pl.BlockSpec((tm, d), lambda i: (i, 0)),
        out_shape=jax.ShapeDtypeStruct((t, d), F32),
        compiler_params=_params("parallel"),
        name="ple",
    )(x, y, p, w_gate, w_proj, g.reshape(1, d))


def _trunk(x, p, lb, layers, mats):
    depth = p.shape[0]
    batch, n, d = x.shape
    t = batch * n
    x = x.reshape(t, d)
    p = p.reshape(depth, t, p.shape[-1])
    cos_l, sin_l = _rope_tables(n)
    for l, w in enumerate(layers):
        u = _in_proj(x, w["norm_mix_pre"], mats["w_in"], l)
        qk_r, v_t = _attn_prep(u, cos_l, sin_l, w["q_norm"], w["k_norm"], n)
        a_out = _flash(qk_r, v_t, w["q_norm"], w["k_norm"], batch, n)
        c_out = _conv(u, w["conv_dw"], w["conv_dw_b"], w["conv_ln_g"], w["conv_ln_b"], mats["conv_pw"], l, n)
        o_fwd = _hgrn_pass(u, lb[l, 0], batch, n, reverse=False)
        h_out = _hgrn_pass(u, lb[l, 1], batch, n, reverse=True, o_prev=o_fwd, gn=w["hgrn_gn"])
        x, xn = _out_proj(a_out, c_out, h_out, x, mats["w_out"], l, w["norm_mix_post"], w["norm_ffn_pre"])
        y = _ffn(xn, mats["w_ffn_in"], mats["w_ffn_out"], l, w["norm_ffn_post"])
        x = _ple(x, y, p, l, mats["w_ple_gate"], mats["w_ple_proj"], w["ple_norm"])
    return x.reshape(batch, n, d)


def kernel(x_prompt, x_sample, p_prompt, p_sample, norm_mix_pre, norm_mix_post, w_in, q_norm, k_norm, conv_dw, conv_dw_b, conv_ln_g, conv_ln_b, conv_pw, hgrn_lb, hgrn_gn, w_out, norm_ffn_pre, norm_ffn_post, w_ffn_in, w_ffn_out, w_ple_gate, w_ple_proj, ple_norm):
    sm = jax.nn.softmax(hgrn_lb.astype(F32), axis=0)
    lb = jnp.cumsum(sm, axis=0) - sm[0]
    vectors = dict(
        norm_mix_pre=norm_mix_pre, norm_mix_post=norm_mix_post, q_norm=q_norm, k_norm=k_norm,
        conv_dw=conv_dw, conv_dw_b=conv_dw_b, conv_ln_g=conv_ln_g, conv_ln_b=conv_ln_b,
        hgrn_gn=hgrn_gn, norm_ffn_pre=norm_ffn_pre, norm_ffn_post=norm_ffn_post, ple_norm=ple_norm,
    )
    matrices = dict(
        w_in=w_in, conv_pw=conv_pw, w_out=w_out, w_ffn_in=w_ffn_in, w_ffn_out=w_ffn_out,
        w_ple_gate=w_ple_gate, w_ple_proj=w_ple_proj,
    )
    mats = {k: v.astype(BF16) for k, v in matrices.items()}
    layers = [{k: v[l] for k, v in vectors.items()} for l in range(p_prompt.shape[0])]
    y_prompt = _trunk(x_prompt, p_prompt, lb, layers, mats)
    y_sample = _trunk(x_sample, p_sample, lb, layers, mats)
    return (y_prompt, y_sample)
```

```python
import functools

import numpy as np
import jax
import jax.numpy as jnp
from jax import lax
from jax.experimental import pallas as pl
from jax.experimental.pallas import tpu as pltpu

F32 = jnp.float32
BF16 = jnp.bfloat16

NORM_EPS = 1e-6
ROPE_THETA = 10000.0
LOG2_E = 1.4426950408889634
GRID_W = 64
HEAD_DIM = 128
N_Q_HEADS = 8
N_KV_HEADS = 2
Q_GROUP = N_Q_HEADS // N_KV_HEADS
ATTN_W = N_Q_HEADS * HEAD_DIM
KV_W = N_KV_HEADS * HEAD_DIM
CONV_W = 512
CONV_K = 31
CONV_HALO = 16
SUBLANES = 8
HGRN_W = 512
HGRN_D = 128
HGRN_HEADS = HGRN_W // HGRN_D
HGRN_CHUNK = 128
HGRN_CHUNKS_PER_STEP = 8
FLASH_ONES_ROWS = 16
FLASH_LOOKAHEAD = 3
FLASH_TQ = 512
FLASH_TK = 8192
FLASH_SUB = 512
FLASH_FIXED_SHIFT_MAX_RANGE = 96.0

OFF_Q = 0
OFF_K = OFF_Q + ATTN_W
OFF_V = OFF_K + KV_W
OFF_CA = OFF_V + KV_W
OFF_CG = OFF_CA + CONV_W
OFF_HQ = OFF_CG + CONV_W
OFF_FF = OFF_HQ + HGRN_W
OFF_FB = OFF_FF + HGRN_W
OFF_HI = OFF_FB + HGRN_W
OFF_HG = OFF_HI + HGRN_W
IN_COLS = OFF_HG + HGRN_W

V7X_VMEM_LIMIT_BYTES = 56 * 1024 * 1024


def _params(*semantics):
    return pltpu.CompilerParams(dimension_semantics=semantics, vmem_limit_bytes=V7X_VMEM_LIMIT_BYTES)


def _tile(dim, pref):
    t = min(dim, pref)
    assert dim % t == 0, (dim, pref)
    return t


def _row_halves(rows):
    half = rows // 2
    return [slice(0, half), slice(half, rows)] if half % 16 == 0 else [slice(0, rows)]


def _rms(x, g):
    return x * lax.rsqrt(jnp.mean(x * x, axis=-1, keepdims=True) + NORM_EPS) * g


def _sigmoid(x):
    return 1.0 / (1.0 + jnp.exp(-x))


def _silu(x):
    return x * _sigmoid(x)


def _dot(a, b):
    return jnp.dot(a, b, preferred_element_type=F32)


def _dot_nt(a, b):
    return lax.dot_general(a, b, (((1,), (1,)), ((), ())), preferred_element_type=F32)


def _in_proj_kernel(x_ref, g_ref, w_ref, o_ref, xn_ref):
    @pl.when(pl.program_id(1) == 0)
    def _():
        xn_ref[...] = _rms(x_ref[...], g_ref[...]).astype(BF16)

    o_ref[...] = _dot(xn_ref[...], w_ref[...]).astype(o_ref.dtype)


def _in_proj(x, g, w, layer):
    t, d = x.shape
    n = w.shape[-1]
    tm, tn = _tile(t, 1024), _tile(n, 1280)
    return pl.pallas_call(
        _in_proj_kernel,
        grid=(t // tm, n // tn),
        in_specs=[
            pl.BlockSpec((tm, d), lambda i, j: (i, 0)),
            pl.BlockSpec((1, d), lambda i, j: (0, 0)),
            pl.BlockSpec((None, d, tn), lambda i, j: (layer, 0, j)),
        ],
        out_specs=pl.BlockSpec((tm, tn), lambda i, j: (i, j)),
        out_shape=jax.ShapeDtypeStruct((t, n), BF16),
        scratch_shapes=[pltpu.VMEM((tm, d), BF16)],
        compiler_params=_params("parallel", "arbitrary"),
        name="in_proj",
    )(x, g.reshape(1, d), w)


def _rope_tables(n):
    rows = n // GRID_W
    row = jnp.repeat(jnp.arange(rows, dtype=F32), GRID_W)
    col = jnp.tile(jnp.arange(GRID_W, dtype=F32), rows)
    axis_dims = HEAD_DIM // 2
    inv = ROPE_THETA ** (-jnp.arange(0, axis_dims, 2, dtype=F32) / axis_dims)
    ang = jnp.concatenate([row[:, None] * inv, col[:, None] * inv], axis=-1)
    cos, sin = jnp.cos(ang), jnp.sin(ang)
    cos_l = jnp.repeat(cos, 2, axis=-1)
    sin_l = jnp.stack([-sin, sin], axis=-1).reshape(n, HEAD_DIM)
    return cos_l, sin_l


def _attn_prep_kernel(q_ref, k_ref, v_ref, cos_ref, sin_ref, qg_ref, kg_ref, qk_ref, vt_ref):
    cos = cos_ref[...]
    sin = sin_ref[...]
    lane = lax.broadcasted_iota(jnp.int32, cos.shape, 1)
    even = (lane & 1) == 0

    def norm_rope(x, g):
        y = _rms(x.astype(F32), g)
        partner = jnp.where(even, pltpu.roll(y, HEAD_DIM - 1, 1), pltpu.roll(y, 1, 1))
        return y * cos + partner * sin

    scale = HEAD_DIM ** -0.5 * LOG2_E
    for h in range(N_Q_HEADS):
        sl = slice(h * HEAD_DIM, (h + 1) * HEAD_DIM)
        qk_ref[:, sl] = (norm_rope(q_ref[:, sl], qg_ref[...]) * scale).astype(qk_ref.dtype)
    for h in range(N_KV_HEADS):
        sl = slice(h * HEAD_DIM, (h + 1) * HEAD_DIM)
        so = slice(ATTN_W + h * HEAD_DIM, ATTN_W + (h + 1) * HEAD_DIM)
        qk_ref[:, so] = norm_rope(k_ref[:, sl], kg_ref[...]).astype(qk_ref.dtype)
    rows = HEAD_DIM + FLASH_ONES_ROWS
    for h in range(N_KV_HEADS):
        vt_ref[h * rows:h * rows + HEAD_DIM, :] = (
            v_ref[:, h * HEAD_DIM:(h + 1) * HEAD_DIM].astype(F32).T.astype(vt_ref.dtype))
        vt_ref[h * rows + HEAD_DIM:(h + 1) * rows, :] = jnp.ones((FLASH_ONES_ROWS, v_ref.shape[0]), vt_ref.dtype)


def _attn_prep(u, cos_l, sin_l, q_gain, k_gain, n):
    t = u.shape[0]
    tm = _tile(n, 1024)
    per_seq = n // tm
    return pl.pallas_call(
        _attn_prep_kernel,
        grid=(t // tm,),
        in_specs=[
            pl.BlockSpec((tm, ATTN_W), lambda i: (i, OFF_Q // ATTN_W)),
            pl.BlockSpec((tm, KV_W), lambda i: (i, OFF_K // KV_W)),
            pl.BlockSpec((tm, KV_W), lambda i: (i, OFF_V // KV_W)),
            pl.BlockSpec((tm, HEAD_DIM), lambda i: (i % per_seq, 0)),
            pl.BlockSpec((tm, HEAD_DIM), lambda i: (i % per_seq, 0)),
            pl.BlockSpec((1, HEAD_DIM), lambda i: (0, 0)),
            pl.BlockSpec((1, HEAD_DIM), lambda i: (0, 0)),
        ],
        out_specs=[
            pl.BlockSpec((tm, ATTN_W + KV_W), lambda i: (i, 0)),
            pl.BlockSpec((N_KV_HEADS * (HEAD_DIM + FLASH_ONES_ROWS), tm), lambda i: (0, i)),
        ],
        out_shape=[
            jax.ShapeDtypeStruct((t, ATTN_W + KV_W), BF16),
            jax.ShapeDtypeStruct((N_KV_HEADS * (HEAD_DIM + FLASH_ONES_ROWS), t), BF16),
        ],
        compiler_params=_params("parallel"),
        name="attn_prep",
    )(u, u, u, cos_l, sin_l, q_gain.reshape(1, HEAD_DIM), k_gain.reshape(1, HEAD_DIM))


def _flash_kernel(shift_ref, q_ref, k_ref, vt_ref, o_ref, m_ref, acc_ref, *, tk, sub, fixed_shift):
    ki = pl.program_id(3)

    @pl.when(ki == 0)
    def _():
        m_ref[...] = jnp.full_like(m_ref, -jnp.inf)
        acc_ref[...] = jnp.zeros_like(acc_ref)

    stages = [(s0, g) for s0 in range(0, tk, sub) for g in range(Q_GROUP)]

    def scores(stage):
        s0, g = stage
        return _dot_nt(k_ref[s0:s0 + sub, :], q_ref[:, g * HEAD_DIM:(g + 1) * HEAD_DIM])

    shift = shift_ref[0, 0]
    partial = [None] * Q_GROUP

    pending = [scores(st) for st in stages[:FLASH_LOOKAHEAD]]
    for i, (s0, g) in enumerate(stages):
        s = pending.pop(0)
        if i + FLASH_LOOKAHEAD < len(stages):
            pending.append(scores(stages[i + FLASH_LOOKAHEAD]))
        vt1 = vt_ref[:, s0:s0 + sub]
        if fixed_shift:
            pv = _dot(vt1, jnp.exp2(s - shift).astype(BF16))
            partial[g] = pv if partial[g] is None else partial[g] + pv
        else:
            m_prev = m_ref[g]
            m_new = jnp.maximum(m_prev, jnp.max(s, axis=0, keepdims=True))
            alpha = jnp.exp2(m_prev - m_new)
            p = jnp.exp2((s - m_new).astype(BF16))
            acc_ref[g] = alpha * acc_ref[g] + _dot(vt1, p)
            m_ref[g] = m_new
    if fixed_shift:
        for g in range(Q_GROUP):
            acc_ref[g] += partial[g]

    @pl.when(ki == pl.num_programs(3) - 1)
    def _():
        for g in range(Q_GROUP):
            out = acc_ref[g, 0:HEAD_DIM, :] / acc_ref[g, HEAD_DIM:HEAD_DIM + 1, :]
            o_ref[:, g * HEAD_DIM:(g + 1) * HEAD_DIM] = out.T.astype(o_ref.dtype)


def _flash(qk, vt, q_gain, k_gain, batch, n):
    t = qk.shape[0]
    tq, tk = _tile(n, FLASH_TQ), _tile(n, FLASH_TK)
    sub = _tile(tk, FLASH_SUB)
    nq, nk = n // tq, n // tk
    gw = Q_GROUP * HEAD_DIM
    bound = (HEAD_DIM ** 0.5 * LOG2_E) * jnp.max(jnp.abs(q_gain)) * jnp.max(jnp.abs(k_gain))
    shift = bound.astype(F32).reshape(1, 1)

    def call(fixed_shift):
        return pl.pallas_call(
            functools.partial(_flash_kernel, tk=tk, sub=sub, fixed_shift=fixed_shift),
            grid=(batch, N_KV_HEADS, nq, nk),
            in_specs=[
                pl.BlockSpec(memory_space=pltpu.SMEM),
                pl.BlockSpec((tq, gw), lambda b, h, qi, ki: (b * nq + qi, h)),
                pl.BlockSpec((tk, HEAD_DIM), lambda b, h, qi, ki: (b * nk + ki, N_Q_HEADS + h)),
                pl.BlockSpec((HEAD_DIM + FLASH_ONES_ROWS, tk), lambda b, h, qi, ki: (h, b * nk + ki)),
            ],
            out_specs=pl.BlockSpec((tq, gw), lambda b, h, qi, ki: (b * nq + qi, h)),
            out_shape=jax.ShapeDtypeStruct((t, ATTN_W), BF16),
            scratch_shapes=[
                pltpu.VMEM((Q_GROUP, 1, tq), F32),
                pltpu.VMEM((Q_GROUP, HEAD_DIM + FLASH_ONES_ROWS, tq), F32),
            ],
            compiler_params=_params("parallel", "parallel", "parallel", "arbitrary"),
            name="flash_attn_fixed" if fixed_shift else "flash_attn",
        )(shift, qk, qk, vt)

    return lax.cond(2.0 * bound <= FLASH_FIXED_SHIFT_MAX_RANGE, lambda: call(True), lambda: call(False))


def _conv_kernel(a_ref, g_ref, ap_ref, gp_ref, an_ref, gn_ref, dw_ref, dwb_ref, lng_ref, lnb_ref, pw_ref,
                 o_ref, h_ref, sh_ref, *, tm, per_seq):
    i = pl.program_id(0)
    pos = i % per_seq

    def glu(a, g):
        return a.astype(F32) * _sigmoid(g.astype(F32))

    prev = jnp.where(pos == 0, 0.0, glu(ap_ref[...], gp_ref[...]))
    nxt = jnp.where(pos == per_seq - 1, 0.0, glu(an_ref[...], gn_ref[...]))
    h_ref[0:CONV_HALO, :] = prev
    h_ref[CONV_HALO:CONV_HALO + tm, :] = glu(a_ref[...], g_ref[...])
    h_ref[CONV_HALO + tm:, :] = nxt

    base = CONV_HALO - CONV_K // 2
    span = tm + 2 * CONV_HALO - SUBLANES
    for b in range(SUBLANES):
        sh_ref[b] = h_ref[b:b + span, :]
    acc = jnp.zeros((tm, CONV_W), F32) + dwb_ref[...]
    for j in range(CONV_K):
        a, b = divmod(base + j, SUBLANES)
        acc = acc + dw_ref[j:j + 1, :] * sh_ref[b, a * SUBLANES:a * SUBLANES + tm, :]
    mu = jnp.mean(acc, axis=-1, keepdims=True)
    cen = acc - mu
    var = jnp.mean(cen * cen, axis=-1, keepdims=True)
    y = cen * lax.rsqrt(var + NORM_EPS) * lng_ref[...] + lnb_ref[...]
    o_ref[...] = _dot(_silu(y).astype(BF16), pw_ref[...]).astype(o_ref.dtype)


def _conv(u, dw, dwb, lng, lnb, pw, layer, n):
    t = u.shape[0]
    tm = _tile(n, 1024)
    per_seq = n // tm
    hb = tm // CONV_HALO
    last = t // CONV_HALO - 1
    ca, cg = OFF_CA // CONV_W, OFF_CG // CONV_W
    vec = lambda v: v.reshape(1, CONV_W)
    row = pl.BlockSpec((1, CONV_W), lambda i: (0, 0))
    return pl.pallas_call(
        functools.partial(_conv_kernel, tm=tm, per_seq=per_seq),
        grid=(t // tm,),
        in_specs=[
            pl.BlockSpec((tm, CONV_W), lambda i: (i, ca)),
            pl.BlockSpec((tm, CONV_W), lambda i: (i, cg)),
            pl.BlockSpec((CONV_HALO, CONV_W), lambda i: (jnp.maximum(i * hb - 1, 0), ca)),
            pl.BlockSpec((CONV_HALO, CONV_W), lambda i: (jnp.maximum(i * hb - 1, 0), cg)),
            pl.BlockSpec((CONV_HALO, CONV_W), lambda i: (jnp.minimum((i + 1) * hb, last), ca)),
            pl.BlockSpec((CONV_HALO, CONV_W), lambda i: (jnp.minimum((i + 1) * hb, last), cg)),
            pl.BlockSpec((CONV_K, CONV_W), lambda i: (0, 0)),
            row, row, row,
            pl.BlockSpec((None, CONV_W, CONV_W), lambda i: (layer, 0, 0)),
        ],
        out_specs=pl.BlockSpec((tm, CONV_W), lambda i: (i, 0)),
        out_shape=jax.ShapeDtypeStruct((t, CONV_W), BF16),
        scratch_shapes=[
            pltpu.VMEM((tm + 2 * CONV_HALO, CONV_W), F32),
            pltpu.VMEM((SUBLANES, tm + 2 * CONV_HALO - SUBLANES, CONV_W), F32),
        ],
        compiler_params=_params("parallel"),
        name="conv_mixer",
    )(u, u, u, u, u, u, dw, vec(dwb), vec(lng), vec(lnb), pw)


def _hgrn_tables(reverse):
    c = HGRN_CHUNK
    levels = []
    h = c // 2
    while h >= 1:
        levels.append(h)
        h //= 2
    r = np.arange(c)
    col = r[None, :]
    seg = np.zeros((len(levels) + 2, c, c), np.float32)
    mask = np.zeros((len(levels) + 1, c, c), np.float32)
    for li, h in enumerate(levels):
        blk = r // (2 * h)
        upper = (r % (2 * h)) >= h
        b = (blk * 2 * h + h - 1)[:, None]
        rr = r[:, None]
        seg[li] = np.where(upper[:, None], (col > b) & (col <= rr), (col > rr) & (col <= b))
        mask[li] = upper[:, None] & (~upper)[None, :] & (blk[:, None] == blk[None, :])
    seg[-2] = col <= r[:, None]
    seg[-1] = col > r[:, None]
    mask[-1] = np.eye(c)
    if reverse:
        seg = seg[:, ::-1, ::-1]
        mask = mask[:, ::-1, ::-1]
    seg = seg.reshape(-1, c)
    seg2 = np.concatenate([seg, seg], axis=1)
    return (jnp.asarray(seg2, dtype=BF16), jnp.asarray(mask, dtype=F32), len(levels))


def _hgrn_kernel(*refs, reverse, final, n_levels, n_chunks):
    if final:
        hq_ref, hf_ref, hi_ref, lb_ref, seg_ref, mask_ref, oprev_ref, hg_ref, gn_ref, o_ref, st_ref = refs
    else:
        hq_ref, hf_ref, hi_ref, lb_ref, seg_ref, mask_ref, o_ref, st_ref = refs
    c = HGRN_CHUNK

    @pl.when(pl.program_id(1) == 0)
    def _():
        st_ref[...] = jnp.zeros_like(st_ref)

    lb = lb_ref[...]
    seg = seg_ref[...]
    total_row = 0 if reverse else c - 1

    def prep(j):
        rows = slice(j * c, (j + 1) * c)
        f = lb + (1.0 - lb) * _sigmoid(hf_ref[rows, :].astype(F32))
        log2f = jnp.log(f) * LOG2_E
        g_hi = log2f.astype(BF16)
        g_lo = (log2f - g_hi.astype(F32)).astype(BF16)
        e_all = jnp.exp2(_dot(seg, jnp.concatenate([g_hi, g_lo], axis=0)))
        decay = e_all[n_levels * c + total_row:n_levels * c + total_row + 1, :]
        return _silu(hq_ref[rows, :].astype(F32)).astype(BF16), (1.0 - f).astype(BF16), e_all.astype(BF16), decay

    def intra(prepped):
        qq, kk, e_all, _ = prepped
        mats = []
        for hd in range(HGRN_HEADS):
            sl = slice(hd * HGRN_D, (hd + 1) * HGRN_D)
            qh, kh = qq[:, sl], kk[:, sl]
            a = mask_ref[n_levels] * _dot_nt(qh, kh)
            for l in range(n_levels):
                el = e_all[l * c:(l + 1) * c, sl]
                a = a + mask_ref[l] * _dot_nt(qh * el, kh * el)
            mats.append(a.astype(BF16))
        return mats

    def outputs(j, prepped, mats):
        qq, kk, e_all, decay = prepped
        rows = slice(j * c, (j + 1) * c)
        if final:
            gate = _silu(hg_ref[rows, :].astype(F32)) * gn_ref[...]
        for hd in range(HGRN_HEADS):
            sl = slice(hd * HGRN_D, (hd + 1) * HGRN_D)
            qh, kh, vh = qq[:, sl], kk[:, sl], hi_ref[rows, sl]
            e_pre = e_all[n_levels * c:(n_levels + 1) * c, sl]
            e_suf = e_all[(n_levels + 1) * c:(n_levels + 2) * c, sl]
            st = st_ref[hd]
            o = _dot(mats[hd], vh) + _dot_nt(qh * e_pre, st.astype(BF16))
            v_t = vh.astype(F32).T.astype(BF16)
            st_ref[hd] = decay[:, sl] * st + _dot(v_t, kh * e_suf)
            if final:
                o = o + oprev_ref[rows, sl]
                o = o * lax.rsqrt(jnp.mean(o * o, axis=-1, keepdims=True) + NORM_EPS)
                o_ref[rows, sl] = (o * gate[:, sl]).astype(o_ref.dtype)
            else:
                o_ref[rows, sl] = o

    order = list(range(n_chunks))[::-1] if reverse else list(range(n_chunks))
    prepped, mats = {}, {}
    for step in range(n_chunks + 2):
        if step < n_chunks:
            prepped[step] = prep(order[step])
        if 0 <= step - 1 < n_chunks:
            mats[step - 1] = intra(prepped[step - 1])
        if 0 <= step - 2 < n_chunks:
            outputs(order[step - 2], prepped.pop(step - 2), mats.pop(step - 2))


def _hgrn_pass(u, lb, batch, n, *, reverse, o_prev=None, gn=None):
    t = u.shape[0]
    n_chunks = _tile(n // HGRN_CHUNK, HGRN_CHUNKS_PER_STEP)
    c = n_chunks * HGRN_CHUNK
    nc = n // c
    final = o_prev is not None
    seg, mask, n_levels = _hgrn_tables(reverse)

    def rows(b, ci):
        return b * nc + (nc - 1 - ci if reverse else ci)

    def col_spec(off):
        return pl.BlockSpec((c, HGRN_W), lambda b, ci: (rows(b, ci), off // HGRN_W))

    const2 = lambda b, ci: (0, 0)
    in_specs = [
        col_spec(OFF_HQ),
        col_spec(OFF_FB if reverse else OFF_FF),
        col_spec(OFF_HI),
        pl.BlockSpec((1, HGRN_W), const2),
        pl.BlockSpec(seg.shape, const2),
        pl.BlockSpec(mask.shape, lambda b, ci: (0, 0, 0)),
    ]
    args = [u, u, u, lb.reshape(1, HGRN_W), seg, mask]
    if final:
        in_specs += [
            pl.BlockSpec((c, HGRN_W), lambda b, ci: (rows(b, ci), 0)),
            col_spec(OFF_HG),
            pl.BlockSpec((1, HGRN_W), const2),
        ]
        args += [o_prev, u, gn.reshape(1, HGRN_W)]
    return pl.pallas_call(
        functools.partial(_hgrn_kernel, reverse=reverse, final=final, n_levels=n_levels, n_chunks=n_chunks),
        grid=(batch, nc),
        in_specs=in_specs,
        out_specs=pl.BlockSpec((c, HGRN_W), lambda b, ci: (rows(b, ci), 0)),
        out_shape=jax.ShapeDtypeStruct((t, HGRN_W), BF16 if final else F32),
        scratch_shapes=[pltpu.VMEM((HGRN_HEADS, HGRN_D, HGRN_D), F32)],
        compiler_params=_params("parallel", "arbitrary"),
        name="hgrn_bwd" if reverse else "hgrn_fwd",
    )(*args)


def _out_proj_kernel(a_ref, c_ref, h_ref, x_ref, w_ref, g_ref, gnext_ref, o_ref, xn_ref):
    halves = _row_halves(x_ref.shape[0])
    mixes = []
    for rows in halves:
        mix = _dot(a_ref[rows, :], w_ref[0:ATTN_W, :])
        mix = mix + _dot(c_ref[rows, :], w_ref[ATTN_W:ATTN_W + CONV_W, :])
        mixes.append(mix + _dot(h_ref[rows, :], w_ref[ATTN_W + CONV_W:, :]))
    for rows, mix in zip(halves, mixes):
        x = x_ref[rows, :] + _rms(mix, g_ref[...])
        o_ref[rows, :] = x
        xn_ref[rows, :] = _rms(x, gnext_ref[...]).astype(xn_ref.dtype)


def _out_proj(a, cv, hg, x, w, layer, g, g_next):
    t, d = x.shape
    tm = _tile(t, 512)
    mixw = w.shape[1]
    return pl.pallas_call(
        _out_proj_kernel,
        grid=(t // tm,),
        in_specs=[
            pl.BlockSpec((tm, ATTN_W), lambda i: (i, 0)),
            pl.BlockSpec((tm, CONV_W), lambda i: (i, 0)),
            pl.BlockSpec((tm, HGRN_W), lambda i: (i, 0)),
            pl.BlockSpec((tm, d), lambda i: (i, 0)),
            pl.BlockSpec((None, mixw, d), lambda i: (layer, 0, 0)),
            pl.BlockSpec((1, d), lambda i: (0, 0)),
            pl.BlockSpec((1, d), lambda i: (0, 0)),
        ],
        out_specs=[pl.BlockSpec((tm, d), lambda i: (i, 0)), pl.BlockSpec((tm, d), lambda i: (i, 0))],
        out_shape=[jax.ShapeDtypeStruct((t, d), F32), jax.ShapeDtypeStruct((t, d), BF16)],
        compiler_params=_params("parallel"),
        name="out_proj",
    )(a, cv, hg, x, w, g.reshape(1, d), g_next.reshape(1, d))


def _ffn_kernel(xn_ref, wg_ref, wu_ref, wo_ref, gpost_ref, o_ref, acc_ref):
    j = pl.program_id(1)

    @pl.when(j == 0)
    def _():
        acc_ref[...] = jnp.zeros_like(acc_ref)

    xn = xn_ref[...]
    hidden = _silu(_dot(xn, wg_ref[...])) * _dot(xn, wu_ref[...])
    acc_ref[...] += _dot(hidden.astype(BF16), wo_ref[...])

    @pl.when(j == pl.num_programs(1) - 1)
    def _():
        o_ref[...] = _rms(acc_ref[...], gpost_ref[...]).astype(o_ref.dtype)


def _ffn(xn, w_in, w_out, layer, gpost):
    t, d = xn.shape
    f = w_out.shape[1]
    tm, th = _tile(t, 1024), _tile(f, 512)
    nh = f // th
    return pl.pallas_call(
        _ffn_kernel,
        grid=(t // tm, nh),
        in_specs=[
            pl.BlockSpec((tm, d), lambda i, j: (i, 0)),
            pl.BlockSpec((None, d, th), lambda i, j: (layer, 0, j)),
            pl.BlockSpec((None, d, th), lambda i, j: (layer, 0, nh + j)),
            pl.BlockSpec((None, th, d), lambda i, j: (layer, j, 0)),
            pl.BlockSpec((1, d), lambda i, j: (0, 0)),
        ],
        out_specs=pl.BlockSpec((tm, d), lambda i, j: (i, 0)),
        out_shape=jax.ShapeDtypeStruct((t, d), BF16),
        scratch_shapes=[pltpu.VMEM((tm, d), F32)],
        compiler_params=_params("parallel", "arbitrary"),
        name="ffn",
    )(xn, w_in, w_in, w_out, gpost.reshape(1, d))


def _ple_kernel(x_ref, y_ref, p_ref, wg_ref, wp_ref, g_ref, o_ref):
    halves = _row_halves(x_ref.shape[0])
    parts = []
    for rows in halves:
        x = x_ref[rows, :] + y_ref[rows, :].astype(F32)
        parts.append((x, _dot(x.astype(BF16), wg_ref[...]), _dot(p_ref[rows, :].astype(BF16), wp_ref[...])))
    for rows, (x, gate_logit, proj) in zip(halves, parts):
        o_ref[rows, :] = x + _rms(proj * _sigmoid(gate_logit), g_ref[...])


def _ple(x, y, p, layer, w_gate, w_proj, g):
    t, d = x.shape
    pd = p.shape[-1]
    tm = _tile(t, 512)
    return pl.pallas_call(
        _ple_kernel,
        grid=(t // tm,),
        in_specs=[
            pl.BlockSpec((tm, d), lambda i: (i, 0)),
            pl.BlockSpec((tm, d), lambda i: (i, 0)),
            pl.BlockSpec((None, tm, pd), lambda i: (layer, i, 0)),
            pl.BlockSpec((None, d, d), lambda i: (layer, 0, 0)),
            pl.BlockSpec((None, pd, d), lambda i: (layer, 0, 0)),
            pl.BlockSpec((1, d), lambda i: (0, 0)),
        ],
        out_specs=pl.BlockSpec((tm, d), lambda i: (i, 0)),
        out_shape=jax.ShapeDtypeStruct((t, d), F32),
        compiler_params=_params("parallel"),
        name="ple",
    )(x, y, p, w_gate, w_proj, g.reshape(1, d))


def _trunk(x, p, lb, layers, mats):
    depth = p.shape[0]
    batch, n, d = x.shape
    t = batch * n
    x = x.reshape(t, d)
    p = p.reshape(depth, t, p.shape[-1])
    cos_l, sin_l = _rope_tables(n)
    for l, w in enumerate(layers):
        u = _in_proj(x, w["norm_mix_pre"], mats["w_in"], l)
        qk_r, v_t = _attn_prep(u, cos_l, sin_l, w["q_norm"], w["k_norm"], n)
        a_out = _flash(qk_r, v_t, w["q_norm"], w["k_norm"], batch, n)
        c_out = _conv(u, w["conv_dw"], w["conv_dw_b"], w["conv_ln_g"], w["conv_ln_b"], mats["conv_pw"], l, n)
        o_fwd = _hgrn_pass(u, lb[l, 0], batch, n, reverse=False)
        h_out = _hgrn_pass(u, lb[l, 1], batch, n, reverse=True, o_prev=o_fwd, gn=w["hgrn_gn"])
        x, xn = _out_proj(a_out, c_out, h_out, x, mats["w_out"], l, w["norm_mix_post"], w["norm_ffn_pre"])
        y = _ffn(xn, mats["w_ffn_in"], mats["w_ffn_out"], l, w["norm_ffn_post"])
        x = _ple(x, y, p, l, mats["w_ple_gate"], mats["w_ple_proj"], w["ple_norm"])
    return x.reshape(batch, n, d)


def kernel(x_prompt, x_sample, p_prompt, p_sample, norm_mix_pre, norm_mix_post, w_in, q_norm, k_norm, conv_dw, conv_dw_b, conv_ln_g, conv_ln_b, conv_pw, hgrn_lb, hgrn_gn, w_out, norm_ffn_pre, norm_ffn_post, w_ffn_in, w_ffn_out, w_ple_gate, w_ple_proj, ple_norm):
    sm = jax.nn.softmax(hgrn_lb.astype(F32), axis=0)
    lb = jnp.cumsum(sm, axis=0) - sm[0]
    vectors = dict(
        norm_mix_pre=norm_mix_pre, norm_mix_post=norm_mix_post, q_norm=q_norm, k_norm=k_norm,
        conv_dw=conv_dw, conv_dw_b=conv_dw_b, conv_ln_g=conv_ln_g, conv_ln_b=conv_ln_b,
        hgrn_gn=hgrn_gn, norm_ffn_pre=norm_ffn_pre, norm_ffn_post=norm_ffn_post, ple_norm=ple_norm,
    )
    matrices = dict(
        w_in=w_in, conv_pw=conv_pw, w_out=w_out, w_ffn_in=w_ffn_in, w_ffn_out=w_ffn_out,
        w_ple_gate=w_ple_gate, w_ple_proj=w_ple_proj,
    )
    mats = {k: v.astype(BF16) for k, v in matrices.items()}
    layers = [{k: v[l] for k, v in vectors.items()} for l in range(p_prompt.shape[0])]
    y_prompt = _trunk(x_prompt, p_prompt, lb, layers, mats)
    y_sample = _trunk(x_sample, p_sample, lb, layers, mats)
    return (y_prompt, y_sample)
```

```python
import functools

import numpy as np
import jax
import jax.numpy as jnp
from jax import lax
from jax.experimental import pallas as pl
from jax.experimental.pallas import tpu as pltpu

F32 = jnp.float32
BF16 = jnp.bfloat16

NORM_EPS = 1e-6
ROPE_THETA = 10000.0
LOG2_E = 1.4426950408889634
GRID_W = 64
HEAD_DIM = 128
N_Q_HEADS = 8
N_KV_HEADS = 2
Q_GROUP = N_Q_HEADS // N_KV_HEADS
ATTN_W = N_Q_HEADS * HEAD_DIM
KV_W = N_KV_HEADS * HEAD_DIM
CONV_W = 512
CONV_K = 31
CONV_HALO = 16
SUBLANES = 8
HGRN_W = 512
HGRN_D = 128
HGRN_HEADS = HGRN_W // HGRN_D
HGRN_CHUNK = 128
HGRN_CHUNKS_PER_STEP = 8
IN_PROJ_NORM_PARTS = 4
FLASH_ONES_ROWS = 16
FLASH_LOOKAHEAD = 3
FLASH_TQ = 512
FLASH_TK = 8192
FLASH_SUB = 512
FLASH_FIXED_SHIFT_MAX_RANGE = 96.0

OFF_Q = 0
OFF_K = OFF_Q + ATTN_W
OFF_V = OFF_K + KV_W
OFF_CA = OFF_V + KV_W
OFF_CG = OFF_CA + CONV_W
OFF_HQ = OFF_CG + CONV_W
OFF_FF = OFF_HQ + HGRN_W
OFF_FB = OFF_FF + HGRN_W
OFF_HI = OFF_FB + HGRN_W
OFF_HG = OFF_HI + HGRN_W
IN_COLS = OFF_HG + HGRN_W

V7X_VMEM_LIMIT_BYTES = 56 * 1024 * 1024


def _params(*semantics):
    return pltpu.CompilerParams(dimension_semantics=semantics, vmem_limit_bytes=V7X_VMEM_LIMIT_BYTES)


def _tile(dim, pref):
    t = min(dim, pref)
    assert dim % t == 0, (dim, pref)
    return t


def _row_halves(rows):
    half = rows // 2
    return [slice(0, half), slice(half, rows)] if half % 16 == 0 else [slice(0, rows)]


def _rms(x, g):
    return x * lax.rsqrt(jnp.mean(x * x, axis=-1, keepdims=True) + NORM_EPS) * g


def _sigmoid(x):
    return 1.0 / (1.0 + jnp.exp(-x))


def _silu(x):
    return x * _sigmoid(x)


def _dot(a, b):
    return jnp.dot(a, b, preferred_element_type=F32)


def _dot_nt(a, b):
    return lax.dot_general(a, b, (((1,), (1,)), ((), ())), preferred_element_type=F32)


def _in_proj_kernel(x_ref, g_ref, w_ref, o_ref, xn_ref):
    j = pl.program_id(1)

    @pl.when(j == 0)
    def _():
        tm = x_ref.shape[0]
        part = tm // IN_PROJ_NORM_PARTS if tm % (IN_PROJ_NORM_PARTS * 16) == 0 else tm
        for r0 in range(0, tm, part):
            xn = _rms(x_ref[r0:r0 + part, :], g_ref[...]).astype(BF16)
            xn_ref[r0:r0 + part, :] = xn
            o_ref[r0:r0 + part, :] = _dot(xn, w_ref[...]).astype(o_ref.dtype)

    @pl.when(j > 0)
    def _():
        o_ref[...] = _dot(xn_ref[...], w_ref[...]).astype(o_ref.dtype)


def _in_proj(x, g, w, layer):
    t, d = x.shape
    n = w.shape[-1]
    tm, tn = _tile(t, 1024), _tile(n, 1280)
    return pl.pallas_call(
        _in_proj_kernel,
        grid=(t // tm, n // tn),
        in_specs=[
            pl.BlockSpec((tm, d), lambda i, j: (i, 0)),
            pl.BlockSpec((1, d), lambda i, j: (0, 0)),
            pl.BlockSpec((None, d, tn), lambda i, j: (layer, 0, j)),
        ],
        out_specs=pl.BlockSpec((tm, tn), lambda i, j: (i, j)),
        out_shape=jax.ShapeDtypeStruct((t, n), BF16),
        scratch_shapes=[pltpu.VMEM((tm, d), BF16)],
        compiler_params=_params("parallel", "arbitrary"),
        name="in_proj",
    )(x, g.reshape(1, d), w)


def _rope_tables(n):
    rows = n // GRID_W
    row = jnp.repeat(jnp.arange(rows, dtype=F32), GRID_W)
    col = jnp.tile(jnp.arange(GRID_W, dtype=F32), rows)
    axis_dims = HEAD_DIM // 2
    inv = ROPE_THETA ** (-jnp.arange(0, axis_dims, 2, dtype=F32) / axis_dims)
    ang = jnp.concatenate([row[:, None] * inv, col[:, None] * inv], axis=-1)
    cos, sin = jnp.cos(ang), jnp.sin(ang)
    cos_l = jnp.repeat(cos, 2, axis=-1)
    sin_l = jnp.stack([-sin, sin], axis=-1).reshape(n, HEAD_DIM)
    return cos_l, sin_l


def _attn_prep_kernel(q_ref, k_ref, v_ref, cos_ref, sin_ref, qg_ref, kg_ref, qk_ref, vt_ref):
    cos = cos_ref[...]
    sin = sin_ref[...]
    lane = lax.broadcasted_iota(jnp.int32, cos.shape, 1)
    even = (lane & 1) == 0

    def norm_rope(x, g):
        y = _rms(x.astype(F32), g)
        partner = jnp.where(even, pltpu.roll(y, HEAD_DIM - 1, 1), pltpu.roll(y, 1, 1))
        return y * cos + partner * sin

    scale = HEAD_DIM ** -0.5 * LOG2_E
    for h in range(N_Q_HEADS):
        sl = slice(h * HEAD_DIM, (h + 1) * HEAD_DIM)
        qk_ref[:, sl] = (norm_rope(q_ref[:, sl], qg_ref[...]) * scale).astype(qk_ref.dtype)
    for h in range(N_KV_HEADS):
        sl = slice(h * HEAD_DIM, (h + 1) * HEAD_DIM)
        so = slice(ATTN_W + h * HEAD_DIM, ATTN_W + (h + 1) * HEAD_DIM)
        qk_ref[:, so] = norm_rope(k_ref[:, sl], kg_ref[...]).astype(qk_ref.dtype)
    rows = HEAD_DIM + FLASH_ONES_ROWS
    for h in range(N_KV_HEADS):
        vt_ref[h * rows:h * rows + HEAD_DIM, :] = (
            v_ref[:, h * HEAD_DIM:(h + 1) * HEAD_DIM].astype(F32).T.astype(vt_ref.dtype))
        vt_ref[h * rows + HEAD_DIM:(h + 1) * rows, :] = jnp.ones((FLASH_ONES_ROWS, v_ref.shape[0]), vt_ref.dtype)


def _attn_prep(u, cos_l, sin_l, q_gain, k_gain, n):
    t = u.shape[0]
    tm = _tile(n, 1024)
    per_seq = n // tm
    return pl.pallas_call(
        _attn_prep_kernel,
        grid=(t // tm,),
        in_specs=[
            pl.BlockSpec((tm, ATTN_W), lambda i: (i, OFF_Q // ATTN_W)),
            pl.BlockSpec((tm, KV_W), lambda i: (i, OFF_K // KV_W)),
            pl.BlockSpec((tm, KV_W), lambda i: (i, OFF_V // KV_W)),
            pl.BlockSpec((tm, HEAD_DIM), lambda i: (i % per_seq, 0)),
            pl.BlockSpec((tm, HEAD_DIM), lambda i: (i % per_seq, 0)),
            pl.BlockSpec((1, HEAD_DIM), lambda i: (0, 0)),
            pl.BlockSpec((1, HEAD_DIM), lambda i: (0, 0)),
        ],
        out_specs=[
            pl.BlockSpec((tm, ATTN_W + KV_W), lambda i: (i, 0)),
            pl.BlockSpec((N_KV_HEADS * (HEAD_DIM + FLASH_ONES_ROWS), tm), lambda i: (0, i)),
        ],
        out_shape=[
            jax.ShapeDtypeStruct((t, ATTN_W + KV_W), BF16),
            jax.ShapeDtypeStruct((N_KV_HEADS * (HEAD_DIM + FLASH_ONES_ROWS), t), BF16),
        ],
        compiler_params=_params("parallel"),
        name="attn_prep",
    )(u, u, u, cos_l, sin_l, q_gain.reshape(1, HEAD_DIM), k_gain.reshape(1, HEAD_DIM))


def _flash_kernel(shift_ref, q_ref, k_ref, vt_ref, o_ref, m_ref, acc_ref, *, tk, sub, fixed_shift):
    ki = pl.program_id(3)

    @pl.when(ki == 0)
    def _():
        m_ref[...] = jnp.full_like(m_ref, -jnp.inf)
        acc_ref[...] = jnp.zeros_like(acc_ref)

    stages = [(s0, g) for s0 in range(0, tk, sub) for g in range(Q_GROUP)]

    def scores(stage):
        s0, g = stage
        return _dot_nt(k_ref[s0:s0 + sub, :], q_ref[:, g * HEAD_DIM:(g + 1) * HEAD_DIM])

    shift = shift_ref[0, 0]
    partial = [None] * Q_GROUP

    pending = [scores(st) for st in stages[:FLASH_LOOKAHEAD]]
    for i, (s0, g) in enumerate(stages):
        s = pending.pop(0)
        if i + FLASH_LOOKAHEAD < len(stages):
            pending.append(scores(stages[i + FLASH_LOOKAHEAD]))
        vt1 = vt_ref[:, s0:s0 + sub]
        if fixed_shift:
            pv = _dot(vt1, jnp.exp2(s - shift).astype(BF16))
            partial[g] = pv if partial[g] is None else partial[g] + pv
        else:
            m_prev = m_ref[g]
            m_new = jnp.maximum(m_prev, jnp.max(s, axis=0, keepdims=True))
            alpha = jnp.exp2(m_prev - m_new)
            p = jnp.exp2((s - m_new).astype(BF16))
            acc_ref[g] = alpha * acc_ref[g] + _dot(vt1, p)
            m_ref[g] = m_new
    if fixed_shift:
        for g in range(Q_GROUP):
            acc_ref[g] += partial[g]

    @pl.when(ki == pl.num_programs(3) - 1)
    def _():
        for g in range(Q_GROUP):
            out = acc_ref[g, 0:HEAD_DIM, :] / acc_ref[g, HEAD_DIM:HEAD_DIM + 1, :]
            o_ref[:, g * HEAD_DIM:(g + 1) * HEAD_DIM] = out.T.astype(o_ref.dtype)


def _flash(qk, vt, q_gain, k_gain, batch, n):
    t = qk.shape[0]
    tq, tk = _tile(n, FLASH_TQ), _tile(n, FLASH_TK)
    sub = _tile(tk, FLASH_SUB)
    nq, nk = n // tq, n // tk
    gw = Q_GROUP * HEAD_DIM
    bound = (HEAD_DIM ** 0.5 * LOG2_E) * jnp.max(jnp.abs(q_gain)) * jnp.max(jnp.abs(k_gain))
    shift = bound.astype(F32).reshape(1, 1)

    def call(fixed_shift):
        return pl.pallas_call(
            functools.partial(_flash_kernel, tk=tk, sub=sub, fixed_shift=fixed_shift),
            grid=(batch, N_KV_HEADS, nq, nk),
            in_specs=[
                pl.BlockSpec(memory_space=pltpu.SMEM),
                pl.BlockSpec((tq, gw), lambda b, h, qi, ki: (b * nq + qi, h)),
                pl.BlockSpec((tk, HEAD_DIM), lambda b, h, qi, ki: (b * nk + ki, N_Q_HEADS + h)),
                pl.BlockSpec((HEAD_DIM + FLASH_ONES_ROWS, tk), lambda b, h, qi, ki: (h, b * nk + ki)),
            ],
            out_specs=pl.BlockSpec((tq, gw), lambda b, h, qi, ki: (b * nq + qi, h)),
            out_shape=jax.ShapeDtypeStruct((t, ATTN_W), BF16),
            scratch_shapes=[
                pltpu.VMEM((Q_GROUP, 1, tq), F32),
                pltpu.VMEM((Q_GROUP, HEAD_DIM + FLASH_ONES_ROWS, tq), F32),
            ],
            compiler_params=_params("parallel", "parallel", "parallel", "arbitrary"),
            name="flash_attn_fixed" if fixed_shift else "flash_attn",
        )(shift, qk, qk, vt)

    return lax.cond(2.0 * bound <= FLASH_FIXED_SHIFT_MAX_RANGE, lambda: call(True), lambda: call(False))


def _conv_kernel(a_ref, g_ref, ap_ref, gp_ref, an_ref, gn_ref, dw_ref, dwb_ref, lng_ref, lnb_ref, pw_ref,
                 o_ref, h_ref, sh_ref, *, tm, per_seq):
    i = pl.program_id(0)
    pos = i % per_seq

    def glu(a, g):
        return a.astype(F32) * _sigmoid(g.astype(F32))

    prev = jnp.where(pos == 0, 0.0, glu(ap_ref[...], gp_ref[...]))
    nxt = jnp.where(pos == per_seq - 1, 0.0, glu(an_ref[...], gn_ref[...]))
    h_ref[0:CONV_HALO, :] = prev
    h_ref[CONV_HALO:CONV_HALO + tm, :] = glu(a_ref[...], g_ref[...])
    h_ref[CONV_HALO + tm:, :] = nxt

    base = CONV_HALO - CONV_K // 2
    span = tm + 2 * CONV_HALO - SUBLANES
    for b in range(SUBLANES):
        sh_ref[b] = h_ref[b:b + span, :]
    acc = jnp.zeros((tm, CONV_W), F32) + dwb_ref[...]
    for j in range(CONV_K):
        a, b = divmod(base + j, SUBLANES)
        acc = acc + dw_ref[j:j + 1, :] * sh_ref[b, a * SUBLANES:a * SUBLANES + tm, :]
    mu = jnp.mean(acc, axis=-1, keepdims=True)
    cen = acc - mu
    var = jnp.mean(cen * cen, axis=-1, keepdims=True)
    y = cen * lax.rsqrt(var + NORM_EPS) * lng_ref[...] + lnb_ref[...]
    o_ref[...] = _dot(_silu(y).astype(BF16), pw_ref[...]).astype(o_ref.dtype)


def _conv(u, dw, dwb, lng, lnb, pw, layer, n):
    t = u.shape[0]
    tm = _tile(n, 1024)
    per_seq = n // tm
    hb = tm // CONV_HALO
    last = t // CONV_HALO - 1
    ca, cg = OFF_CA // CONV_W, OFF_CG // CONV_W
    vec = lambda v: v.reshape(1, CONV_W)
    row = pl.BlockSpec((1, CONV_W), lambda i: (0, 0))
    return pl.pallas_call(
        functools.partial(_conv_kernel, tm=tm, per_seq=per_seq),
        grid=(t // tm,),
        in_specs=[
            pl.BlockSpec((tm, CONV_W), lambda i: (i, ca)),
            pl.BlockSpec((tm, CONV_W), lambda i: (i, cg)),
            pl.BlockSpec((CONV_HALO, CONV_W), lambda i: (jnp.maximum(i * hb - 1, 0), ca)),
            pl.BlockSpec((CONV_HALO, CONV_W), lambda i: (jnp.maximum(i * hb - 1, 0), cg)),
            pl.BlockSpec((CONV_HALO, CONV_W), lambda i: (jnp.minimum((i + 1) * hb, last), ca)),
            pl.BlockSpec((CONV_HALO, CONV_W), lambda i: (jnp.minimum((i + 1) * hb, last), cg)),
            pl.BlockSpec((CONV_K, CONV_W), lambda i: (0, 0)),
            row, row, row,
            pl.BlockSpec((None, CONV_W, CONV_W), lambda i: (layer, 0, 0)),
        ],
        out_specs=pl.BlockSpec((tm, CONV_W), lambda i: (i, 0)),
        out_shape=jax.ShapeDtypeStruct((t, CONV_W), BF16),
        scratch_shapes=[
            pltpu.VMEM((tm + 2 * CONV_HALO, CONV_W), F32),
            pltpu.VMEM((SUBLANES, tm + 2 * CONV_HALO - SUBLANES, CONV_W), F32),
        ],
        compiler_params=_params("parallel"),
        name="conv_mixer",
    )(u, u, u, u, u, u, dw, vec(dwb), vec(lng), vec(lnb), pw)


def _hgrn_tables(reverse):
    c = HGRN_CHUNK
    levels = []
    h = c // 2
    while h >= 1:
        levels.append(h)
        h //= 2
    r = np.arange(c)
    col = r[None, :]
    seg = np.zeros((len(levels) + 2, c, c), np.float32)
    mask = np.zeros((len(levels) + 1, c, c), np.float32)
    for li, h in enumerate(levels):
        blk = r // (2 * h)
        upper = (r % (2 * h)) >= h
        b = (blk * 2 * h + h - 1)[:, None]
        rr = r[:, None]
        seg[li] = np.where(upper[:, None], (col > b) & (col <= rr), (col > rr) & (col <= b))
        mask[li] = upper[:, None] & (~upper)[None, :] & (blk[:, None] == blk[None, :])
    seg[-2] = col <= r[:, None]
    seg[-1] = col > r[:, None]
    mask[-1] = np.eye(c)
    if reverse:
        seg = seg[:, ::-1, ::-1]
        mask = mask[:, ::-1, ::-1]
    seg = seg.reshape(-1, c)
    seg2 = np.concatenate([seg, seg], axis=1)
    return (jnp.asarray(seg2, dtype=BF16), jnp.asarray(mask, dtype=F32), len(levels))


def _hgrn_kernel(*refs, reverse, final, n_levels, n_chunks):
    if final:
        hq_ref, hf_ref, hi_ref, lb_ref, seg_ref, mask_ref, oprev_ref, hg_ref, gn_ref, o_ref, st_ref = refs
    else:
        hq_ref, hf_ref, hi_ref, lb_ref, seg_ref, mask_ref, o_ref, st_ref = refs
    c = HGRN_CHUNK

    @pl.when(pl.program_id(1) == 0)
    def _():
        st_ref[...] = jnp.zeros_like(st_ref)

    lb = lb_ref[...]
    seg = seg_ref[...]
    total_row = 0 if reverse else c - 1

    def prep(j):
        rows = slice(j * c, (j + 1) * c)
        f = lb + (1.0 - lb) * _sigmoid(hf_ref[rows, :].astype(F32))
        log2f = jnp.log(f) * LOG2_E
        g_hi = log2f.astype(BF16)
        g_lo = (log2f - g_hi.astype(F32)).astype(BF16)
        e_all = jnp.exp2(_dot(seg, jnp.concatenate([g_hi, g_lo], axis=0)))
        decay = e_all[n_levels * c + total_row:n_levels * c + total_row + 1, :]
        return _silu(hq_ref[rows, :].astype(F32)).astype(BF16), (1.0 - f).astype(BF16), e_all.astype(BF16), decay

    def intra(prepped):
        qq, kk, e_all, _ = prepped
        mats = []
        for hd in range(HGRN_HEADS):
            sl = slice(hd * HGRN_D, (hd + 1) * HGRN_D)
            qh, kh = qq[:, sl], kk[:, sl]
            a = mask_ref[n_levels] * _dot_nt(qh, kh)
            for l in range(n_levels):
                el = e_all[l * c:(l + 1) * c, sl]
                a = a + mask_ref[l] * _dot_nt(qh * el, kh * el)
            mats.append(a.astype(BF16))
        return mats

    def outputs(j, prepped, mats):
        qq, kk, e_all, decay = prepped
        rows = slice(j * c, (j + 1) * c)
        if final:
            gate = _silu(hg_ref[rows, :].astype(F32)) * gn_ref[...]
        for hd in range(HGRN_HEADS):
            sl = slice(hd * HGRN_D, (hd + 1) * HGRN_D)
            qh, kh, vh = qq[:, sl], kk[:, sl], hi_ref[rows, sl]
            e_pre = e_all[n_levels * c:(n_levels + 1) * c, sl]
            e_suf = e_all[(n_levels + 1) * c:(n_levels + 2) * c, sl]
            st = st_ref[hd]
            o = _dot(mats[hd], vh) + _dot_nt(qh * e_pre, st.astype(BF16))
            v_t = vh.astype(F32).T.astype(BF16)
            st_ref[hd] = decay[:, sl] * st + _dot(v_t, kh * e_suf)
            if final:
                o = o + oprev_ref[rows, sl]
                o = o * lax.rsqrt(jnp.mean(o * o, axis=-1, keepdims=True) + NORM_EPS)
                o_ref[rows, sl] = (o * gate[:, sl]).astype(o_ref.dtype)
            else:
                o_ref[rows, sl] = o

    order = list(range(n_chunks))[::-1] if reverse else list(range(n_chunks))
    prepped, mats = {}, {}
    for step in range(n_chunks + 2):
        if step < n_chunks:
            prepped[step] = prep(order[step])
        if 0 <= step - 1 < n_chunks:
            mats[step - 1] = intra(prepped[step - 1])
        if 0 <= step - 2 < n_chunks:
            outputs(order[step - 2], prepped.pop(step - 2), mats.pop(step - 2))


def _hgrn_pass(u, lb, batch, n, *, reverse, o_prev=None, gn=None):
    t = u.shape[0]
    n_chunks = _tile(n // HGRN_CHUNK, HGRN_CHUNKS_PER_STEP)
    c = n_chunks * HGRN_CHUNK
    nc = n // c
    final = o_prev is not None
    seg, mask, n_levels = _hgrn_tables(reverse)

    def rows(b, ci):
        return b * nc + (nc - 1 - ci if reverse else ci)

    def col_spec(off):
        return pl.BlockSpec((c, HGRN_W), lambda b, ci: (rows(b, ci), off // HGRN_W))

    const2 = lambda b, ci: (0, 0)
    in_specs = [
        col_spec(OFF_HQ),
        col_spec(OFF_FB if reverse else OFF_FF),
        col_spec(OFF_HI),
        pl.BlockSpec((1, HGRN_W), const2),
        pl.BlockSpec(seg.shape, const2),
        pl.BlockSpec(mask.shape, lambda b, ci: (0, 0, 0)),
    ]
    args = [u, u, u, lb.reshape(1, HGRN_W), seg, mask]
    if final:
        in_specs += [
            pl.BlockSpec((c, HGRN_W), lambda b, ci: (rows(b, ci), 0)),
            col_spec(OFF_HG),
            pl.BlockSpec((1, HGRN_W), const2),
        ]
        args += [o_prev, u, gn.reshape(1, HGRN_W)]
    return pl.pallas_call(
        functools.partial(_hgrn_kernel, reverse=reverse, final=final, n_levels=n_levels, n_chunks=n_chunks),
        grid=(batch, nc),
        in_specs=in_specs,
        out_specs=pl.BlockSpec((c, HGRN_W), lambda b, ci: (rows(b, ci), 0)),
        out_shape=jax.ShapeDtypeStruct((t, HGRN_W), BF16 if final else F32),
        scratch_shapes=[pltpu.VMEM((HGRN_HEADS, HGRN_D, HGRN_D), F32)],
        compiler_params=_params("parallel", "arbitrary"),
        name="hgrn_bwd" if reverse else "hgrn_fwd",
    )(*args)


def _out_proj_kernel(a_ref, c_ref, h_ref, x_ref, w_ref, g_ref, gnext_ref, o_ref, xn_ref):
    halves = _row_halves(x_ref.shape[0])
    mixes = []
    for rows in halves:
        mix = _dot(a_ref[rows, :], w_ref[0:ATTN_W, :])
        mix = mix + _dot(c_ref[rows, :], w_ref[ATTN_W:ATTN_W + CONV_W, :])
        mixes.append(mix + _dot(h_ref[rows, :], w_ref[ATTN_W + CONV_W:, :]))
    for rows, mix in zip(halves, mixes):
        x = x_ref[rows, :] + _rms(mix, g_ref[...])
        o_ref[rows, :] = x
        xn_ref[rows, :] = _rms(x, gnext_ref[...]).astype(xn_ref.dtype)


def _out_proj(a, cv, hg, x, w, layer, g, g_next):
    t, d = x.shape
    tm = _tile(t, 512)
    mixw = w.shape[1]
    return pl.pallas_call(
        _out_proj_kernel,
        grid=(t // tm,),
        in_specs=[
            pl.BlockSpec((tm, ATTN_W), lambda i: (i, 0)),
            pl.BlockSpec((tm, CONV_W), lambda i: (i, 0)),
            pl.BlockSpec((tm, HGRN_W), lambda i: (i, 0)),
            pl.BlockSpec((tm, d), lambda i: (i, 0)),
            pl.BlockSpec((None, mixw, d), lambda i: (layer, 0, 0)),
            pl.BlockSpec((1, d), lambda i: (0, 0)),
            pl.BlockSpec((1, d), lambda i: (0, 0)),
        ],
        out_specs=[pl.BlockSpec((tm, d), lambda i: (i, 0)), pl.BlockSpec((tm, d), lambda i: (i, 0))],
        out_shape=[jax.ShapeDtypeStruct((t, d), F32), jax.ShapeDtypeStruct((t, d), BF16)],
        compiler_params=_params("parallel"),
        name="out_proj",
    )(a, cv, hg, x, w, g.reshape(1, d), g_next.reshape(1, d))


def _ffn_kernel(xn_ref, wg_ref, wu_ref, wo_ref, gpost_ref, o_ref, acc_ref):
    j = pl.program_id(1)

    @pl.when(j == 0)
    def _():
        acc_ref[...] = jnp.zeros_like(acc_ref)

    xn = xn_ref[...]
    hidden = _silu(_dot(xn, wg_ref[...])) * _dot(xn, wu_ref[...])
    acc_ref[...] += _dot(hidden.astype(BF16), wo_ref[...])

    @pl.when(j == pl.num_programs(1) - 1)
    def _():
        o_ref[...] = _rms(acc_ref[...], gpost_ref[...]).astype(o_ref.dtype)


def _ffn(xn, w_in, w_out, layer, gpost):
    t, d = xn.shape
    f = w_out.shape[1]
    tm, th = _tile(t, 1024), _tile(f, 512)
    nh = f // th
    return pl.pallas_call(
        _ffn_kernel,
        grid=(t // tm, nh),
        in_specs=[
            pl.BlockSpec((tm, d), lambda i, j: (i, 0)),
            pl.BlockSpec((None, d, th), lambda i, j: (layer, 0, j)),
            pl.BlockSpec((None, d, th), lambda i, j: (layer, 0, nh + j)),
            pl.BlockSpec((None, th, d), lambda i, j: (layer, j, 0)),
            pl.BlockSpec((1, d), lambda i, j: (0, 0)),
        ],
        out_specs=pl.BlockSpec((tm, d), lambda i, j: (i, 0)),
        out_shape=jax.ShapeDtypeStruct((t, d), BF16),
        scratch_shapes=[pltpu.VMEM((tm, d), F32)],
        compiler_params=_params("parallel", "arbitrary"),
        name="ffn",
    )(xn, w_in, w_in, w_out, gpost.reshape(1, d))


def _ple_kernel(x_ref, y_ref, p_ref, wg_ref, wp_ref, g_ref, o_ref):
    halves = _row_halves(x_ref.shape[0])
    parts = []
    for rows in halves:
        x = x_ref[rows, :] + y_ref[rows, :].astype(F32)
        parts.append((x, _dot(x.astype(BF16), wg_ref[...]), _dot(p_ref[rows, :].astype(BF16), wp_ref[...])))
    for rows, (x, gate_logit, proj) in zip(halves, parts):
        o_ref[rows, :] = x + _rms(proj * _sigmoid(gate_logit), g_ref[...])


def _ple(x, y, p, layer, w_gate, w_proj, g):
    t, d = x.shape
    pd = p.shape[-1]
    tm = _tile(t, 512)
    return pl.pallas_call(
        _ple_kernel,
        grid=(t // tm,),
        in_specs=[
            pl.BlockSpec((tm, d), lambda i: (i, 0)),
            pl.BlockSpec((tm, d), lambda i: (i, 0)),
            pl.BlockSpec((None, tm, pd), lambda i: (layer, i, 0)),
            pl.BlockSpec((None, d, d), lambda i: (layer, 0, 0)),
            pl.BlockSpec((None, pd, d), lambda i: (layer, 0, 0)),
            pl.BlockSpec((1, d), lambda i: (0, 0)),
        ],
        out_specs=pl.BlockSpec((tm, d), lambda i: (i, 0)),
        out_shape=jax.ShapeDtypeStruct((t, d), F32),
        compiler_params=_params("parallel"),
        name="ple",
    )(x, y, p, w_gate, w_proj, g.reshape(1, d))


def _trunk(x, p, lb, layers, mats):
    depth = p.shape[0]
    batch, n, d = x.shape
    t = batch * n
    x = x.reshape(t, d)
    p = p.reshape(depth, t, p.shape[-1])
    cos_l, sin_l = _rope_tables(n)
    for l, w in enumerate(layers):
        u = _in_proj(x, w["norm_mix_pre"], mats["w_in"], l)
        qk_r, v_t = _attn_prep(u, cos_l, sin_l, w["q_norm"], w["k_norm"], n)
        a_out = _flash(qk_r, v_t, w["q_norm"], w["k_norm"], batch, n)
        c_out = _conv(u, w["conv_dw"], w["conv_dw_b"], w["conv_ln_g"], w["conv_ln_b"], mats["conv_pw"], l, n)
        o_fwd = _hgrn_pass(u, lb[l, 0], batch, n, reverse=False)
        h_out = _hgrn_pass(u, lb[l, 1], batch, n, reverse=True, o_prev=o_fwd, gn=w["hgrn_gn"])
        x, xn = _out_proj(a_out, c_out, h_out, x, mats["w_out"], l, w["norm_mix_post"], w["norm_ffn_pre"])
        y = _ffn(xn, mats["w_ffn_in"], mats["w_ffn_out"], l, w["norm_ffn_post"])
        x = _ple(x, y, p, l, mats["w_ple_gate"], mats["w_ple_proj"], w["ple_norm"])
    return x.reshape(batch, n, d)


def kernel(x_prompt, x_sample, p_prompt, p_sample, norm_mix_pre, norm_mix_post, w_in, q_norm, k_norm, conv_dw, conv_dw_b, conv_ln_g, conv_ln_b, conv_pw, hgrn_lb, hgrn_gn, w_out, norm_ffn_pre, norm_ffn_post, w_ffn_in, w_ffn_out, w_ple_gate, w_ple_proj, ple_norm):
    sm = jax.nn.softmax(hgrn_lb.astype(F32), axis=0)
    lb = jnp.cumsum(sm, axis=0) - sm[0]
    vectors = dict(
        norm_mix_pre=norm_mix_pre, norm_mix_post=norm_mix_post, q_norm=q_norm, k_norm=k_norm,
        conv_dw=conv_dw, conv_dw_b=conv_dw_b, conv_ln_g=conv_ln_g, conv_ln_b=conv_ln_b,
        hgrn_gn=hgrn_gn, norm_ffn_pre=norm_ffn_pre, norm_ffn_post=norm_ffn_post, ple_norm=ple_norm,
    )
    matrices = dict(
        w_in=w_in, conv_pw=conv_pw, w_out=w_out, w_ffn_in=w_ffn_in, w_ffn_out=w_ffn_out,
        w_ple_gate=w_ple_gate, w_ple_proj=w_ple_proj,
    )
    mats = {k: v.astype(BF16) for k, v in matrices.items()}
    layers = [{k: v[l] for k, v in vectors.items()} for l in range(p_prompt.shape[0])]
    y_prompt = _trunk(x_prompt, p_prompt, lb, layers, mats)
    y_sample = _trunk(x_sample, p_sample, lb, layers, mats)
    return (y_prompt, y_sample)
```

```python
import functools

import numpy as np
import jax
import jax.numpy as jnp
from jax import lax
from jax.experimental import pallas as pl
from jax.experimental.pallas import tpu as pltpu

F32 = jnp.float32
BF16 = jnp.bfloat16

NORM_EPS = 1e-6
ROPE_THETA = 10000.0
LOG2_E = 1.4426950408889634
GRID_W = 64
HEAD_DIM = 128
N_Q_HEADS = 8
N_KV_HEADS = 2
Q_GROUP = N_Q_HEADS // N_KV_HEADS
ATTN_W = N_Q_HEADS * HEAD_DIM
KV_W = N_KV_HEADS * HEAD_DIM
CONV_W = 512
CONV_K = 31
CONV_HALO = 16
SUBLANES = 8
HGRN_W = 512
HGRN_D = 128
HGRN_HEADS = HGRN_W // HGRN_D
HGRN_CHUNK = 128
HGRN_CHUNKS_PER_STEP = 8
IN_PROJ_NORM_PARTS = 4
FLASH_ONES_ROWS = 16
FLASH_LOOKAHEAD = 3
FLASH_TQ = 512
FLASH_TK = 8192
FLASH_SUB = 512
FLASH_FIXED_SHIFT_MAX_RANGE = 96.0

OFF_Q = 0
OFF_K = OFF_Q + ATTN_W
OFF_V = OFF_K + KV_W
OFF_CA = OFF_V + KV_W
OFF_CG = OFF_CA + CONV_W
OFF_HQ = OFF_CG + CONV_W
OFF_FF = OFF_HQ + HGRN_W
OFF_FB = OFF_FF + HGRN_W
OFF_HI = OFF_FB + HGRN_W
OFF_HG = OFF_HI + HGRN_W
IN_COLS = OFF_HG + HGRN_W

V7X_VMEM_LIMIT_BYTES = 56 * 1024 * 1024


def _params(*semantics):
    return pltpu.CompilerParams(dimension_semantics=semantics, vmem_limit_bytes=V7X_VMEM_LIMIT_BYTES)


def _tile(dim, pref):
    t = min(dim, pref)
    assert dim % t == 0, (dim, pref)
    return t


def _row_halves(rows):
    half = rows // 2
    return [slice(0, half), slice(half, rows)] if half % 16 == 0 else [slice(0, rows)]


def _rms(x, g):
    return x * lax.rsqrt(jnp.mean(x * x, axis=-1, keepdims=True) + NORM_EPS) * g


def _sigmoid(x):
    return 1.0 / (1.0 + jnp.exp(-x))


def _silu(x):
    return x * _sigmoid(x)


def _dot(a, b):
    return jnp.dot(a, b, preferred_element_type=F32)


def _dot_nt(a, b):
    return lax.dot_general(a, b, (((1,), (1,)), ((), ())), preferred_element_type=F32)


def _in_proj_kernel(x_ref, g_ref, w_ref, o_ref, xn_ref):
    j = pl.program_id(1)

    @pl.when(j == 0)
    def _():
        tm = x_ref.shape[0]
        part = tm // IN_PROJ_NORM_PARTS if tm % (IN_PROJ_NORM_PARTS * 16) == 0 else tm
        for r0 in range(0, tm, part):
            xn = _rms(x_ref[r0:r0 + part, :], g_ref[...]).astype(BF16)
            xn_ref[r0:r0 + part, :] = xn
            o_ref[r0:r0 + part, :] = _dot(xn, w_ref[...]).astype(o_ref.dtype)

    @pl.when(j > 0)
    def _():
        o_ref[...] = _dot(xn_ref[...], w_ref[...]).astype(o_ref.dtype)


def _in_proj(x, g, w, layer):
    t, d = x.shape
    n = w.shape[-1]
    tm, tn = _tile(t, 1024), _tile(n, 1280)
    return pl.pallas_call(
        _in_proj_kernel,
        grid=(t // tm, n // tn),
        in_specs=[
            pl.BlockSpec((tm, d), lambda i, j: (i, 0)),
            pl.BlockSpec((1, d), lambda i, j: (0, 0)),
            pl.BlockSpec((None, d, tn), lambda i, j: (layer, 0, j)),
        ],
        out_specs=pl.BlockSpec((tm, tn), lambda i, j: (i, j)),
        out_shape=jax.ShapeDtypeStruct((t, n), BF16),
        scratch_shapes=[pltpu.VMEM((tm, d), BF16)],
        compiler_params=_params("parallel", "arbitrary"),
        name="in_proj",
    )(x, g.reshape(1, d), w)


def _rope_tables(n):
    rows = n // GRID_W
    row = jnp.repeat(jnp.arange(rows, dtype=F32), GRID_W)
    col = jnp.tile(jnp.arange(GRID_W, dtype=F32), rows)
    axis_dims = HEAD_DIM // 2
    inv = ROPE_THETA ** (-jnp.arange(0, axis_dims, 2, dtype=F32) / axis_dims)
    ang = jnp.concatenate([row[:, None] * inv, col[:, None] * inv], axis=-1)
    cos, sin = jnp.cos(ang), jnp.sin(ang)
    cos_l = jnp.repeat(cos, 2, axis=-1)
    sin_l = jnp.stack([-sin, sin], axis=-1).reshape(n, HEAD_DIM)
    return cos_l, sin_l


def _attn_prep_kernel(q_ref, k_ref, v_ref, cos_ref, sin_ref, qg_ref, kg_ref, qk_ref, vt_ref):
    cos = cos_ref[...]
    sin = sin_ref[...]
    lane = lax.broadcasted_iota(jnp.int32, cos.shape, 1)
    even = (lane & 1) == 0

    def norm_rope(x, g):
        y = _rms(x.astype(F32), g)
        partner = jnp.where(even, pltpu.roll(y, HEAD_DIM - 1, 1), pltpu.roll(y, 1, 1))
        return y * cos + partner * sin

    scale = HEAD_DIM ** -0.5 * LOG2_E
    for h in range(N_Q_HEADS):
        sl = slice(h * HEAD_DIM, (h + 1) * HEAD_DIM)
        qk_ref[:, sl] = (norm_rope(q_ref[:, sl], qg_ref[...]) * scale).astype(qk_ref.dtype)
    for h in range(N_KV_HEADS):
        sl = slice(h * HEAD_DIM, (h + 1) * HEAD_DIM)
        so = slice(ATTN_W + h * HEAD_DIM, ATTN_W + (h + 1) * HEAD_DIM)
        qk_ref[:, so] = norm_rope(k_ref[:, sl], kg_ref[...]).astype(qk_ref.dtype)
    rows = HEAD_DIM + FLASH_ONES_ROWS
    for h in range(N_KV_HEADS):
        vt_ref[h * rows:h * rows + HEAD_DIM, :] = (
            v_ref[:, h * HEAD_DIM:(h + 1) * HEAD_DIM].astype(F32).T.astype(vt_ref.dtype))
        vt_ref[h * rows + HEAD_DIM:(h + 1) * rows, :] = jnp.ones((FLASH_ONES_ROWS, v_ref.shape[0]), vt_ref.dtype)


def _attn_prep(u, cos_l, sin_l, q_gain, k_gain, n):
    t = u.shape[0]
    tm = _tile(n, 1024)
    per_seq = n // tm
    return pl.pallas_call(
        _attn_prep_kernel,
        grid=(t // tm,),
        in_specs=[
            pl.BlockSpec((tm, ATTN_W), lambda i: (i, OFF_Q // ATTN_W)),
            pl.BlockSpec((tm, KV_W), lambda i: (i, OFF_K // KV_W)),
            pl.BlockSpec((tm, KV_W), lambda i: (i, OFF_V // KV_W)),
            pl.BlockSpec((tm, HEAD_DIM), lambda i: (i % per_seq, 0)),
            pl.BlockSpec((tm, HEAD_DIM), lambda i: (i % per_seq, 0)),
            pl.BlockSpec((1, HEAD_DIM), lambda i: (0, 0)),
            pl.BlockSpec((1, HEAD_DIM), lambda i: (0, 0)),
        ],
        out_specs=[
            pl.BlockSpec((tm, ATTN_W + KV_W), lambda i: (i, 0)),
            pl.BlockSpec((N_KV_HEADS * (HEAD_DIM + FLASH_ONES_ROWS), tm), lambda i: (0, i)),
        ],
        out_shape=[
            jax.ShapeDtypeStruct((t, ATTN_W + KV_W), BF16),
            jax.ShapeDtypeStruct((N_KV_HEADS * (HEAD_DIM + FLASH_ONES_ROWS), t), BF16),
        ],
        compiler_params=_params("parallel"),
        name="attn_prep",
    )(u, u, u, cos_l, sin_l, q_gain.reshape(1, HEAD_DIM), k_gain.reshape(1, HEAD_DIM))


def _flash_kernel(shift_ref, q_ref, k_ref, vt_ref, o_ref, m_ref, acc_ref, *, tk, sub, fixed_shift):
    ki = pl.program_id(3)

    @pl.when(ki == 0)
    def _():
        m_ref[...] = jnp.full_like(m_ref, -jnp.inf)
        acc_ref[...] = jnp.zeros_like(acc_ref)

    stages = [(s0, g) for s0 in range(0, tk, sub) for g in range(Q_GROUP)]

    def scores(stage):
        s0, g = stage
        return _dot_nt(k_ref[s0:s0 + sub, :], q_ref[:, g * HEAD_DIM:(g + 1) * HEAD_DIM])

    shift = shift_ref[0, 0]
    partial = [None] * Q_GROUP

    pending = [scores(st) for st in stages[:FLASH_LOOKAHEAD]]
    for i, (s0, g) in enumerate(stages):
        s = pending.pop(0)
        if i + FLASH_LOOKAHEAD < len(stages):
            pending.append(scores(stages[i + FLASH_LOOKAHEAD]))
        vt1 = vt_ref[:, s0:s0 + sub]
        if fixed_shift:
            pv = _dot(vt1, jnp.exp2(s - shift).astype(BF16))
            partial[g] = pv if partial[g] is None else partial[g] + pv
        else:
            m_prev = m_ref[g]
            m_new = jnp.maximum(m_prev, jnp.max(s, axis=0, keepdims=True))
            alpha = jnp.exp2(m_prev - m_new)
            p = jnp.exp2((s - m_new).astype(BF16))
            acc_ref[g] = alpha * acc_ref[g] + _dot(vt1, p)
            m_ref[g] = m_new
    if fixed_shift:
        for g in range(Q_GROUP):
            acc_ref[g] += partial[g]

    @pl.when(ki == pl.num_programs(3) - 1)
    def _():
        for g in range(Q_GROUP):
            out = acc_ref[g, 0:HEAD_DIM, :] / acc_ref[g, HEAD_DIM:HEAD_DIM + 1, :]
            o_ref[:, g * HEAD_DIM:(g + 1) * HEAD_DIM] = out.T.astype(o_ref.dtype)


def _flash(qk, vt, q_gain, k_gain, batch, n):
    t = qk.shape[0]
    tq, tk = _tile(n, FLASH_TQ), _tile(n, FLASH_TK)
    sub = _tile(tk, FLASH_SUB)
    nq, nk = n // tq, n // tk
    gw = Q_GROUP * HEAD_DIM
    bound = (HEAD_DIM ** 0.5 * LOG2_E) * jnp.max(jnp.abs(q_gain)) * jnp.max(jnp.abs(k_gain))
    shift = bound.astype(F32).reshape(1, 1)

    def call(fixed_shift):
        return pl.pallas_call(
            functools.partial(_flash_kernel, tk=tk, sub=sub, fixed_shift=fixed_shift),
            grid=(batch, N_KV_HEADS, nq, nk),
            in_specs=[
                pl.BlockSpec(memory_space=pltpu.SMEM),
                pl.BlockSpec((tq, gw), lambda b, h, qi, ki: (b * nq + qi, h)),
                pl.BlockSpec((tk, HEAD_DIM), lambda b, h, qi, ki: (b * nk + ki, N_Q_HEADS + h)),
                pl.BlockSpec((HEAD_DIM + FLASH_ONES_ROWS, tk), lambda b, h, qi, ki: (h, b * nk + ki)),
            ],
            out_specs=pl.BlockSpec((tq, gw), lambda b, h, qi, ki: (b * nq + qi, h)),
            out_shape=jax.ShapeDtypeStruct((t, ATTN_W), BF16),
            scratch_shapes=[
                pltpu.VMEM((Q_GROUP, 1, tq), F32),
                pltpu.VMEM((Q_GROUP, HEAD_DIM + FLASH_ONES_ROWS, tq), F32),
            ],
            compiler_params=_params("parallel", "parallel", "parallel", "arbitrary"),
            name="flash_attn_fixed" if fixed_shift else "flash_attn",
        )(shift, qk, qk, vt)

    return lax.cond(2.0 * bound <= FLASH_FIXED_SHIFT_MAX_RANGE, lambda: call(True), lambda: call(False))


def _conv_kernel(a_ref, g_ref, ap_ref, gp_ref, an_ref, gn_ref, dw_ref, dwb_ref, lng_ref, lnb_ref, pw_ref,
                 o_ref, h_ref, sh_ref, *, tm, per_seq):
    i = pl.program_id(0)
    pos = i % per_seq

    def glu(a, g):
        return a.astype(F32) * _sigmoid(g.astype(F32))

    prev = jnp.where(pos == 0, 0.0, glu(ap_ref[...], gp_ref[...]))
    nxt = jnp.where(pos == per_seq - 1, 0.0, glu(an_ref[...], gn_ref[...]))
    h_ref[0:CONV_HALO, :] = prev
    h_ref[CONV_HALO:CONV_HALO + tm, :] = glu(a_ref[...], g_ref[...])
    h_ref[CONV_HALO + tm:, :] = nxt

    base = CONV_HALO - CONV_K // 2
    span = tm + 2 * CONV_HALO - SUBLANES
    for b in range(SUBLANES):
        sh_ref[b] = h_ref[b:b + span, :]
    acc = jnp.zeros((tm, CONV_W), F32) + dwb_ref[...]
    for j in range(CONV_K):
        a, b = divmod(base + j, SUBLANES)
        acc = acc + dw_ref[j:j + 1, :] * sh_ref[b, a * SUBLANES:a * SUBLANES + tm, :]
    mu = jnp.mean(acc, axis=-1, keepdims=True)
    cen = acc - mu
    var = jnp.mean(cen * cen, axis=-1, keepdims=True)
    y = cen * lax.rsqrt(var + NORM_EPS) * lng_ref[...] + lnb_ref[...]
    o_ref[...] = _dot(_silu(y).astype(BF16), pw_ref[...]).astype(o_ref.dtype)


def _conv(u, dw, dwb, lng, lnb, pw, layer, n):
    t = u.shape[0]
    tm = _tile(n, 1024)
    per_seq = n // tm
    hb = tm // CONV_HALO
    last = t // CONV_HALO - 1
    ca, cg = OFF_CA // CONV_W, OFF_CG // CONV_W
    vec = lambda v: v.reshape(1, CONV_W)
    row = pl.BlockSpec((1, CONV_W), lambda i: (0, 0))
    return pl.pallas_call(
        functools.partial(_conv_kernel, tm=tm, per_seq=per_seq),
        grid=(t // tm,),
        in_specs=[
            pl.BlockSpec((tm, CONV_W), lambda i: (i, ca)),
            pl.BlockSpec((tm, CONV_W), lambda i: (i, cg)),
            pl.BlockSpec((CONV_HALO, CONV_W), lambda i: (jnp.maximum(i * hb - 1, 0), ca)),
            pl.BlockSpec((CONV_HALO, CONV_W), lambda i: (jnp.maximum(i * hb - 1, 0), cg)),
            pl.BlockSpec((CONV_HALO, CONV_W), lambda i: (jnp.minimum((i + 1) * hb, last), ca)),
            pl.BlockSpec((CONV_HALO, CONV_W), lambda i: (jnp.minimum((i + 1) * hb, last), cg)),
            pl.BlockSpec((CONV_K, CONV_W), lambda i: (0, 0)),
            row, row, row,
            pl.BlockSpec((None, CONV_W, CONV_W), lambda i: (layer, 0, 0)),
        ],
        out_specs=pl.BlockSpec((tm, CONV_W), lambda i: (i, 0)),
        out_shape=jax.ShapeDtypeStruct((t, CONV_W), BF16),
        scratch_shapes=[
            pltpu.VMEM((tm + 2 * CONV_HALO, CONV_W), F32),
            pltpu.VMEM((SUBLANES, tm + 2 * CONV_HALO - SUBLANES, CONV_W), F32),
        ],
        compiler_params=_params("parallel"),
        name="conv_mixer",
    )(u, u, u, u, u, u, dw, vec(dwb), vec(lng), vec(lnb), pw)


def _hgrn_tables(reverse):
    c = HGRN_CHUNK
    levels = []
    h = c // 2
    while h >= 1:
        levels.append(h)
        h //= 2
    r = np.arange(c)
    col = r[None, :]
    seg = np.zeros((len(levels) + 2, c, c), np.float32)
    mask = np.zeros((len(levels) + 1, c, c), np.float32)
    for li, h in enumerate(levels):
        blk = r // (2 * h)
        upper = (r % (2 * h)) >= h
        b = (blk * 2 * h + h - 1)[:, None]
        rr = r[:, None]
        seg[li] = np.where(upper[:, None], (col > b) & (col <= rr), (col > rr) & (col <= b))
        mask[li] = upper[:, None] & (~upper)[None, :] & (blk[:, None] == blk[None, :])
    seg[-2] = col <= r[:, None]
    seg[-1] = col > r[:, None]
    mask[-1] = np.eye(c)
    if reverse:
        seg = seg[:, ::-1, ::-1]
        mask = mask[:, ::-1, ::-1]
    seg = seg.reshape(-1, c)
    seg2 = np.concatenate([seg, seg], axis=1)
    return (jnp.asarray(seg2, dtype=BF16), jnp.asarray(mask, dtype=F32), len(levels))


def _hgrn_kernel(*refs, reverse, final, n_levels, n_chunks):
    if final:
        hq_ref, hf_ref, hi_ref, lb_ref, seg_ref, mask_ref, oprev_ref, hg_ref, gn_ref, o_ref, st_ref = refs
    else:
        hq_ref, hf_ref, hi_ref, lb_ref, seg_ref, mask_ref, o_ref, st_ref = refs
    c = HGRN_CHUNK

    @pl.when(pl.program_id(1) == 0)
    def _():
        st_ref[...] = jnp.zeros_like(st_ref)

    lb = lb_ref[...]
    seg = seg_ref[...]
    total_row = 0 if reverse else c - 1

    def prep(j):
        rows = slice(j * c, (j + 1) * c)
        f = lb + (1.0 - lb) * _sigmoid(hf_ref[rows, :].astype(F32))
        log2f = jnp.log(f) * LOG2_E
        g_hi = log2f.astype(BF16)
        g_lo = (log2f - g_hi.astype(F32)).astype(BF16)
        e_all = jnp.exp2(_dot(seg, jnp.concatenate([g_hi, g_lo], axis=0)))
        decay = e_all[n_levels * c + total_row:n_levels * c + total_row + 1, :]
        return _silu(hq_ref[rows, :].astype(F32)).astype(BF16), (1.0 - f).astype(BF16), e_all.astype(BF16), decay

    def intra(prepped):
        qq, kk, e_all, _ = prepped
        mats = []
        for hd in range(HGRN_HEADS):
            sl = slice(hd * HGRN_D, (hd + 1) * HGRN_D)
            qh, kh = qq[:, sl], kk[:, sl]
            a = mask_ref[n_levels] * _dot_nt(qh, kh)
            for l in range(n_levels):
                el = e_all[l * c:(l + 1) * c, sl]
                a = a + mask_ref[l] * _dot_nt(qh * el, kh * el)
            mats.append(a.astype(BF16))
        return mats

    def outputs(j, prepped, mats):
        qq, kk, e_all, decay = prepped
        rows = slice(j * c, (j + 1) * c)
        if final:
            gate = _silu(hg_ref[rows, :].astype(F32)) * gn_ref[...]
        for hd in range(HGRN_HEADS):
            sl = slice(hd * HGRN_D, (hd + 1) * HGRN_D)
            qh, kh, vh = qq[:, sl], kk[:, sl], hi_ref[rows, sl]
            e_pre = e_all[n_levels * c:(n_levels + 1) * c, sl]
            e_suf = e_all[(n_levels + 1) * c:(n_levels + 2) * c, sl]
            st = st_ref[hd]
            o = _dot(mats[hd], vh) + _dot_nt(qh * e_pre, st.astype(BF16))
            v_t = vh.astype(F32).T.astype(BF16)
            st_ref[hd] = decay[:, sl] * st + _dot(v_t, kh * e_suf)
            if final:
                o = o + oprev_ref[rows, sl]
                o = o * lax.rsqrt(jnp.mean(o * o, axis=-1, keepdims=True) + NORM_EPS)
                o_ref[rows, sl] = (o * gate[:, sl]).astype(o_ref.dtype)
            else:
                o_ref[rows, sl] = o

    order = list(range(n_chunks))[::-1] if reverse else list(range(n_chunks))
    prepped, mats = {}, {}
    for step in range(n_chunks + 2):
        if step < n_chunks:
            prepped[step] = prep(order[step])
        if 0 <= step - 1 < n_chunks:
            mats[step - 1] = intra(prepped[step - 1])
        if 0 <= step - 2 < n_chunks:
            outputs(order[step - 2], prepped.pop(step - 2), mats.pop(step - 2))


def _hgrn_pass(u, lb, batch, n, *, reverse, o_prev=None, gn=None):
    t = u.shape[0]
    n_chunks = _tile(n // HGRN_CHUNK, HGRN_CHUNKS_PER_STEP)
    c = n_chunks * HGRN_CHUNK
    nc = n // c
    final = o_prev is not None
    seg, mask, n_levels = _hgrn_tables(reverse)

    def rows(b, ci):
        return b * nc + (nc - 1 - ci if reverse else ci)

    def col_spec(off):
        return pl.BlockSpec((c, HGRN_W), lambda b, ci: (rows(b, ci), off // HGRN_W))

    const2 = lambda b, ci: (0, 0)
    in_specs = [
        col_spec(OFF_HQ),
        col_spec(OFF_FB if reverse else OFF_FF),
        col_spec(OFF_HI),
        pl.BlockSpec((1, HGRN_W), const2),
        pl.BlockSpec(seg.shape, const2),
        pl.BlockSpec(mask.shape, lambda b, ci: (0, 0, 0)),
    ]
    args = [u, u, u, lb.reshape(1, HGRN_W), seg, mask]
    if final:
        in_specs += [
            pl.BlockSpec((c, HGRN_W), lambda b, ci: (rows(b, ci), 0)),
            col_spec(OFF_HG),
            pl.BlockSpec((1, HGRN_W), const2),
        ]
        args += [o_prev, u, gn.reshape(1, HGRN_W)]
    return pl.pallas_call(
        functools.partial(_hgrn_kernel, reverse=reverse, final=final, n_levels=n_levels, n_chunks=n_chunks),
        grid=(batch, nc),
        in_specs=in_specs,
        out_specs=pl.BlockSpec((c, HGRN_W), lambda b, ci: (rows(b, ci), 0)),
        out_shape=jax.ShapeDtypeStruct((t, HGRN_W), BF16 if final else F32),
        scratch_shapes=[pltpu.VMEM((HGRN_HEADS, HGRN_D, HGRN_D), F32)],
        compiler_params=_params("parallel", "arbitrary"),
        name="hgrn_bwd" if reverse else "hgrn_fwd",
    )(*args)


def _out_proj_kernel(a_ref, c_ref, h_ref, x_ref, w_ref, g_ref, gnext_ref, o_ref, xn_ref):
    halves = _row_halves(x_ref.shape[0])
    mixes = []
    for rows in halves:
        mix = _dot(a_ref[rows, :], w_ref[0:ATTN_W, :])
        mix = mix + _dot(c_ref[rows, :], w_ref[ATTN_W:ATTN_W + CONV_W, :])
        mixes.append(mix + _dot(h_ref[rows, :], w_ref[ATTN_W + CONV_W:, :]))
    for rows, mix in zip(halves, mixes):
        x = x_ref[rows, :] + _rms(mix, g_ref[...])
        o_ref[rows, :] = x
        xn_ref[rows, :] = _rms(x, gnext_ref[...]).astype(xn_ref.dtype)


def _out_proj(a, cv, hg, x, w, layer, g, g_next):
    t, d = x.shape
    tm = _tile(t, 512)
    mixw = w.shape[1]
    return pl.pallas_call(
        _out_proj_kernel,
        grid=(t // tm,),
        in_specs=[
            pl.BlockSpec((tm, ATTN_W), lambda i: (i, 0)),
            pl.BlockSpec((tm, CONV_W), lambda i: (i, 0)),
            pl.BlockSpec((tm, HGRN_W), lambda i: (i, 0)),
            pl.BlockSpec((tm, d), lambda i: (i, 0)),
            pl.BlockSpec((None, mixw, d), lambda i: (layer, 0, 0)),
            pl.BlockSpec((1, d), lambda i: (0, 0)),
            pl.BlockSpec((1, d), lambda i: (0, 0)),
        ],
        out_specs=[pl.BlockSpec((tm, d), lambda i: (i, 0)), pl.BlockSpec((tm, d), lambda i: (i, 0))],
        out_shape=[jax.ShapeDtypeStruct((t, d), F32), jax.ShapeDtypeStruct((t, d), BF16)],
        compiler_params=_params("parallel"),
        name="out_proj",
    )(a, cv, hg, x, w, g.reshape(1, d), g_next.reshape(1, d))


def _ffn_kernel(xn_ref, wg_ref, wu_ref, wo_ref, gpost_ref, o_ref, acc_ref):
    j = pl.program_id(1)

    def down():
        xn = xn_ref[...]
        hidden = _silu(_dot(xn, wg_ref[...])) * _dot(xn, wu_ref[...])
        return _dot(hidden.astype(BF16), wo_ref[...])

    @pl.when(j == 0)
    def _():
        acc_ref[...] = down()

    @pl.when(j > 0)
    def _():
        acc_ref[...] += down()

    @pl.when(j == pl.num_programs(1) - 1)
    def _():
        o_ref[...] = _rms(acc_ref[...], gpost_ref[...]).astype(o_ref.dtype)


def _ffn(xn, w_in, w_out, layer, gpost):
    t, d = xn.shape
    f = w_out.shape[1]
    tm, th = _tile(t, 1024), _tile(f, 512)
    nh = f // th
    return pl.pallas_call(
        _ffn_kernel,
        grid=(t // tm, nh),
        in_specs=[
            pl.BlockSpec((tm, d), lambda i, j: (i, 0)),
            pl.BlockSpec((None, d, th), lambda i, j: (layer, 0, j)),
            pl.BlockSpec((None, d, th), lambda i, j: (layer, 0, nh + j)),
            pl.BlockSpec((None, th, d), lambda i, j: (layer, j, 0)),
            pl.BlockSpec((1, d), lambda i, j: (0, 0)),
        ],
        out_specs=pl.BlockSpec((tm, d), lambda i, j: (i, 0)),
        out_shape=jax.ShapeDtypeStruct((t, d), BF16),
        scratch_shapes=[pltpu.VMEM((tm, d), F32)],
        compiler_params=_params("parallel", "arbitrary"),
        name="ffn",
    )(xn, w_in, w_in, w_out, gpost.reshape(1, d))


def _ple_kernel(x_ref, y_ref, p_ref, wg_ref, wp_ref, g_ref, o_ref):
    halves = _row_halves(x_ref.shape[0])
    parts = []
    for rows in halves:
        x = x_ref[rows, :] + y_ref[rows, :].astype(F32)
        parts.append((x, _dot(x.astype(BF16), wg_ref[...]), _dot(p_ref[rows, :].astype(BF16), wp_ref[...])))
    for rows, (x, gate_logit, proj) in zip(halves, parts):
        o_ref[rows, :] = x + _rms(proj * _sigmoid(gate_logit), g_ref[...])


def _ple(x, y, p, layer, w_gate, w_proj, g):
    t, d = x.shape
    pd = p.shape[-1]
    tm = _tile(t, 512)
    return pl.pallas_call(
        _ple_kernel,
        grid=(t // tm,),
        in_specs=[
            pl.BlockSpec((tm, d), lambda i: (i, 0)),
            pl.BlockSpec((tm, d), lambda i: (i, 0)),
            pl.BlockSpec((None, tm, pd), lambda i: (layer, i, 0)),
            pl.BlockSpec((None, d, d), lambda i: (layer, 0, 0)),
            pl.BlockSpec((None, pd, d), lambda i: (layer, 0, 0)),
            pl.BlockSpec((1, d), lambda i: (0, 0)),
        ],
        out_specs=pl.BlockSpec((tm, d), lambda i: (i, 0)),
        out_shape=jax.ShapeDtypeStruct((t, d), F32),
        compiler_params=_params("parallel"),
        name="ple",
    )(x, y, p, w_gate, w_proj, g.reshape(1, d))


def _trunk(x, p, lb, layers, mats):
    depth = p.shape[0]
    batch, n, d = x.shape
    t = batch * n
    x = x.reshape(t, d)
    p = p.reshape(depth, t, p.shape[-1])
    cos_l, sin_l = _rope_tables(n)
    for l, w in enumerate(layers):
        u = _in_proj(x, w["norm_mix_pre"], mats["w_in"], l)
        qk_r, v_t = _attn_prep(u, cos_l, sin_l, w["q_norm"], w["k_norm"], n)
        a_out = _flash(qk_r, v_t, w["q_norm"], w["k_norm"], batch, n)
        c_out = _conv(u, w["conv_dw"], w["conv_dw_b"], w["conv_ln_g"], w["conv_ln_b"], mats["conv_pw"], l, n)
        o_fwd = _hgrn_pass(u, lb[l, 0], batch, n, reverse=False)
        h_out = _hgrn_pass(u, lb[l, 1], batch, n, reverse=True, o_prev=o_fwd, gn=w["hgrn_gn"])
        x, xn = _out_proj(a_out, c_out, h_out, x, mats["w_out"], l, w["norm_mix_post"], w["norm_ffn_pre"])
        y = _ffn(xn, mats["w_ffn_in"], mats["w_ffn_out"], l, w["norm_ffn_post"])
        x = _ple(x, y, p, l, mats["w_ple_gate"], mats["w_ple_proj"], w["ple_norm"])
    return x.reshape(batch, n, d)


def kernel(x_prompt, x_sample, p_prompt, p_sample, norm_mix_pre, norm_mix_post, w_in, q_norm, k_norm, conv_dw, conv_dw_b, conv_ln_g, conv_ln_b, conv_pw, hgrn_lb, hgrn_gn, w_out, norm_ffn_pre, norm_ffn_post, w_ffn_in, w_ffn_out, w_ple_gate, w_ple_proj, ple_norm):
    sm = jax.nn.softmax(hgrn_lb.astype(F32), axis=0)
    lb = jnp.cumsum(sm, axis=0) - sm[0]
    vectors = dict(
        norm_mix_pre=norm_mix_pre, norm_mix_post=norm_mix_post, q_norm=q_norm, k_norm=k_norm,
        conv_dw=conv_dw, conv_dw_b=conv_dw_b, conv_ln_g=conv_ln_g, conv_ln_b=conv_ln_b,
        hgrn_gn=hgrn_gn, norm_ffn_pre=norm_ffn_pre, norm_ffn_post=norm_ffn_post, ple_norm=ple_norm,
    )
    matrices = dict(
        w_in=w_in, conv_pw=conv_pw, w_out=w_out, w_ffn_in=w_ffn_in, w_ffn_out=w_ffn_out,
        w_ple_gate=w_ple_gate, w_ple_proj=w_ple_proj,
    )
    mats = {k: v.astype(BF16) for k, v in matrices.items()}
    layers = [{k: v[l] for k, v in vectors.items()} for l in range(p_prompt.shape[0])]
    y_prompt = _trunk(x_prompt, p_prompt, lb, layers, mats)
    y_sample = _trunk(x_sample, p_sample, lb, layers, mats)
    return (y_prompt, y_sample)
```
